```python
import jax, jax.numpy as jnp
from jax import lax
import numpy as np

D_MODEL = 1024
BATCH = 16
SEQ = 256
DEPTH = 2
DEC_BATCH = 8
DEC_SEQ = 4096
PAST_LEN = 512

GRID_W = 64
MIXER_ORDER = ('rglru', 'pool')
N_MIXERS = len(MIXER_ORDER)
D_RNN = D_MODEL
N_LRU_BLOCKS = 4
LRU_BLOCK = D_RNN // N_LRU_BLOCKS
CONV_W = 4
CONV_LEFT = 2
LRU_C = 8.0
POOL_WINDOWS = (2, 4, 8, 16)
N_POOL_GROUPS = len(POOL_WINDOWS)
POOL_GROUP = D_MODEL // N_POOL_GROUPS
D_FF = -(-8 * D_MODEL // (3 * 256)) * 256
N_MOD = 6
EPS = 1e-6
POS_THETA = 10000.0

kernel_name = 'hybrid_rglru_pool_diffusion_step'


def rms_norm(x, g):
    x32 = x.astype(jnp.float32)
    y = x32 * lax.rsqrt(jnp.mean(x32 * x32, axis=-1, keepdims=True) + EPS)
    return (y * g.astype(jnp.float32)).astype(x.dtype)


def modulation(cond, w, b):
    m = jax.nn.silu(cond) @ w + b
    return jnp.split(m[:, None, :], N_MOD, axis=-1)


def grid_pos_emb(t_len, dtype):
    rows = t_len // GRID_W
    r = jnp.repeat(jnp.arange(rows), GRID_W)
    col = jnp.tile(jnp.arange(GRID_W), rows)
    quarter = D_MODEL // 4
    omega = 1.0 / (POS_THETA ** (jnp.arange(quarter, dtype=jnp.float32) / quarter))

    def emb(p):
        ang = p.astype(jnp.float32)[:, None] * omega[None, :]
        return jnp.concatenate([jnp.sin(ang), jnp.cos(ang)], axis=-1)

    return jnp.concatenate([emb(r), emb(col)], axis=-1).astype(dtype)


def centred_depthwise_conv(u, w, b):
    t_len = u.shape[1]
    up = jnp.pad(u, ((0, 0), (CONV_LEFT, CONV_W - 1 - CONV_LEFT), (0, 0)))
    y = up[:, 0:t_len] * w[0]
    for k in range(1, CONV_W):
        y = y + up[:, k:k + t_len] * w[k]
    return y + b


def _lin_combine(left, right):
    a1, b1 = left
    a2, b2 = right
    return a1 * a2, a2 * b1 + b2


def rglru_direction(u, wa, ba, wx, bx, lam, h0, reverse):
    bsz, t_len, _ = u.shape
    u32 = u.astype(jnp.float32)
    ub = u32.reshape(bsz, t_len, N_LRU_BLOCKS, LRU_BLOCK)
    ra = jnp.einsum('btnd,nde->btne', ub, wa.astype(jnp.float32)).reshape(bsz, t_len, D_RNN)
    ix = jnp.einsum('btnd,nde->btne', ub, wx.astype(jnp.float32)).reshape(bsz, t_len, D_RNN)
    r = jax.nn.sigmoid(ra + ba.astype(jnp.float32))
    i = jax.nn.sigmoid(ix + bx.astype(jnp.float32))
    log_a = -LRU_C * r * jax.nn.softplus(-lam.astype(jnp.float32))
    a = jnp.exp(log_a)
    b = jnp.sqrt(-jnp.expm1(2.0 * log_a)) * (i * u32)
    a_cum, b_cum = lax.associative_scan(_lin_combine, (a, b), reverse=reverse, axis=1)
    h = a_cum * h0.astype(jnp.float32)[:, None, :] + b_cum
    h_last = h[:, 0] if reverse else h[:, -1]
    return h, h_last


def rglru_mixer(h, h0, w_in, conv_w, conv_b, wa, ba, wx, bx, lam, w_out):
    z = h @ w_in
    gate_br, rec = jnp.split(z, 2, axis=-1)
    u = centred_depthwise_conv(rec, conv_w, conv_b)
    hf, sf = rglru_direction(u, wa[0], ba[0], wx[0], bx[0], lam[0], h0[:, 0], False)
    hb, sb = rglru_direction(u, wa[1], ba[1], wx[1], bx[1], lam[1], h0[:, 1], True)
    mixed = (hf + hb) * jax.nn.gelu(gate_br.astype(jnp.float32))
    y = mixed.astype(h.dtype) @ w_out
    return y, jnp.stack([sf, sb], axis=1).astype(h.dtype)


def centred_window_mean(u32, win):
    t_len = u32.shape[1]
    cs = jnp.pad(jnp.cumsum(u32, axis=1), ((0, 0), (1, 0), (0, 0)))
    t = jnp.arange(t_len)
    lo = jnp.clip(t - win // 2, 0, t_len)
    hi = jnp.clip(t + win // 2, 0, t_len)
    cnt = (hi - lo).astype(jnp.float32)
    return (cs[:, hi] - cs[:, lo]) / cnt[None, :, None]


def pool_mixer(h, pool_w, pool_b, pool_scale):
    bsz, t_len, _ = h.shape
    groups = h.astype(jnp.float32).reshape(bsz, t_len, N_POOL_GROUPS, POOL_GROUP)
    pooled = jnp.stack(
        [centred_window_mean(groups[:, :, gi], win) - groups[:, :, gi]
         for gi, win in enumerate(POOL_WINDOWS)], axis=2)
    y = jnp.einsum('btgc,gcd->btgd', pooled, pool_w.astype(jnp.float32)).reshape(bsz, t_len, D_MODEL)
    y = (y + pool_b.astype(jnp.float32)) * pool_scale.astype(jnp.float32)
    return y.astype(h.dtype)


def swiglu(h, w_in, w_out):
    gate, up = jnp.split(h @ w_in, 2, axis=-1)
    return (jax.nn.silu(gate) * up) @ w_out


def layer_forward(x, cond, p, h0):
    sh1, sc1, g1, sh2, sc2, g2 = modulation(cond, p['mod_w'], p['mod_b'])
    h = rms_norm(x, p['mix_pre_g']) * (1.0 + sc1) + sh1
    state = None
    if p['kind'] == 'rglru':
        y, state = rglru_mixer(h, h0, p['w_in'], p['conv_w'], p['conv_b'], p['gate_a_w'], p['gate_a_b'],
                               p['gate_x_w'], p['gate_x_b'], p['lam'], p['w_out'])
    else:
        y = pool_mixer(h, p['pool_w'], p['pool_b'], p['pool_scale'])
    x = x + g1 * rms_norm(y, p['mix_post_g'])
    h = rms_norm(x, p['ffn_pre_g']) * (1.0 + sc2) + sh2
    x = x + g2 * rms_norm(swiglu(h, p['ffn_w_in'], p['ffn_w_out']), p['ffn_post_g'])
    return x, state


def setup_inputs(seed: int = 0) -> dict:
    key = jax.random.key(seed)
    ks = iter(jax.random.split(key, 48))
    d = D_MODEL

    def nrm(shape, scale):
        return jax.random.normal(next(ks), shape, jnp.float32) * scale

    def gain(shape):
        return 1.0 + nrm(shape, 0.05)

    def lru_lambda():
        u = jax.random.uniform(next(ks), (2, D_RNN), jnp.float32, 0.9, 0.999)
        a0 = u ** (1.0 / LRU_C)
        return jnp.log(a0) - jnp.log1p(-a0)

    inp = {}
    inp['x_prompt'] = nrm((BATCH, SEQ, d), 1.0)
    inp['x_sample'] = nrm((DEC_BATCH, DEC_SEQ, d), 1.0)
    inp['state_l0_rglru'] = nrm((DEC_BATCH, 2, D_RNN), 0.5)
    inp['c'] = nrm((DEC_BATCH, d), 1.0)
    inp['c_ctx'] = nrm((d,), 1.0)
    inp['l0_mod_w'] = nrm((d, N_MOD * d), 0.5 * d ** -0.5)
    inp['l0_mod_b'] = nrm((N_MOD * d,), 0.01)
    inp['l0_mix_pre_g'] = gain((d,))
    inp['l0_mix_post_g'] = gain((d,))
    inp['l0_w_in'] = nrm((d, 2 * D_RNN), d ** -0.5)
    inp['l0_conv_w'] = nrm((CONV_W, D_RNN), CONV_W ** -0.5)
    inp['l0_conv_b'] = nrm((D_RNN,), 0.01)
    inp['l0_gate_a_w'] = nrm((2, N_LRU_BLOCKS, LRU_BLOCK, LRU_BLOCK), LRU_BLOCK ** -0.5)
    inp['l0_gate_a_b'] = nrm((2, D_RNN), 0.01)
    inp['l0_gate_x_w'] = nrm((2, N_LRU_BLOCKS, LRU_BLOCK, LRU_BLOCK), LRU_BLOCK ** -0.5)
    inp['l0_gate_x_b'] = nrm((2, D_RNN), 0.01)
    inp['l0_lambda'] = lru_lambda()
    inp['l0_w_out'] = nrm((D_RNN, d), D_RNN ** -0.5)
    inp['l0_ffn_pre_g'] = gain((d,))
    inp['l0_ffn_post_g'] = gain((d,))
    inp['l0_ffn_w_in'] = nrm((d, 2 * D_FF), d ** -0.5)
    inp['l0_ffn_w_out'] = nrm((D_FF, d), D_FF ** -0.5)
    inp['l1_mod_w'] = nrm((d, N_MOD * d), 0.5 * d ** -0.5)
    inp['l1_mod_b'] = nrm((N_MOD * d,), 0.01)
    inp['l1_mix_pre_g'] = gain((d,))
    inp['l1_mix_post_g'] = gain((d,))
    inp['l1_pool_w'] = nrm((N_POOL_GROUPS, POOL_GROUP, POOL_GROUP), POOL_GROUP ** -0.5)
    inp['l1_pool_b'] = nrm((d,), 0.01)
    inp['l1_pool_scale'] = gain((d,))
    inp['l1_ffn_pre_g'] = gain((d,))
    inp['l1_ffn_post_g'] = gain((d,))
    inp['l1_ffn_w_in'] = nrm((d, 2 * D_FF), d ** -0.5)
    inp['l1_ffn_w_out'] = nrm((D_FF, d), D_FF ** -0.5)
    return inp


def reference(x_prompt, x_sample, state_l0_rglru, c, c_ctx,
              l0_mod_w, l0_mod_b, l0_mix_pre_g, l0_mix_post_g, l0_w_in, l0_conv_w, l0_conv_b,
              l0_gate_a_w, l0_gate_a_b, l0_gate_x_w, l0_gate_x_b, l0_lambda, l0_w_out,
              l0_ffn_pre_g, l0_ffn_post_g, l0_ffn_w_in, l0_ffn_w_out,
              l1_mod_w, l1_mod_b, l1_mix_pre_g, l1_mix_post_g, l1_pool_w, l1_pool_b, l1_pool_scale,
              l1_ffn_pre_g, l1_ffn_post_g, l1_ffn_w_in, l1_ffn_w_out):
    layers = [
        dict(kind=MIXER_ORDER[0 % N_MIXERS], mod_w=l0_mod_w, mod_b=l0_mod_b,
             mix_pre_g=l0_mix_pre_g, mix_post_g=l0_mix_post_g, w_in=l0_w_in,
             conv_w=l0_conv_w, conv_b=l0_conv_b, gate_a_w=l0_gate_a_w, gate_a_b=l0_gate_a_b,
             gate_x_w=l0_gate_x_w, gate_x_b=l0_gate_x_b, lam=l0_lambda, w_out=l0_w_out,
             ffn_pre_g=l0_ffn_pre_g, ffn_post_g=l0_ffn_post_g,
             ffn_w_in=l0_ffn_w_in, ffn_w_out=l0_ffn_w_out),
        dict(kind=MIXER_ORDER[1 % N_MIXERS], mod_w=l1_mod_w, mod_b=l1_mod_b,
             mix_pre_g=l1_mix_pre_g, mix_post_g=l1_mix_post_g, pool_w=l1_pool_w,
             pool_b=l1_pool_b, pool_scale=l1_pool_scale,
             ffn_pre_g=l1_ffn_pre_g, ffn_post_g=l1_ffn_post_g,
             ffn_w_in=l1_ffn_w_in, ffn_w_out=l1_ffn_w_out),
    ]
    cached_states = {0: state_l0_rglru}

    y_prompt = x_prompt
    ctx_cond = c_ctx[None, :]
    new_states = {}
    for i in range(DEPTH):
        p = layers[i]
        h0 = jnp.zeros((y_prompt.shape[0], 2, D_RNN), y_prompt.dtype) if p['kind'] == 'rglru' else None
        y_prompt, st = layer_forward(y_prompt, ctx_cond, p, h0)
        if st is not None:
            new_states[i] = st

    y_sample = x_sample + grid_pos_emb(x_sample.shape[1], x_sample.dtype)[None]
    for i in range(DEPTH):
        p = layers[i]
        h0 = cached_states[i] if p['kind'] == 'rglru' else None
        y_sample, _ = layer_forward(y_sample, c, p, h0)

    return (y_prompt, y_sample, new_states[0])
```

```python
import functools

import jax
import jax.numpy as jnp
from jax import lax
from jax.experimental import pallas as pl
from jax.experimental.pallas import tpu as pltpu

D_MODEL = 1024
D_RNN = D_MODEL
N_LRU_BLOCKS = 4
LRU_BLOCK = D_RNN // N_LRU_BLOCKS
CONV_W = 4
LRU_C = 8.0
POOL_WINDOWS = (2, 4, 8, 16)
POOL_GROUP = D_MODEL // len(POOL_WINDOWS)
D_FF = 2816
N_MOD = 6
EPS = 1e-6
POS_THETA = 10000.0
GRID_W = 64

HALO = 8
FF_CHUNK = 256
N_FF_CHUNKS = D_FF // FF_CHUNK
MOD_ROWS = 16
CTX_ROW = 8
MOD_TN = 1536

VMEM_LIMIT = 56 * 1024 * 1024

F32 = jnp.float32
BF16 = jnp.bfloat16


def _dot(a, b):
    return jnp.dot(a, b, preferred_element_type=F32)


def _rms(x):
    return x * lax.rsqrt(jnp.mean(x * x, axis=-1, keepdims=True) + EPS)


def _sigmoid(x):
    return 1.0 / (1.0 + jnp.exp(-x))


def _mod_body(cond_ref, w_ref, b_ref, o_ref):
    c = cond_ref[...]
    s = (c * _sigmoid(c)).astype(BF16)
    o_ref[...] = _dot(s, w_ref[...].astype(BF16)) + b_ref[...]


def _modulation(cond, w, b):
    n = N_MOD * D_MODEL
    out = pl.pallas_call(
        _mod_body,
        grid=(n // MOD_TN,),
        in_specs=[
            pl.BlockSpec((MOD_ROWS, D_MODEL), lambda j: (0, 0)),
            pl.BlockSpec((D_MODEL, MOD_TN), lambda j: (0, j)),
            pl.BlockSpec((1, MOD_TN), lambda j: (0, j)),
        ],
        out_specs=pl.BlockSpec((MOD_ROWS, MOD_TN), lambda j: (0, j)),
        out_shape=jax.ShapeDtypeStruct((MOD_ROWS, n), F32),
        compiler_params=pltpu.CompilerParams(
            dimension_semantics=("parallel",), vmem_limit_bytes=VMEM_LIMIT),
        name="modulation",
    )(cond, w, b.reshape(1, n))
    return out.reshape(MOD_ROWS, N_MOD, D_MODEL)


def _inproj_body(*refs, has_pos):
    if has_pos:
        x_ref, pos_ref, mod_ref, g_ref, w_ref, gg_ref, rec_ref = refs
        x = x_ref[0] + pos_ref[...]
    else:
        x_ref, mod_ref, g_ref, w_ref, gg_ref, rec_ref = refs
        x = x_ref[0]
    sh = mod_ref[0, 0:1, :]
    sc = mod_ref[0, 1:2, :]
    h = _rms(x) * (g_ref[...] * (1.0 + sc)) + sh
    z = _dot(h.astype(BF16), w_ref[...])
    gg_ref[0] = jax.nn.gelu(z[:, :D_RNN])
    rec_ref[0] = z[:, D_RNN:]


def _conv(rec, prev, nxt, cw_ref, cb_ref):
    ext = jnp.concatenate([prev, rec, nxt], axis=0)
    n = ext.shape[0]
    u = (pltpu.roll(ext, 2, 0) * cw_ref[0:1, :] + pltpu.roll(ext, 1, 0) * cw_ref[1:2, :]
         + ext * cw_ref[2:3, :] + pltpu.roll(ext, n - 1, 0) * cw_ref[3:4, :])
    return u[HALO:n - HALO] + cb_ref[...]


def _gate_ab(u, wg_ref, bg_ref, lam_ref):
    ub = u.astype(BF16)
    ra, ix = [], []
    for n in range(N_LRU_BLOCKS):
        o = _dot(ub[:, n * LRU_BLOCK:(n + 1) * LRU_BLOCK], wg_ref[n])
        ra.append(o[:, :LRU_BLOCK])
        ix.append(o[:, LRU_BLOCK:])
    r = _sigmoid(jnp.concatenate(ra, axis=1) + bg_ref[0:1, :])
    i = _sigmoid(jnp.concatenate(ix, axis=1) + bg_ref[1:2, :])
    nl = -lam_ref[...]
    softplus = jnp.maximum(nl, 0.0) + jnp.log1p(jnp.exp(-jnp.abs(nl)))
    log_a = r * (-LRU_C * softplus)
    a = jnp.exp(log_a)
    b = jnp.sqrt(-jnp.tanh(log_a) * (1.0 + a * a)) * (i * u)
    return a, b


def _chunk_scan(a, b, h_in, reverse):
    tc = a.shape[0]
    row = lax.broadcasted_iota(jnp.int32, a.shape, 0)
    d = 1
    while d < tc:
        if reverse:
            valid = row < tc - d
            a_sh = pltpu.roll(a, tc - d, 0)
            b_sh = pltpu.roll(b, tc - d, 0)
        else:
            valid = row >= d
            a_sh = pltpu.roll(a, d, 0)
            b_sh = pltpu.roll(b, d, 0)
        a_sh = jnp.where(valid, a_sh, 1.0)
        b_sh = jnp.where(valid, b_sh, 0.0)
        b = a * b_sh + b
        a = a * a_sh
        d *= 2
    h = a * h_in + b
    h_last = h[0:1, :] if reverse else h[tc - 1:tc, :]
    return h, h_last


def _scan_fwd_body(rec_ref, prev_ref, next_ref, cw_ref, cb_ref, wg_ref, bg_ref, lam_ref, h0_ref,
                   hf_ref, sf_ref, carry_ref, *, nc):
    c = pl.program_id(1)

    @pl.when(c == 0)
    def _():
        carry_ref[...] = h0_ref[0]

    prev = jnp.where(c > 0, prev_ref[0], 0.0)
    nxt = jnp.where(c < nc - 1, next_ref[0], 0.0)
    u = _conv(rec_ref[0], prev, nxt, cw_ref, cb_ref)
    a, b = _gate_ab(u, wg_ref, bg_ref, lam_ref)
    h, h_last = _chunk_scan(a, b, carry_ref[...], False)
    hf_ref[0] = h
    carry_ref[...] = h_last
    sf_ref[0] = h_last


def _scan_bwd_body(*refs, nc, has_pos):
    if has_pos:
        (rec_ref, prev_ref, next_ref, gg_ref, hf_ref, x_ref, pos_ref, mod_ref, cw_ref, cb_ref, wg_ref,
         bg_ref, lam_ref, h0_ref, wo_ref, pg_ref, x1_ref, sb_ref, carry_ref) = refs
    else:
        (rec_ref, prev_ref, next_ref, gg_ref, hf_ref, x_ref, mod_ref, cw_ref, cb_ref, wg_ref,
         bg_ref, lam_ref, h0_ref, wo_ref, pg_ref, x1_ref, sb_ref, carry_ref) = refs
    c = pl.program_id(1)
    ci = nc - 1 - c

    @pl.when(c == 0)
    def _():
        carry_ref[...] = h0_ref[0]

    prev = jnp.where(ci > 0, prev_ref[0], 0.0)
    nxt = jnp.where(ci < nc - 1, next_ref[0], 0.0)
    u = _conv(rec_ref[0], prev, nxt, cw_ref, cb_ref)
    a, b = _gate_ab(u, wg_ref, bg_ref, lam_ref)
    hb, h_last = _chunk_scan(a, b, carry_ref[...], True)
    carry_ref[...] = h_last
    sb_ref[0] = h_last

    mixed = ((hf_ref[0] + hb) * gg_ref[0]).astype(BF16)
    y = _dot(mixed, wo_ref[...])
    x = x_ref[0]
    if has_pos:
        x = x + pos_ref[...]
    g1 = mod_ref[0, 2:3, :]
    x1_ref[0] = x + _rms(y) * (pg_ref[...] * g1)


def _const_spec(shape):
    nd = len(shape)
    return pl.BlockSpec(shape, lambda *_: (0,) * nd)


def _rglru_layer(x, pos, mod, mod_row, h0, p, tc):
    bsz, t_len, d = x.shape
    nc = t_len // tc
    hb8 = tc // HALO
    n_hblk = t_len // HALO
    has_pos = pos is not None
    act = jax.ShapeDtypeStruct((bsz, t_len, d), F32)
    st = jax.ShapeDtypeStruct((bsz, 1, d), F32)
    cp = pltpu.CompilerParams(dimension_semantics=("parallel", "arbitrary"), vmem_limit_bytes=VMEM_LIMIT)

    row_spec = pl.BlockSpec((1, d), lambda b, c: (0, 0))
    mod_spec = pl.BlockSpec((1, N_MOD, d), lambda b, c: (mod_row(b), 0, 0))

    chunk = pl.BlockSpec((1, tc, d), lambda b, c: (b, c, 0))
    in_specs = [chunk] + ([pl.BlockSpec((tc, d), lambda b, c: (c, 0))] if has_pos else []) + [
        mod_spec, row_spec, _const_spec((d, 2 * D_RNN))]
    args = [x] + ([pos] if has_pos else []) + [mod, p['mix_pre_g'], p['w_in']]
    gg, rec = pl.pallas_call(
        functools.partial(_inproj_body, has_pos=has_pos),
        grid=(bsz, nc), in_specs=in_specs, out_specs=[chunk, chunk], out_shape=[act, act],
        compiler_params=cp, name="l0_inproj",
    )(*args)

    gate_specs = [_const_spec((CONV_W, d)), row_spec, _const_spec((N_LRU_BLOCKS, LRU_BLOCK, 2 * LRU_BLOCK)),
                  _const_spec((2, d)), row_spec]

    def halo_specs(pos_of):
        return [pl.BlockSpec((1, tc, d), lambda b, c: (b, pos_of(c), 0)),
                pl.BlockSpec((1, HALO, d), lambda b, c: (b, jnp.maximum(pos_of(c) * hb8 - 1, 0), 0)),
                pl.BlockSpec((1, HALO, d), lambda b, c: (b, jnp.minimum((pos_of(c) + 1) * hb8, n_hblk - 1), 0))]

    state_out = pl.BlockSpec((1, 1, d), lambda b, c: (b, 0, 0))
    fwd = lambda c: c
    hf, sf = pl.pallas_call(
        functools.partial(_scan_fwd_body, nc=nc),
        grid=(bsz, nc),
        in_specs=halo_specs(fwd) + gate_specs + [pl.BlockSpec((1, 1, d), lambda b, c: (b, 0, 0))],
        out_specs=[chunk, state_out], out_shape=[act, st],
        scratch_shapes=[pltpu.VMEM((1, d), F32)],
        compiler_params=cp, name="l0_scan_fwd",
    )(rec, rec, rec, p['conv_w'], p['conv_b'], p['wg'][0], p['bg'][0], p['lam'][0], h0[:, 0:1])

    bwd = lambda c: nc - 1 - c
    rchunk = pl.BlockSpec((1, tc, d), lambda b, c: (b, bwd(c), 0))
    in_specs = (halo_specs(bwd) + [rchunk, rchunk, rchunk]
                + ([pl.BlockSpec((tc, d), lambda b, c: (bwd(c), 0))] if has_pos else [])
                + [mod_spec] + gate_specs
                + [pl.BlockSpec((1, 1, d), lambda b, c: (b, 0, 0)), _const_spec((D_RNN, d)), row_spec])
    args = ([rec, rec, rec, gg, hf, x] + ([pos] if has_pos else [])
            + [mod, p['conv_w'], p['conv_b'], p['wg'][1], p['bg'][1], p['lam'][1], h0[:, 1:2],
               p['w_out'], p['mix_post_g']])
    x1, sb = pl.pallas_call(
        functools.partial(_scan_bwd_body, nc=nc, has_pos=has_pos),
        grid=(bsz, nc), in_specs=in_specs, out_specs=[rchunk, state_out], out_shape=[act, st],
        scratch_shapes=[pltpu.VMEM((1, d), F32)],
        compiler_params=cp, name="l0_scan_bwd",
    )(*args)
    return x1, jnp.concatenate([sf, sb], axis=1)


def _ffn_body(x_ref, mod_ref, pre_ref, post_ref, wi_ref, wo_ref, o_ref, act_ref):
    x = x_ref[...]
    sh = mod_ref[0, 3:4, :]
    sc = mod_ref[0, 4:5, :]
    g2 = mod_ref[0, 5:6, :]
    h = (_rms(x) * (pre_ref[...] * (1.0 + sc)) + sh).astype(BF16)
    for n in range(N_FF_CHUNKS):
        gu = _dot(h, wi_ref[n])
        gate = gu[:, :FF_CHUNK]
        act_ref[:, n * FF_CHUNK:(n + 1) * FF_CHUNK] = (gate * _sigmoid(gate) * gu[:, FF_CHUNK:]).astype(BF16)
    y = _dot(act_ref[...], wo_ref[...])
    o_ref[...] = x + _rms(y) * (post_ref[...] * g2)


def _ffn(x2d, mod, mod_row_of_block, p, tm):
    n_tok, d = x2d.shape
    row_spec = pl.BlockSpec((1, d), lambda i: (0, 0))
    tile = pl.BlockSpec((tm, d), lambda i: (i, 0))
    return pl.pallas_call(
        _ffn_body,
        grid=(n_tok // tm,),
        in_specs=[tile, pl.BlockSpec((1, N_MOD, d), lambda i: (mod_row_of_block(i), 0, 0)), row_spec, row_spec,
                  pl.BlockSpec((N_FF_CHUNKS, d, 2 * FF_CHUNK), lambda i: (0, 0, 0), pipeline_mode=pl.Buffered(1)),
                  pl.BlockSpec((D_FF, d), lambda i: (0, 0), pipeline_mode=pl.Buffered(1))],
        out_specs=tile,
        out_shape=jax.ShapeDtypeStruct((n_tok, d), F32),
        scratch_shapes=[pltpu.VMEM((tm, D_FF), BF16)],
        compiler_params=pltpu.CompilerParams(dimension_semantics=("parallel",), vmem_limit_bytes=VMEM_LIMIT),
        name="ffn",
    )(x2d, mod, p['ffn_pre_g'], p['ffn_post_g'], p['ffn_w_in'], p['ffn_w_out'])


def _pool_body(x_ref, prev_ref, next_ref, mod_ref, pre_ref, post_ref, pw_ref, pb_ref, ps_ref, o_ref, *, nc, t_len):
    c = pl.program_id(1)
    x = x_ref[0]
    tc = x.shape[0]
    sh = mod_ref[0, 0:1, :]
    sc = mod_ref[0, 1:2, :]
    g1 = mod_ref[0, 2:3, :]
    gs = pre_ref[...] * (1.0 + sc)
    h = _rms(x) * gs + sh
    hp = jnp.where(c > 0, _rms(prev_ref[0]) * gs + sh, 0.0)
    hn = jnp.where(c < nc - 1, _rms(next_ref[0]) * gs + sh, 0.0)
    ext = jnp.concatenate([hp, h, hn], axis=0)
    n = ext.shape[0]
    t = c * tc + lax.broadcasted_iota(jnp.int32, (tc, POOL_GROUP), 0)
    ys = []
    for gi, win in enumerate(POOL_WINDOWS):
        e = ext[:, gi * POOL_GROUP:(gi + 1) * POOL_GROUP]
        w = e + pltpu.roll(e, 1, 0)
        half = 1
        while 2 * half < win:
            w = pltpu.roll(w, half, 0) + pltpu.roll(w, n - half, 0)
            half *= 2
        cnt = (jnp.minimum(t + win // 2, t_len) - jnp.maximum(t - win // 2, 0)).astype(F32)
        pooled = w[HALO:n - HALO] / cnt - h[:, gi * POOL_GROUP:(gi + 1) * POOL_GROUP]
        ys.append(_dot(pooled.astype(BF16), pw_ref[gi]))
    y = (jnp.concatenate(ys, axis=1) + pb_ref[...]) * ps_ref[...]
    o_ref[0] = x + _rms(y) * (post_ref[...] * g1)


def _pool_layer(x, mod, mod_row, p, tc):
    bsz, t_len, d = x.shape
    nc = t_len // tc
    hb8 = tc // HALO
    n_hblk = t_len // HALO
    row_spec = pl.BlockSpec((1, d), lambda b, c: (0, 0))
    chunk = pl.BlockSpec((1, tc, d), lambda b, c: (b, c, 0))
    return pl.pallas_call(
        functools.partial(_pool_body, nc=nc, t_len=t_len),
        grid=(bsz, nc),
        in_specs=[chunk,
                  pl.BlockSpec((1, HALO, d), lambda b, c: (b, jnp.maximum(c * hb8 - 1, 0), 0)),
                  pl.BlockSpec((1, HALO, d), lambda b, c: (b, jnp.minimum((c + 1) * hb8, n_hblk - 1), 0)),
                  pl.BlockSpec((1, N_MOD, d), lambda b, c: (mod_row(b), 0, 0)),
                  row_spec, row_spec,
                  _const_spec((len(POOL_WINDOWS), POOL_GROUP, POOL_GROUP)), row_spec, row_spec],
        out_specs=chunk,
        out_shape=jax.ShapeDtypeStruct((bsz, t_len, d), F32),
        compiler_params=pltpu.CompilerParams(
            dimension_semantics=("parallel", "parallel"), vmem_limit_bytes=VMEM_LIMIT),
        name="l1_pool",
    )(x, x, x, mod, p['mix_pre_g'], p['mix_post_g'], p['pool_w'], p['pool_b'], p['pool_scale'])


def _grid_pos_emb(t_len):
    rows = t_len // GRID_W
    r = jnp.repeat(jnp.arange(rows), GRID_W)
    col = jnp.tile(jnp.arange(GRID_W), rows)
    quarter = D_MODEL // 4
    omega = 1.0 / (POS_THETA ** (jnp.arange(quarter, dtype=F32) / quarter))

    def emb(p):
        ang = p.astype(F32)[:, None] * omega[None, :]
        return jnp.concatenate([jnp.sin(ang), jnp.cos(ang)], axis=-1)

    return jnp.concatenate([emb(r), emb(col)], axis=-1)


def _ffn_weights(w_in, w_out):
    gate = w_in[:, :D_FF].reshape(D_MODEL, N_FF_CHUNKS, FF_CHUNK)
    up = w_in[:, D_FF:].reshape(D_MODEL, N_FF_CHUNKS, FF_CHUNK)
    wi = jnp.concatenate([gate, up], axis=-1).transpose(1, 0, 2).astype(BF16)
    return wi, w_out.astype(BF16)


def _row(v):
    return v.reshape(1, -1)


def kernel(x_prompt, x_sample, state_l0_rglru, c, c_ctx, l0_mod_w, l0_mod_b, l0_mix_pre_g, l0_mix_post_g, l0_w_in, l0_conv_w, l0_conv_b, l0_gate_a_w, l0_gate_a_b, l0_gate_x_w, l0_gate_x_b, l0_lambda, l0_w_out, l0_ffn_pre_g, l0_ffn_post_g, l0_ffn_w_in, l0_ffn_w_out, l1_mod_w, l1_mod_b, l1_mix_pre_g, l1_mix_post_g, l1_pool_w, l1_pool_b, l1_pool_scale, l1_ffn_pre_g, l1_ffn_post_g, l1_ffn_w_in, l1_ffn_w_out):
    n_ctx, t_ctx, d = x_prompt.shape
    n_lat, t_lat, _ = x_sample.shape

    cond = jnp.concatenate(
        [c, c_ctx[None, :], jnp.zeros((MOD_ROWS - n_lat - 1, d), F32)], axis=0)
    mod0 = _modulation(cond, l0_mod_w, l0_mod_b)
    mod1 = _modulation(cond, l1_mod_w, l1_mod_b)

    wi0, wo0 = _ffn_weights(l0_ffn_w_in, l0_ffn_w_out)
    wi1, wo1 = _ffn_weights(l1_ffn_w_in, l1_ffn_w_out)
    p0 = dict(
        mix_pre_g=_row(l0_mix_pre_g), mix_post_g=_row(l0_mix_post_g), w_in=l0_w_in.astype(BF16),
        conv_w=l0_conv_w, conv_b=_row(l0_conv_b),
        wg=[jnp.concatenate([l0_gate_a_w[k], l0_gate_x_w[k]], axis=-1).astype(BF16) for k in range(2)],
        bg=[jnp.stack([l0_gate_a_b[k], l0_gate_x_b[k]], axis=0) for k in range(2)],
        lam=[_row(l0_lambda[k]) for k in range(2)],
        w_out=l0_w_out.astype(BF16),
        ffn_pre_g=_row(l0_ffn_pre_g), ffn_post_g=_row(l0_ffn_post_g), ffn_w_in=wi0, ffn_w_out=wo0)
    p1 = dict(
        mix_pre_g=_row(l1_mix_pre_g), mix_post_g=_row(l1_mix_post_g), pool_w=l1_pool_w.astype(BF16),
        pool_b=_row(l1_pool_b), pool_scale=_row(l1_pool_scale),
        ffn_pre_g=_row(l1_ffn_pre_g), ffn_post_g=_row(l1_ffn_post_g), ffn_w_in=wi1, ffn_w_out=wo1)

    tc = 256
    tm = 256

    def run(x, pos, mod_row, h0):
        bsz, t_len, _ = x.shape
        blocks_per_seq = t_len // tm
        row_of_block = lambda i: mod_row(i // blocks_per_seq)
        x1, state = _rglru_layer(x, pos, mod0, mod_row, h0, p0, tc)
        x2 = _ffn(x1.reshape(bsz * t_len, d), mod0, row_of_block, p0, tm).reshape(bsz, t_len, d)
        x3 = _pool_layer(x2, mod1, mod_row, p1, tc)
        x4 = _ffn(x3.reshape(bsz * t_len, d), mod1, row_of_block, p1, tm).reshape(bsz, t_len, d)
        return x4, state

    y_prompt, new_state = run(x_prompt, None, lambda b: CTX_ROW, jnp.zeros((n_ctx, 2, D_RNN), F32))
    y_sample, _ = run(x_sample, _grid_pos_emb(t_lat), lambda b: b, state_l0_rglru)
    return y_prompt, y_sample, new_state
```

```python
import functools

import jax
import jax.numpy as jnp
from jax import lax
from jax.experimental import pallas as pl
from jax.experimental.pallas import tpu as pltpu

D_MODEL = 1024
D_RNN = D_MODEL
N_LRU_BLOCKS = 4
LRU_BLOCK = D_RNN // N_LRU_BLOCKS
CONV_W = 4
LRU_C = 8.0
POOL_WINDOWS = (2, 4, 8, 16)
POOL_GROUP = D_MODEL // len(POOL_WINDOWS)
D_FF = 2816
N_MOD = 6
EPS = 1e-6
POS_THETA = 10000.0
GRID_W = 64

LANES = 128
SUBLANES = 8
HALO = SUBLANES
N_SLABS = D_RNN // LANES
FF_CHUNK = 256
N_FF_CHUNKS = D_FF // FF_CHUNK
MOD_ROWS = 16
CTX_ROW = 8
MOD_TN = 1536
SQRT_FLOOR = 1e-36

VMEM_LIMIT = 56 * 1024 * 1024

F32 = jnp.float32
BF16 = jnp.bfloat16


def _dot(a, b):
    return jnp.dot(a, b, preferred_element_type=F32)


def _rms(x):
    return x * lax.rsqrt(jnp.mean(x * x, axis=-1, keepdims=True) + EPS)


def _sigmoid(x):
    return 1.0 / (1.0 + jnp.exp(-x))


def _mod_body(cond_ref, w_ref, b_ref, o_ref):
    c = cond_ref[...]
    s = (c * _sigmoid(c)).astype(BF16)
    o_ref[...] = _dot(s, w_ref[...].astype(BF16)) + b_ref[...]


def _modulation(cond, w, b):
    n = N_MOD * D_MODEL
    out = pl.pallas_call(
        _mod_body,
        grid=(n // MOD_TN,),
        in_specs=[
            pl.BlockSpec((MOD_ROWS, D_MODEL), lambda j: (0, 0)),
            pl.BlockSpec((D_MODEL, MOD_TN), lambda j: (0, j)),
            pl.BlockSpec((1, MOD_TN), lambda j: (0, j)),
        ],
        out_specs=pl.BlockSpec((MOD_ROWS, MOD_TN), lambda j: (0, j)),
        out_shape=jax.ShapeDtypeStruct((MOD_ROWS, n), F32),
        compiler_params=pltpu.CompilerParams(
            dimension_semantics=("parallel",), vmem_limit_bytes=VMEM_LIMIT),
        name="modulation",
    )(cond, w, b.reshape(1, n))
    return out.reshape(MOD_ROWS, N_MOD, D_MODEL)


def _gate_ab(u, wg_ref, bg_ref, lam_ref):
    ub = u.astype(BF16)
    ra, ix = [], []
    for n in range(N_LRU_BLOCKS):
        o = _dot(ub[:, n * LRU_BLOCK:(n + 1) * LRU_BLOCK], wg_ref[n])
        ra.append(o[:, :LRU_BLOCK])
        ix.append(o[:, LRU_BLOCK:])
    t_r = jnp.tanh(jnp.concatenate(ra, axis=1) + bg_ref[0:1, :])
    t_i = jnp.tanh(jnp.concatenate(ix, axis=1) + bg_ref[1:2, :])
    nl = -lam_ref[...]
    softplus = jnp.maximum(nl, 0.0) + jnp.log1p(jnp.exp(-jnp.abs(nl)))
    log_a = (t_r + 1.0) * ((-0.5 * LRU_C) * softplus)
    a = jnp.exp(log_a)
    s = jnp.tanh(log_a) * (-1.0 - a * a)
    root = s * lax.rsqrt(jnp.maximum(s, SQRT_FLOOR))
    hu = 0.5 * u
    b = root * (hu + hu * t_i)
    return a, b


def _scan(a, b, h_in, reverse):
    tc = a.shape[0]
    sub = tc // SUBLANES
    sub_id = lax.broadcasted_iota(jnp.int32, (SUBLANES, LANES), 0)
    steps = range(sub - 1, -1, -1) if reverse else range(sub)
    order = range(SUBLANES - 1, -1, -1) if reverse else range(SUBLANES)
    slabs, last = [], []
    for k in range(N_SLABS):
        lanes = slice(k * LANES, (k + 1) * LANES)
        hs, ps = [None] * sub, [None] * sub
        h = p = None
        for j in steps:
            av = a[j * SUBLANES:(j + 1) * SUBLANES, lanes]
            bv = b[j * SUBLANES:(j + 1) * SUBLANES, lanes]
            h = bv if h is None else av * h + bv
            p = av if p is None else av * p
            hs[j], ps[j] = h, p
        carry = h_in[:, lanes]
        carry_in = jnp.zeros((SUBLANES, LANES), F32)
        for s in order:
            carry_in = jnp.where(sub_id == s, carry, carry_in)
            carry = p[s:s + 1, :] * carry + h[s:s + 1, :]
        slabs.append(jnp.concatenate([ps[j] * carry_in + hs[j] for j in range(sub)], axis=0))
        last.append(carry)
    return jnp.concatenate(slabs, axis=1), jnp.concatenate(last, axis=1)


def _pitch(tc):
    return tc // SUBLANES + SUBLANES


def _permute_in(e, ext_s, tc, n_tiles):
    sub = tc // SUBLANES
    pitch = _pitch(tc)
    row = lax.broadcasted_iota(jnp.int32, (SUBLANES, LANES), 0)
    for k in range(N_SLABS):
        lanes = slice(k * LANES, (k + 1) * LANES)
        ext_s[k, 0:HALO, :] = e[0:HALO, lanes]
        for s in range(SUBLANES):
            base = HALO + s * pitch
            end = HALO + (s + 1) * sub
            ext_s[k, base:base + sub, :] = e[end - sub:end, lanes]
            ext_s[k, base + sub:base + pitch, :] = jnp.where(
                row < SUBLANES // 2, e[end:end + SUBLANES, lanes], e[end - SUBLANES:end, lanes])
    tiles = [jnp.concatenate([ext_s[k, pl.ds(HALO - 2 + q, SUBLANES, stride=pitch), :] for k in range(N_SLABS)],
                             axis=1) for q in range(n_tiles)]
    return jnp.concatenate(tiles, axis=0)


def _permute_out(r, o_s, tc):
    sub = tc // SUBLANES
    pitch = _pitch(tc)
    for k in range(N_SLABS):
        for j in range(sub):
            o_s[k, pl.ds(j, SUBLANES, stride=pitch), :] = r[j * SUBLANES:(j + 1) * SUBLANES, k * LANES:(k + 1) * LANES]
    return jnp.concatenate(
        [jnp.concatenate([o_s[k, s * pitch:s * pitch + sub, :] for s in range(SUBLANES)], axis=0)
         for k in range(N_SLABS)], axis=1)


def _fwd_body(*refs, nc, tc, has_pos):
    if has_pos:
        (x_ref, xp_ref, xn_ref, pos_ref, mod_ref, g_ref, w_ref, cw_ref, cb_ref, wg_ref, bg_ref, lam_ref, h0_ref,
         gg_ref, u_ref, hf_ref, sf_ref, carry_ref, ext_s) = refs
    else:
        (x_ref, xp_ref, xn_ref, mod_ref, g_ref, w_ref, cw_ref, cb_ref, wg_ref, bg_ref, lam_ref, h0_ref,
         gg_ref, u_ref, hf_ref, sf_ref, carry_ref, ext_s) = refs
    c = pl.program_id(1)
    sub = tc // SUBLANES
    n_tiles = sub + CONV_W

    @pl.when(c == 0)
    def _():
        carry_ref[...] = h0_ref[0]

    x, xp, xn = x_ref[0], xp_ref[0], xn_ref[0]
    if has_pos:
        t_len = nc * tc
        x = x + pos_ref[pl.ds(pl.multiple_of(c * tc, tc), tc), :]
        xp = xp + pos_ref[pl.ds(pl.multiple_of(jnp.maximum(c * tc - HALO, 0), HALO), HALO), :]
        xn = xn + pos_ref[pl.ds(pl.multiple_of(jnp.minimum((c + 1) * tc, t_len - HALO), HALO), HALO), :]
    gs = g_ref[...] * (1.0 + mod_ref[0, 1:2, :])
    sh = mod_ref[0, 0:1, :]
    e = jnp.concatenate([_rms(xp) * gs + sh, _rms(x) * gs + sh, _rms(xn) * gs + sh], axis=0)
    hp = _permute_in(e, ext_s, tc, n_tiles).astype(BF16)
    gg_ref[0] = jax.nn.gelu(_dot(hp[2 * SUBLANES:2 * SUBLANES + tc], w_ref[:, :D_RNN]))
    rec = _dot(hp, w_ref[:, D_RNN:])
    row = lax.broadcasted_iota(jnp.int32, (2 * SUBLANES, D_RNN), 0) & (SUBLANES - 1)
    head = jnp.where(row < jnp.where(c == 0, 1, 0), 0.0, rec[0:2 * SUBLANES])
    tail = jnp.where(row > jnp.where(c == nc - 1, SUBLANES - 2, SUBLANES - 1), 0.0, rec[(sub + 2) * SUBLANES:])
    rec = jnp.concatenate([head, rec[2 * SUBLANES:(sub + 2) * SUBLANES], tail], axis=0)
    u = rec[0:tc] * cw_ref[0:1, :]
    for k in range(1, CONV_W):
        u = u + rec[k * SUBLANES:k * SUBLANES + tc] * cw_ref[k:k + 1, :]
    u = u + cb_ref[...]
    u_ref[0] = u
    a, b = _gate_ab(u, wg_ref, bg_ref, lam_ref)
    hf, h_last = _scan(a, b, carry_ref[...], False)
    hf_ref[0] = hf
    carry_ref[...] = h_last
    sf_ref[0] = h_last


def _bwd_body(*refs, nc, tc, has_pos):
    if has_pos:
        (u_ref, gg_ref, hf_ref, x_ref, pos_ref, mod_ref, wg_ref, bg_ref, lam_ref, h0_ref, wo_ref, pg_ref,
         x1_ref, sb_ref, carry_ref, o_s) = refs
    else:
        (u_ref, gg_ref, hf_ref, x_ref, mod_ref, wg_ref, bg_ref, lam_ref, h0_ref, wo_ref, pg_ref,
         x1_ref, sb_ref, carry_ref, o_s) = refs
    c = pl.program_id(1)

    @pl.when(c == 0)
    def _():
        carry_ref[...] = h0_ref[0]

    a, b = _gate_ab(u_ref[0], wg_ref, bg_ref, lam_ref)
    hb, h_last = _scan(a, b, carry_ref[...], True)
    carry_ref[...] = h_last
    sb_ref[0] = h_last

    mixed = ((hf_ref[0] + hb) * gg_ref[0]).astype(BF16)
    y = _dot(mixed, wo_ref[...])
    x = x_ref[0]
    if has_pos:
        x = x + pos_ref[pl.ds(pl.multiple_of((nc - 1 - c) * tc, tc), tc), :]
    x1_ref[0] = x + _permute_out(_rms(y) * (pg_ref[...] * mod_ref[0, 2:3, :]), o_s, tc)


def _const_spec(shape, single=False):
    nd = len(shape)
    if single:
        return pl.BlockSpec(shape, lambda *_: (0,) * nd, pipeline_mode=pl.Buffered(1))
    return pl.BlockSpec(shape, lambda *_: (0,) * nd)


def _rglru_layer(x, pos, mod, mod_row, h0, p, tc):
    bsz, t_len, d = x.shape
    nc = t_len // tc
    hb8 = tc // HALO
    n_hblk = t_len // HALO
    has_pos = pos is not None
    act = jax.ShapeDtypeStruct((bsz, t_len, d), F32)
    st = jax.ShapeDtypeStruct((bsz, 1, d), F32)
    cp = pltpu.CompilerParams(dimension_semantics=("parallel", "arbitrary"), vmem_limit_bytes=VMEM_LIMIT)

    row_spec = pl.BlockSpec((1, d), lambda b, c: (0, 0))
    mod_spec = pl.BlockSpec((1, N_MOD, d), lambda b, c: (mod_row(b), 0, 0))
    pos_spec = [_const_spec((t_len, d), single=True)] if has_pos else []
    pos_arg = [pos] if has_pos else []
    state_spec = pl.BlockSpec((1, 1, d), lambda b, c: (b, 0, 0))
    gate_specs = [_const_spec((N_LRU_BLOCKS, LRU_BLOCK, 2 * LRU_BLOCK)), _const_spec((2, d)), row_spec, state_spec]

    chunk = pl.BlockSpec((1, tc, d), lambda b, c: (b, c, 0))
    in_specs = ([chunk,
                 pl.BlockSpec((1, HALO, d), lambda b, c: (b, jnp.maximum(c * hb8 - 1, 0), 0)),
                 pl.BlockSpec((1, HALO, d), lambda b, c: (b, jnp.minimum((c + 1) * hb8, n_hblk - 1), 0))]
                + pos_spec
                + [mod_spec, row_spec, _const_spec((d, 2 * D_RNN)), _const_spec((CONV_W, d)), row_spec]
                + gate_specs)
    args = ([x, x, x] + pos_arg
            + [mod, p['mix_pre_g'], p['w_in'], p['conv_w'], p['conv_b'], p['wg'][0], p['bg'][0], p['lam'][0],
               h0[:, 0:1]])
    gg, u, hf, sf = pl.pallas_call(
        functools.partial(_fwd_body, nc=nc, tc=tc, has_pos=has_pos),
        grid=(bsz, nc), in_specs=in_specs, out_specs=[chunk, chunk, chunk, state_spec],
        out_shape=[act, act, act, st],
        scratch_shapes=[pltpu.VMEM((1, d), F32),
                        pltpu.VMEM((N_SLABS, HALO + SUBLANES * _pitch(tc), LANES), F32)],
        compiler_params=cp, name="l0_fwd",
    )(*args)

    rchunk = pl.BlockSpec((1, tc, d), lambda b, c: (b, nc - 1 - c, 0))
    in_specs = ([rchunk, rchunk, rchunk, rchunk] + pos_spec + [mod_spec] + gate_specs
                + [_const_spec((D_RNN, d)), row_spec])
    args = ([u, gg, hf, x] + pos_arg
            + [mod, p['wg'][1], p['bg'][1], p['lam'][1], h0[:, 1:2], p['w_out'], p['mix_post_g']])
    x1, sb = pl.pallas_call(
        functools.partial(_bwd_body, nc=nc, tc=tc, has_pos=has_pos),
        grid=(bsz, nc), in_specs=in_specs, out_specs=[rchunk, state_spec], out_shape=[act, st],
        scratch_shapes=[pltpu.VMEM((1, d), F32), pltpu.VMEM((N_SLABS, SUBLANES * _pitch(tc), LANES), F32)],
        compiler_params=cp, name="l0_bwd",
    )(*args)
    return x1, jnp.concatenate([sf, sb], axis=1)


def _ffn_body(x_ref, mod_ref, pre_ref, post_ref, wi_ref, wo_ref, o_ref, act_ref):
    x = x_ref[...]
    sh = mod_ref[0, 3:4, :]
    sc = mod_ref[0, 4:5, :]
    g2 = mod_ref[0, 5:6, :]
    h = (_rms(x) * (pre_ref[...] * (1.0 + sc)) + sh).astype(BF16)
    for n in range(N_FF_CHUNKS):
        gu = _dot(h, wi_ref[n])
        gate = gu[:, :FF_CHUNK]
        act_ref[:, n * FF_CHUNK:(n + 1) * FF_CHUNK] = (gate * _sigmoid(gate) * gu[:, FF_CHUNK:]).astype(BF16)
    y = _dot(act_ref[...], wo_ref[...])
    o_ref[...] = x + _rms(y) * (post_ref[...] * g2)


def _ffn(x2d, mod, mod_row_of_block, p, tm):
    n_tok, d = x2d.shape
    row_spec = pl.BlockSpec((1, d), lambda i: (0, 0))
    tile = pl.BlockSpec((tm, d), lambda i: (i, 0))
    return pl.pallas_call(
        _ffn_body,
        grid=(n_tok // tm,),
        in_specs=[tile, pl.BlockSpec((1, N_MOD, d), lambda i: (mod_row_of_block(i), 0, 0)), row_spec, row_spec,
                  _const_spec((N_FF_CHUNKS, d, 2 * FF_CHUNK), single=True), _const_spec((D_FF, d), single=True)],
        out_specs=tile,
        out_shape=jax.ShapeDtypeStruct((n_tok, d), F32),
        scratch_shapes=[pltpu.VMEM((tm, D_FF), BF16)],
        compiler_params=pltpu.CompilerParams(dimension_semantics=("parallel",), vmem_limit_bytes=VMEM_LIMIT),
        name="ffn",
    )(x2d, mod, p['ffn_pre_g'], p['ffn_post_g'], p['ffn_w_in'], p['ffn_w_out'])


def _pool_body(x_ref, prev_ref, next_ref, mod_ref, pre_ref, post_ref, pw_ref, pb_ref, ps_ref, o_ref, *, nc, t_len):
    c = pl.program_id(1)
    x = x_ref[0]
    tc = x.shape[0]
    sh = mod_ref[0, 0:1, :]
    sc = mod_ref[0, 1:2, :]
    g1 = mod_ref[0, 2:3, :]
    gs = pre_ref[...] * (1.0 + sc)
    h = _rms(x) * gs + sh
    hp = jnp.where(c > 0, _rms(prev_ref[0]) * gs + sh, 0.0)
    hn = jnp.where(c < nc - 1, _rms(next_ref[0]) * gs + sh, 0.0)
    ext = jnp.concatenate([hp, h, hn], axis=0)
    n = ext.shape[0]
    t = c * tc + lax.broadcasted_iota(jnp.int32, (tc, POOL_GROUP), 0)
    ys = []
    for gi, win in enumerate(POOL_WINDOWS):
        e = ext[:, gi * POOL_GROUP:(gi + 1) * POOL_GROUP]
        w = e + pltpu.roll(e, 1, 0)
        half = 1
        while 2 * half < win:
            w = pltpu.roll(w, half, 0) + pltpu.roll(w, n - half, 0)
            half *= 2
        cnt = (jnp.minimum(t + win // 2, t_len) - jnp.maximum(t - win // 2, 0)).astype(F32)
        pooled = w[HALO:n - HALO] / cnt - h[:, gi * POOL_GROUP:(gi + 1) * POOL_GROUP]
        ys.append(_dot(pooled.astype(BF16), pw_ref[gi]))
    y = (jnp.concatenate(ys, axis=1) + pb_ref[...]) * ps_ref[...]
    o_ref[0] = x + _rms(y) * (post_ref[...] * g1)


def _pool_layer(x, mod, mod_row, p, tc):
    bsz, t_len, d = x.shape
    nc = t_len // tc
    hb8 = tc // HALO
    n_hblk = t_len // HALO
    row_spec = pl.BlockSpec((1, d), lambda b, c: (0, 0))
    chunk = pl.BlockSpec((1, tc, d), lambda b, c: (b, c, 0))
    return pl.pallas_call(
        functools.partial(_pool_body, nc=nc, t_len=t_len),
        grid=(bsz, nc),
        in_specs=[chunk,
                  pl.BlockSpec((1, HALO, d), lambda b, c: (b, jnp.maximum(c * hb8 - 1, 0), 0)),
                  pl.BlockSpec((1, HALO, d), lambda b, c: (b, jnp.minimum((c + 1) * hb8, n_hblk - 1), 0)),
                  pl.BlockSpec((1, N_MOD, d), lambda b, c: (mod_row(b), 0, 0)),
                  row_spec, row_spec,
                  _const_spec((len(POOL_WINDOWS), POOL_GROUP, POOL_GROUP)), row_spec, row_spec],
        out_specs=chunk,
        out_shape=jax.ShapeDtypeStruct((bsz, t_len, d), F32),
        compiler_params=pltpu.CompilerParams(
            dimension_semantics=("parallel", "parallel"), vmem_limit_bytes=VMEM_LIMIT),
        name="l1_pool",
    )(x, x, x, mod, p['mix_pre_g'], p['mix_post_g'], p['pool_w'], p['pool_b'], p['pool_scale'])


def _grid_pos_emb(t_len):
    rows = t_len // GRID_W
    r = jnp.repeat(jnp.arange(rows), GRID_W)
    col = jnp.tile(jnp.arange(GRID_W), rows)
    quarter = D_MODEL // 4
    omega = 1.0 / (POS_THETA ** (jnp.arange(quarter, dtype=F32) / quarter))

    def emb(p):
        ang = p.astype(F32)[:, None] * omega[None, :]
        return jnp.concatenate([jnp.sin(ang), jnp.cos(ang)], axis=-1)

    return jnp.concatenate([emb(r), emb(col)], axis=-1)


def _ffn_weights(w_in, w_out):
    gate = w_in[:, :D_FF].reshape(D_MODEL, N_FF_CHUNKS, FF_CHUNK)
    up = w_in[:, D_FF:].reshape(D_MODEL, N_FF_CHUNKS, FF_CHUNK)
    wi = jnp.concatenate([gate, up], axis=-1).transpose(1, 0, 2).astype(BF16)
    return wi, w_out.astype(BF16)


def _row(v):
    return v.reshape(1, -1)


def kernel(x_prompt, x_sample, state_l0_rglru, c, c_ctx, l0_mod_w, l0_mod_b, l0_mix_pre_g, l0_mix_post_g, l0_w_in, l0_conv_w, l0_conv_b, l0_gate_a_w, l0_gate_a_b, l0_gate_x_w, l0_gate_x_b, l0_lambda, l0_w_out, l0_ffn_pre_g, l0_ffn_post_g, l0_ffn_w_in, l0_ffn_w_out, l1_mod_w, l1_mod_b, l1_mix_pre_g, l1_mix_post_g, l1_pool_w, l1_pool_b, l1_pool_scale, l1_ffn_pre_g, l1_ffn_post_g, l1_ffn_w_in, l1_ffn_w_out):
    n_ctx, t_ctx, d = x_prompt.shape
    n_lat, t_lat, _ = x_sample.shape

    cond = jnp.concatenate(
        [c, c_ctx[None, :], jnp.zeros((MOD_ROWS - n_lat - 1, d), F32)], axis=0)
    mod0 = _modulation(cond, l0_mod_w, l0_mod_b)
    mod1 = _modulation(cond, l1_mod_w, l1_mod_b)

    wi0, wo0 = _ffn_weights(l0_ffn_w_in, l0_ffn_w_out)
    wi1, wo1 = _ffn_weights(l1_ffn_w_in, l1_ffn_w_out)
    p0 = dict(
        mix_pre_g=_row(l0_mix_pre_g), mix_post_g=_row(l0_mix_post_g), w_in=l0_w_in.astype(BF16),
        conv_w=l0_conv_w, conv_b=_row(l0_conv_b),
        wg=[(0.5 * jnp.concatenate([l0_gate_a_w[k], l0_gate_x_w[k]], axis=-1)).astype(BF16) for k in range(2)],
        bg=[0.5 * jnp.stack([l0_gate_a_b[k], l0_gate_x_b[k]], axis=0) for k in range(2)],
        lam=[_row(l0_lambda[k]) for k in range(2)],
        w_out=l0_w_out.astype(BF16),
        ffn_pre_g=_row(l0_ffn_pre_g), ffn_post_g=_row(l0_ffn_post_g), ffn_w_in=wi0, ffn_w_out=wo0)
    p1 = dict(
        mix_pre_g=_row(l1_mix_pre_g), mix_post_g=_row(l1_mix_post_g), pool_w=l1_pool_w.astype(BF16),
        pool_b=_row(l1_pool_b), pool_scale=_row(l1_pool_scale),
        ffn_pre_g=_row(l1_ffn_pre_g), ffn_post_g=_row(l1_ffn_post_g), ffn_w_in=wi1, ffn_w_out=wo1)

    tc = 256
    tm = 256

    def run(x, pos, mod_row, h0):
        bsz, t_len, _ = x.shape
        blocks_per_seq = t_len // tm
        row_of_block = lambda i: mod_row(i // blocks_per_seq)
        x1, state = _rglru_layer(x, pos, mod0, mod_row, h0, p0, tc)
        x2 = _ffn(x1.reshape(bsz * t_len, d), mod0, row_of_block, p0, tm).reshape(bsz, t_len, d)
        x3 = _pool_layer(x2, mod1, mod_row, p1, tc)
        x4 = _ffn(x3.reshape(bsz * t_len, d), mod1, row_of_block, p1, tm).reshape(bsz, t_len, d)
        return x4, state

    y_prompt, new_state = run(x_prompt, None, lambda b: CTX_ROW, jnp.zeros((n_ctx, 2, D_RNN), F32))
    y_sample, _ = run(x_sample, _grid_pos_emb(t_lat), lambda b: b, state_l0_rglru)
    return y_prompt, y_sample, new_state
```

```python
import functools

import jax
import jax.numpy as jnp
from jax import lax
from jax.experimental import pallas as pl
from jax.experimental.pallas import tpu as pltpu

D_MODEL = 1024
D_RNN = D_MODEL
N_LRU_BLOCKS = 4
LRU_BLOCK = D_RNN // N_LRU_BLOCKS
CONV_W = 4
LRU_C = 8.0
POOL_WINDOWS = (2, 4, 8, 16)
POOL_GROUP = D_MODEL // len(POOL_WINDOWS)
D_FF = 2816
N_MOD = 6
EPS = 1e-6
POS_THETA = 10000.0
GRID_W = 64

LANES = 128
SUBLANES = 8
HALO = SUBLANES
N_SLABS = D_RNN // LANES
FF_CHUNK = 256
N_FF_CHUNKS = D_FF // FF_CHUNK
MOD_ROWS = 16
CTX_ROW = 8
MOD_TN = 1536
SQRT_FLOOR = 1e-36
CHUNK = 256

VMEM_LIMIT = 56 * 1024 * 1024

F32 = jnp.float32
BF16 = jnp.bfloat16


def _dot(a, b):
    return jnp.dot(a, b, preferred_element_type=F32)


def _rms(x):
    return x * lax.rsqrt(jnp.mean(x * x, axis=-1, keepdims=True) + EPS)


def _sigmoid(x):
    return 1.0 / (1.0 + jnp.exp(-x))


def _mod_body(cond_ref, w_ref, b_ref, o_ref):
    c = cond_ref[...]
    s = (c * _sigmoid(c)).astype(BF16)
    o_ref[...] = _dot(s, w_ref[...].astype(BF16)) + b_ref[...]


def _modulation(cond, w, b):
    n = N_MOD * D_MODEL
    out = pl.pallas_call(
        _mod_body,
        grid=(n // MOD_TN,),
        in_specs=[
            pl.BlockSpec((MOD_ROWS, D_MODEL), lambda j: (0, 0)),
            pl.BlockSpec((D_MODEL, MOD_TN), lambda j: (0, j)),
            pl.BlockSpec((1, MOD_TN), lambda j: (0, j)),
        ],
        out_specs=pl.BlockSpec((MOD_ROWS, MOD_TN), lambda j: (0, j)),
        out_shape=jax.ShapeDtypeStruct((MOD_ROWS, n), F32),
        compiler_params=pltpu.CompilerParams(
            dimension_semantics=("parallel",), vmem_limit_bytes=VMEM_LIMIT),
        name="modulation",
    )(cond, w, b.reshape(1, n))
    return out.reshape(MOD_ROWS, N_MOD, D_MODEL)


def _gate_ab(u, n, wg_ref, bg_ref, lam_ref):
    cols = slice(n * LRU_BLOCK, (n + 1) * LRU_BLOCK)
    o = _dot(u.astype(BF16), wg_ref[n])
    t_r = jnp.tanh(o[:, :LRU_BLOCK] + bg_ref[0:1, cols])
    t_i = jnp.tanh(o[:, LRU_BLOCK:] + bg_ref[1:2, cols])
    nl = -lam_ref[:, cols]
    softplus = jnp.maximum(nl, 0.0) + jnp.log1p(jnp.exp(-jnp.abs(nl)))
    log_a = (t_r + 1.0) * ((-0.5 * LRU_C) * softplus)
    a = jnp.exp(log_a)
    s = jnp.tanh(log_a) * (-1.0 - a * a)
    root = s * lax.rsqrt(jnp.maximum(s, SQRT_FLOOR))
    hu = 0.5 * u
    b = root * (hu + hu * t_i)
    return a, b


def _scan(a, b, h_in, reverse):
    tc = a.shape[0]
    sub = tc // SUBLANES
    sub_id = lax.broadcasted_iota(jnp.int32, (SUBLANES, LANES), 0)
    steps = range(sub - 1, -1, -1) if reverse else range(sub)
    order = range(SUBLANES - 1, -1, -1) if reverse else range(SUBLANES)
    slabs, last = [], []
    for k in range(a.shape[1] // LANES):
        lanes = slice(k * LANES, (k + 1) * LANES)
        hs, ps = [None] * sub, [None] * sub
        h = p = None
        for j in steps:
            av = a[j * SUBLANES:(j + 1) * SUBLANES, lanes]
            bv = b[j * SUBLANES:(j + 1) * SUBLANES, lanes]
            h = bv if h is None else av * h + bv
            p = av if p is None else av * p
            hs[j], ps[j] = h, p
        carry = h_in[:, lanes]
        carry_in = jnp.zeros((SUBLANES, LANES), F32)
        for s in order:
            carry_in = jnp.where(sub_id == s, carry, carry_in)
            carry = p[s:s + 1, :] * carry + h[s:s + 1, :]
        slabs.append(jnp.concatenate([ps[j] * carry_in + hs[j] for j in range(sub)], axis=0))
        last.append(carry)
    return jnp.concatenate(slabs, axis=1), jnp.concatenate(last, axis=1)


def _pitch(tc):
    return tc // SUBLANES + SUBLANES


def _permute_in(e, ext_s, tc, n_tiles):
    sub = tc // SUBLANES
    pitch = _pitch(tc)
    row = lax.broadcasted_iota(jnp.int32, (SUBLANES, LANES), 0)
    for k in range(N_SLABS):
        lanes = slice(k * LANES, (k + 1) * LANES)
        ext_s[k, 0:HALO, :] = e[0:HALO, lanes]
        for s in range(SUBLANES):
            base = HALO + s * pitch
            end = HALO + (s + 1) * sub
            ext_s[k, base:base + sub, :] = e[end - sub:end, lanes]
            ext_s[k, base + sub:base + pitch, :] = jnp.where(
                row < SUBLANES // 2, e[end:end + SUBLANES, lanes], e[end - SUBLANES:end, lanes])
    tiles = [jnp.concatenate([ext_s[k, pl.ds(HALO - 2 + q, SUBLANES, stride=pitch), :] for k in range(N_SLABS)],
                             axis=1) for q in range(n_tiles)]
    return jnp.concatenate(tiles, axis=0)


def _permute_out(r, o_s, tc):
    sub = tc // SUBLANES
    pitch = _pitch(tc)
    for k in range(N_SLABS):
        for j in range(sub):
            o_s[k, pl.ds(j, SUBLANES, stride=pitch), :] = r[j * SUBLANES:(j + 1) * SUBLANES, k * LANES:(k + 1) * LANES]
    return jnp.concatenate(
        [jnp.concatenate([o_s[k, s * pitch:s * pitch + sub, :] for s in range(SUBLANES)], axis=0)
         for k in range(N_SLABS)], axis=1)


def _fwd_body(*refs, nc, tc, has_pos):
    if has_pos:
        (x_ref, xp_ref, xn_ref, pos_ref, posp_ref, posn_ref, mod_ref, g_ref, w_ref, cw_ref, cb_ref, wg_ref, bg_ref,
         lam_ref, h0_ref, gg_ref, u_ref, hf_ref, sf_ref, carry_ref, ext_s) = refs
    else:
        (x_ref, xp_ref, xn_ref, mod_ref, g_ref, w_ref, cw_ref, cb_ref, wg_ref, bg_ref,
         lam_ref, h0_ref, gg_ref, u_ref, hf_ref, sf_ref, carry_ref, ext_s) = refs
    c = pl.program_id(1)
    sub = tc // SUBLANES
    n_tiles = sub + CONV_W

    @pl.when(c == 0)
    def _():
        carry_ref[...] = h0_ref[0]

    x, xp, xn = x_ref[0], xp_ref[0], xn_ref[0]
    if has_pos:
        x, xp, xn = x + pos_ref[...], xp + posp_ref[...], xn + posn_ref[...]
    gs = g_ref[...] * (1.0 + mod_ref[0, 1:2, :])
    sh = mod_ref[0, 0:1, :]
    e = jnp.concatenate([_rms(xp) * gs + sh, _rms(x) * gs + sh, _rms(xn) * gs + sh], axis=0)
    hp = _permute_in(e, ext_s, tc, n_tiles).astype(BF16)
    gg_ref[0] = jax.nn.gelu(_dot(hp[2 * SUBLANES:2 * SUBLANES + tc], w_ref[:, :D_RNN]))
    rec = _dot(hp, w_ref[:, D_RNN:])
    row = lax.broadcasted_iota(jnp.int32, (2 * SUBLANES, D_RNN), 0) & (SUBLANES - 1)
    head = jnp.where(row < jnp.where(c == 0, 1, 0), 0.0, rec[0:2 * SUBLANES])
    tail = jnp.where(row > jnp.where(c == nc - 1, SUBLANES - 2, SUBLANES - 1), 0.0, rec[(sub + 2) * SUBLANES:])
    rec = jnp.concatenate([head, rec[2 * SUBLANES:(sub + 2) * SUBLANES], tail], axis=0)
    u = rec[0:tc] * cw_ref[0:1, :]
    for k in range(1, CONV_W):
        u = u + rec[k * SUBLANES:k * SUBLANES + tc] * cw_ref[k:k + 1, :]
    u = u + cb_ref[...]
    u_ref[0] = u
    lasts = []
    for n in range(N_LRU_BLOCKS):
        cols = slice(n * LRU_BLOCK, (n + 1) * LRU_BLOCK)
        a, b = _gate_ab(u[:, cols], n, wg_ref, bg_ref, lam_ref)
        hf, h_last = _scan(a, b, carry_ref[:, cols], False)
        hf_ref[0, :, cols] = hf
        lasts.append(h_last)
    h_last = jnp.concatenate(lasts, axis=1)
    carry_ref[...] = h_last
    sf_ref[0] = h_last


def _ffn_head(x, mod_ref, pre_ref):
    sh = mod_ref[0, 3:4, :]
    sc = mod_ref[0, 4:5, :]
    return (_rms(x) * (pre_ref[...] * (1.0 + sc)) + sh).astype(BF16)


def _ffn_steps(h, wi_ref, wo_ref, act_ref, out):
    def chunk(n):
        def f():
            g = _dot(h, wi_ref[:, n * FF_CHUNK:(n + 1) * FF_CHUNK])
            v = _dot(h, wi_ref[:, D_FF + n * FF_CHUNK:D_FF + (n + 1) * FF_CHUNK])
            act_ref[:, n * FF_CHUNK:(n + 1) * FF_CHUNK] = (g * _sigmoid(g) * v).astype(BF16)
        return f

    def down():
        out.append(_dot(act_ref[...], wo_ref[...]))

    return [chunk(n) for n in range(N_FF_CHUNKS)] + [down]


def _ffn_tail(x, y, mod_ref, post_ref):
    return x + _rms(y) * (post_ref[...] * mod_ref[0, 5:6, :])


def _trace_interleaved(major, minor):
    done = 0
    for k, step in enumerate(major):
        step()
        while done < (k + 1) * len(minor) // len(major):
            minor[done]()
            done += 1


def _bwd_ffn_body(*refs, nc, tc, n_chunks, has_pos):
    if has_pos:
        (u_ref, gg_ref, hf_ref, x_ref, pos_ref, mod_ref, wg_ref, bg_ref, lam_ref, h0_ref, wo_ref, pg_ref,
         fmod_ref, pre_ref, post_ref, wi_ref, wo2_ref, x2_ref, sb_ref,
         carry_ref, o_s, x1_s, act_ref, mixed_ref) = refs
    else:
        (u_ref, gg_ref, hf_ref, x_ref, mod_ref, wg_ref, bg_ref, lam_ref, h0_ref, wo_ref, pg_ref,
         fmod_ref, pre_ref, post_ref, wi_ref, wo2_ref, x2_ref, sb_ref,
         carry_ref, o_s, x1_s, act_ref, mixed_ref) = refs
    i = pl.program_id(0)
    c = jnp.minimum(i, n_chunks - 1) % nc

    @pl.when(i == 0)
    def _():
        x1_s[...] = jnp.zeros(x1_s.shape, F32)

    @pl.when(c == 0)
    def _():
        carry_ref[...] = h0_ref[0]

    st = {}
    lasts = [None] * N_LRU_BLOCKS

    def block(n):
        def f():
            cols = slice(n * LRU_BLOCK, (n + 1) * LRU_BLOCK)
            a, b = _gate_ab(u_ref[0, :, cols], n, wg_ref, bg_ref, lam_ref)
            hb, lasts[n] = _scan(a, b, carry_ref[:, cols], True)
            mixed_ref[:, cols] = ((hf_ref[0, :, cols] + hb) * gg_ref[0, :, cols]).astype(BF16)
        return f

    def out_proj():
        st['y'] = _dot(mixed_ref[...], wo_ref[...])

    def residual():
        x = x_ref[0]
        if has_pos:
            x = x + pos_ref[...]
        x1_s[i % 2] = x + _permute_out(_rms(st['y']) * (pg_ref[...] * mod_ref[0, 2:3, :]), o_s, tc)

    mixer = [block(n) for n in range(N_LRU_BLOCKS)] + [out_proj, residual]

    xf = x1_s[(i + 1) % 2]
    yf = []
    _trace_interleaved(_ffn_steps(_ffn_head(xf, fmod_ref, pre_ref), wi_ref, wo2_ref, act_ref, yf), mixer)
    x2_ref[0] = _ffn_tail(xf, jnp.concatenate(yf, axis=1), fmod_ref, post_ref)
    h_last = jnp.concatenate(lasts, axis=1)

    @pl.when(i < n_chunks)
    def _():
        carry_ref[...] = h_last
        sb_ref[0] = h_last


def _pool_group(ext, gi, c, pw_ref, t_len):
    n = ext.shape[0]
    tc = n - 2 * HALO
    win = POOL_WINDOWS[gi]
    e = ext[:, gi * POOL_GROUP:(gi + 1) * POOL_GROUP]
    w = e + pltpu.roll(e, 1, 0)
    half = 1
    while 2 * half < win:
        w = pltpu.roll(w, half, 0) + pltpu.roll(w, n - half, 0)
        half *= 2
    t = c * tc + lax.broadcasted_iota(jnp.int32, (tc, POOL_GROUP), 0)
    cnt = (jnp.minimum(t + win // 2, t_len) - jnp.maximum(t - win // 2, 0)).astype(F32)
    pooled = w[HALO:n - HALO] / cnt - e[HALO:n - HALO]
    return _dot(pooled.astype(BF16), pw_ref[gi])


def _pool_ffn_body(x_ref, xp_ref, xn_ref, mod_ref, mpre_ref, mpost_ref, pw_ref, pb_ref, ps_ref,
                   fmod_ref, pre_ref, post_ref, wi_ref, wo_ref, o_ref, x3_s, act_ref, *, nc, t_len, n_chunks):
    i = pl.program_id(0)
    c = jnp.minimum(i, n_chunks - 1) % nc

    @pl.when(i == 0)
    def _():
        x3_s[...] = jnp.zeros(x3_s.shape, F32)

    st = {}
    ys = [None] * len(POOL_WINDOWS)

    def pre_norm():
        gs = mpre_ref[...] * (1.0 + mod_ref[0, 1:2, :])
        sh = mod_ref[0, 0:1, :]
        st['ext'] = jnp.concatenate([jnp.where(c > 0, _rms(xp_ref[0]) * gs + sh, 0.0),
                                     _rms(x_ref[0]) * gs + sh,
                                     jnp.where(c < nc - 1, _rms(xn_ref[0]) * gs + sh, 0.0)], axis=0)

    def group(gi):
        def f():
            ys[gi] = _pool_group(st['ext'], gi, c, pw_ref, t_len)
        return f

    def residual():
        y = (jnp.concatenate(ys, axis=1) + pb_ref[...]) * ps_ref[...]
        x3_s[i % 2] = x_ref[0] + _rms(y) * (mpost_ref[...] * mod_ref[0, 2:3, :])

    mixer = [pre_norm] + [group(gi) for gi in range(len(POOL_WINDOWS))] + [residual]

    xf = x3_s[(i + 1) % 2]
    yf = []
    _trace_interleaved(_ffn_steps(_ffn_head(xf, fmod_ref, pre_ref), wi_ref, wo_ref, act_ref, yf), mixer)
    o_ref[0] = _ffn_tail(xf, jnp.concatenate(yf, axis=1), fmod_ref, post_ref)


def _const_spec(shape, single=False):
    nd = len(shape)
    if single:
        return pl.BlockSpec(shape, lambda *_: (0,) * nd, pipeline_mode=pl.Buffered(1))
    return pl.BlockSpec(shape, lambda *_: (0,) * nd)


def _ffn_specs(d):
    return [_const_spec((1, d)), _const_spec((1, d)),
            _const_spec((d, 2 * D_FF), single=True), _const_spec((D_FF, d), single=True)]


def _layer0(x, pos, mod, mod_row, h0, p, tc):
    bsz, t_len, d = x.shape
    nc = t_len // tc
    n_chunks = bsz * nc
    hb8 = tc // HALO
    n_hblk = t_len // HALO
    has_pos = pos is not None
    act = jax.ShapeDtypeStruct((bsz, t_len, d), F32)
    st = jax.ShapeDtypeStruct((bsz, 1, d), F32)

    prev_blk = lambda c: jnp.maximum(c * hb8 - 1, 0)
    next_blk = lambda c: jnp.minimum((c + 1) * hb8, n_hblk - 1)
    row_spec = pl.BlockSpec((1, d), lambda b, c: (0, 0))
    chunk = pl.BlockSpec((1, tc, d), lambda b, c: (b, c, 0))
    state_spec = pl.BlockSpec((1, 1, d), lambda b, c: (b, 0, 0))
    pos_specs = [pl.BlockSpec((tc, d), lambda b, c: (c, 0)),
                 pl.BlockSpec((HALO, d), lambda b, c: (prev_blk(c), 0)),
                 pl.BlockSpec((HALO, d), lambda b, c: (next_blk(c), 0))] if has_pos else []
    in_specs = ([chunk,
                 pl.BlockSpec((1, HALO, d), lambda b, c: (b, prev_blk(c), 0)),
                 pl.BlockSpec((1, HALO, d), lambda b, c: (b, next_blk(c), 0))]
                + pos_specs
                + [pl.BlockSpec((1, N_MOD, d), lambda b, c: (mod_row(b), 0, 0)), row_spec,
                   _const_spec((d, 2 * D_RNN)), _const_spec((CONV_W, d)), row_spec,
                   _const_spec((N_LRU_BLOCKS, LRU_BLOCK, 2 * LRU_BLOCK)), _const_spec((2, d)), row_spec, state_spec])
    args = ([x, x, x] + ([pos, pos, pos] if has_pos else [])
            + [mod, p['mix_pre_g'], p['w_in'], p['conv_w'], p['conv_b'], p['wg'][0], p['bg'][0], p['lam'][0],
               h0[:, 0:1]])
    gg, u, hf, sf = pl.pallas_call(
        functools.partial(_fwd_body, nc=nc, tc=tc, has_pos=has_pos),
        grid=(bsz, nc), in_specs=in_specs, out_specs=[chunk, chunk, chunk, state_spec],
        out_shape=[act, act, act, st],
        scratch_shapes=[pltpu.VMEM((1, d), F32),
                        pltpu.VMEM((N_SLABS, HALO + SUBLANES * _pitch(tc), LANES), F32)],
        compiler_params=pltpu.CompilerParams(
            dimension_semantics=("parallel", "arbitrary"), vmem_limit_bytes=VMEM_LIMIT),
        name="l0_fwd",
    )(*args)

    def mix_at(i):
        im = jnp.minimum(i, n_chunks - 1)
        return im // nc, nc - 1 - im % nc

    def ffn_at(i):
        return mix_at(jnp.maximum(i - 1, 0))

    mchunk = pl.BlockSpec((1, tc, d), lambda i: (*mix_at(i), 0))
    in_specs = ([mchunk, mchunk, mchunk, mchunk]
                + ([pl.BlockSpec((tc, d), lambda i: (mix_at(i)[1], 0))] if has_pos else [])
                + [pl.BlockSpec((1, N_MOD, d), lambda i: (mod_row(mix_at(i)[0]), 0, 0)),
                   _const_spec((N_LRU_BLOCKS, LRU_BLOCK, 2 * LRU_BLOCK)), _const_spec((2, d)), _const_spec((1, d)),
                   pl.BlockSpec((1, 1, d), lambda i: (mix_at(i)[0], 0, 0)),
                   _const_spec((D_RNN, d)), _const_spec((1, d)),
                   pl.BlockSpec((1, N_MOD, d), lambda i: (mod_row(ffn_at(i)[0]), 0, 0))]
                + _ffn_specs(d))
    args = ([u, gg, hf, x] + ([pos] if has_pos else [])
            + [mod, p['wg'][1], p['bg'][1], p['lam'][1], h0[:, 1:2], p['w_out'], p['mix_post_g'],
               mod, p['ffn_pre_g'], p['ffn_post_g'], p['ffn_w_in'], p['ffn_w_out']])
    x2, sb = pl.pallas_call(
        functools.partial(_bwd_ffn_body, nc=nc, tc=tc, n_chunks=n_chunks, has_pos=has_pos),
        grid=(n_chunks + 1,), in_specs=in_specs,
        out_specs=[pl.BlockSpec((1, tc, d), lambda i: (*ffn_at(i), 0)),
                   pl.BlockSpec((1, 1, d), lambda i: (mix_at(i)[0], 0, 0))],
        out_shape=[act, st],
        scratch_shapes=[pltpu.VMEM((1, d), F32), pltpu.VMEM((N_SLABS, SUBLANES * _pitch(tc), LANES), F32),
                        pltpu.VMEM((2, tc, d), F32), pltpu.VMEM((tc, D_FF), BF16), pltpu.VMEM((tc, D_RNN), BF16)],
        compiler_params=pltpu.CompilerParams(dimension_semantics=("arbitrary",), vmem_limit_bytes=VMEM_LIMIT),
        name="l0_bwd_ffn",
    )(*args)
    return x2, jnp.concatenate([sf, sb], axis=1)


def _layer1(x, mod, mod_row, p, tc):
    bsz, t_len, d = x.shape
    nc = t_len // tc
    n_chunks = bsz * nc
    hb8 = tc // HALO
    n_hblk = t_len // HALO

    def mix_at(i):
        im = jnp.minimum(i, n_chunks - 1)
        return im // nc, im % nc

    def ffn_at(i):
        return mix_at(jnp.maximum(i - 1, 0))

    def halo_spec(blk_of):
        return pl.BlockSpec((1, HALO, d), lambda i: (mix_at(i)[0], blk_of(mix_at(i)[1]), 0))

    return pl.pallas_call(
        functools.partial(_pool_ffn_body, nc=nc, t_len=t_len, n_chunks=n_chunks),
        grid=(n_chunks + 1,),
        in_specs=[pl.BlockSpec((1, tc, d), lambda i: (*mix_at(i), 0)),
                  halo_spec(lambda c: jnp.maximum(c * hb8 - 1, 0)),
                  halo_spec(lambda c: jnp.minimum((c + 1) * hb8, n_hblk - 1)),
                  pl.BlockSpec((1, N_MOD, d), lambda i: (mod_row(mix_at(i)[0]), 0, 0)),
                  _const_spec((1, d)), _const_spec((1, d)),
                  _const_spec((len(POOL_WINDOWS), POOL_GROUP, POOL_GROUP)), _const_spec((1, d)), _const_spec((1, d)),
                  pl.BlockSpec((1, N_MOD, d), lambda i: (mod_row(ffn_at(i)[0]), 0, 0))] + _ffn_specs(d),
        out_specs=pl.BlockSpec((1, tc, d), lambda i: (*ffn_at(i), 0)),
        out_shape=jax.ShapeDtypeStruct((bsz, t_len, d), F32),
        scratch_shapes=[pltpu.VMEM((2, tc, d), F32), pltpu.VMEM((tc, D_FF), BF16)],
        compiler_params=pltpu.CompilerParams(dimension_semantics=("arbitrary",), vmem_limit_bytes=VMEM_LIMIT),
        name="l1_ffn",
    )(x, x, x, mod, p['mix_pre_g'], p['mix_post_g'], p['pool_w'], p['pool_b'], p['pool_scale'],
      mod, p['ffn_pre_g'], p['ffn_post_g'], p['ffn_w_in'], p['ffn_w_out'])


def _grid_pos_emb(t_len):
    rows = t_len // GRID_W
    quarter = D_MODEL // 4
    omega = 1.0 / (POS_THETA ** (jnp.arange(quarter, dtype=F32) / quarter))
    ang = jnp.arange(max(rows, GRID_W), dtype=F32)[:, None] * omega[None, :]
    table = jnp.concatenate([jnp.sin(ang), jnp.cos(ang)], axis=-1)
    half = D_MODEL // 2
    by_row = jnp.broadcast_to(table[:rows, None, :], (rows, GRID_W, half))
    by_col = jnp.broadcast_to(table[None, :GRID_W, :], (rows, GRID_W, half))
    return jnp.concatenate([by_row, by_col], axis=-1).reshape(t_len, D_MODEL)


def _row(v):
    return v.reshape(1, -1)


def kernel(x_prompt, x_sample, state_l0_rglru, c, c_ctx, l0_mod_w, l0_mod_b, l0_mix_pre_g, l0_mix_post_g, l0_w_in, l0_conv_w, l0_conv_b, l0_gate_a_w, l0_gate_a_b, l0_gate_x_w, l0_gate_x_b, l0_lambda, l0_w_out, l0_ffn_pre_g, l0_ffn_post_g, l0_ffn_w_in, l0_ffn_w_out, l1_mod_w, l1_mod_b, l1_mix_pre_g, l1_mix_post_g, l1_pool_w, l1_pool_b, l1_pool_scale, l1_ffn_pre_g, l1_ffn_post_g, l1_ffn_w_in, l1_ffn_w_out):
    n_ctx, t_ctx, d = x_prompt.shape
    n_lat, t_lat, _ = x_sample.shape

    cond = jnp.concatenate(
        [c, c_ctx[None, :], jnp.zeros((MOD_ROWS - n_lat - 1, d), F32)], axis=0)
    mod0 = _modulation(cond, l0_mod_w, l0_mod_b)
    mod1 = _modulation(cond, l1_mod_w, l1_mod_b)

    p0 = dict(
        mix_pre_g=_row(l0_mix_pre_g), mix_post_g=_row(l0_mix_post_g), w_in=l0_w_in.astype(BF16),
        conv_w=l0_conv_w, conv_b=_row(l0_conv_b),
        wg=[(0.5 * jnp.concatenate([l0_gate_a_w[k], l0_gate_x_w[k]], axis=-1)).astype(BF16) for k in range(2)],
        bg=[0.5 * jnp.stack([l0_gate_a_b[k], l0_gate_x_b[k]], axis=0) for k in range(2)],
        lam=[_row(l0_lambda[k]) for k in range(2)],
        w_out=l0_w_out.astype(BF16),
        ffn_pre_g=_row(l0_ffn_pre_g), ffn_post_g=_row(l0_ffn_post_g),
        ffn_w_in=l0_ffn_w_in.astype(BF16), ffn_w_out=l0_ffn_w_out.astype(BF16))
    p1 = dict(
        mix_pre_g=_row(l1_mix_pre_g), mix_post_g=_row(l1_mix_post_g), pool_w=l1_pool_w.astype(BF16),
        pool_b=_row(l1_pool_b), pool_scale=_row(l1_pool_scale),
        ffn_pre_g=_row(l1_ffn_pre_g), ffn_post_g=_row(l1_ffn_post_g),
        ffn_w_in=l1_ffn_w_in.astype(BF16), ffn_w_out=l1_ffn_w_out.astype(BF16))

    def run(x, pos, mod_row, h0):
        x2, state = _layer0(x, pos, mod0, mod_row, h0, p0, CHUNK)
        return _layer1(x2, mod1, mod_row, p1, CHUNK), state

    y_prompt, new_state = run(x_prompt, None, lambda b: CTX_ROW, jnp.zeros((n_ctx, 2, D_RNN), F32))
    y_sample, _ = run(x_sample, _grid_pos_emb(t_lat), lambda b: b, state_l0_rglru)
    return y_prompt, y_sample, new_state
```

```python
import functools

import jax
import jax.numpy as jnp
from jax import lax
from jax.experimental import pallas as pl
from jax.experimental.pallas import tpu as pltpu

D_MODEL = 1024
D_RNN = D_MODEL
N_LRU_BLOCKS = 4
LRU_BLOCK = D_RNN // N_LRU_BLOCKS
CONV_W = 4
LRU_C = 8.0
POOL_WINDOWS = (2, 4, 8, 16)
POOL_GROUP = D_MODEL // len(POOL_WINDOWS)
D_FF = 2816
N_MOD = 6
EPS = 1e-6
POS_THETA = 10000.0
GRID_W = 64

LANES = 128
SUBLANES = 8
HALO = SUBLANES
N_SLABS = D_RNN // LANES
FF_CHUNK = 256
N_FF_CHUNKS = D_FF // FF_CHUNK
MOD_ROWS = 16
CTX_ROW = 8
MOD_TN = 1536
SQRT_FLOOR = 1e-36
CHUNK_L0 = 256
CHUNK_L1 = 512

VMEM_LIMIT = 56 * 1024 * 1024

F32 = jnp.float32
BF16 = jnp.bfloat16


def _dot(a, b):
    return jnp.dot(a, b, preferred_element_type=F32)


def _rms(x):
    return x * lax.rsqrt(jnp.mean(x * x, axis=-1, keepdims=True) + EPS)


def _sigmoid(x):
    return 1.0 / (1.0 + jnp.exp(-x))


def _mod_body(cond_ref, w_ref, b_ref, o_ref):
    c = cond_ref[...]
    s = (c * _sigmoid(c)).astype(BF16)
    o_ref[...] = _dot(s, w_ref[...].astype(BF16)) + b_ref[...]


def _modulation(cond, w, b):
    n = N_MOD * D_MODEL
    out = pl.pallas_call(
        _mod_body,
        grid=(n // MOD_TN,),
        in_specs=[
            pl.BlockSpec((MOD_ROWS, D_MODEL), lambda j: (0, 0)),
            pl.BlockSpec((D_MODEL, MOD_TN), lambda j: (0, j)),
            pl.BlockSpec((1, MOD_TN), lambda j: (0, j)),
        ],
        out_specs=pl.BlockSpec((MOD_ROWS, MOD_TN), lambda j: (0, j)),
        out_shape=jax.ShapeDtypeStruct((MOD_ROWS, n), F32),
        compiler_params=pltpu.CompilerParams(
            dimension_semantics=("parallel",), vmem_limit_bytes=VMEM_LIMIT),
        name="modulation",
    )(cond, w, b.reshape(1, n))
    return out.reshape(MOD_ROWS, N_MOD, D_MODEL)


def _gate_ab(hu, n, wg_ref, bg_ref, lam_ref):
    cols = slice(n * LRU_BLOCK, (n + 1) * LRU_BLOCK)
    o = _dot(hu.astype(BF16), wg_ref[n])
    t_r = jnp.tanh(o[:, :LRU_BLOCK] + bg_ref[0:1, cols])
    t_i = jnp.tanh(o[:, LRU_BLOCK:] + bg_ref[1:2, cols])
    nl = -lam_ref[:, cols]
    softplus = jnp.maximum(nl, 0.0) + jnp.log1p(jnp.exp(-jnp.abs(nl)))
    log_a = (t_r + 1.0) * ((-0.5 * LRU_C) * softplus)
    a = jnp.exp(log_a)
    s = jnp.tanh(log_a) * (-1.0 - a * a)
    root = s * lax.rsqrt(jnp.maximum(s, SQRT_FLOOR))
    b = root * (hu + hu * t_i)
    return a, b


def _scan(a, b, h_in, reverse):
    tc = a.shape[0]
    sub = tc // SUBLANES
    sub_id = lax.broadcasted_iota(jnp.int32, (SUBLANES, LANES), 0)
    steps = range(sub - 1, -1, -1) if reverse else range(sub)
    order = range(SUBLANES - 1, -1, -1) if reverse else range(SUBLANES)
    slabs, last = [], []
    for k in range(a.shape[1] // LANES):
        lanes = slice(k * LANES, (k + 1) * LANES)
        hs, ps = [None] * sub, [None] * sub
        h = p = None
        for j in steps:
            av = a[j * SUBLANES:(j + 1) * SUBLANES, lanes]
            bv = b[j * SUBLANES:(j + 1) * SUBLANES, lanes]
            h = bv if h is None else av * h + bv
            p = av if p is None else av * p
            hs[j], ps[j] = h, p
        carry = h_in[:, lanes]
        carry_in = jnp.zeros((SUBLANES, LANES), F32)
        for s in order:
            carry_in = jnp.where(sub_id == s, carry, carry_in)
            carry = p[s:s + 1, :] * carry + h[s:s + 1, :]
        slabs.append(jnp.concatenate([ps[j] * carry_in + hs[j] for j in range(sub)], axis=0))
        last.append(carry)
    return jnp.concatenate(slabs, axis=1), jnp.concatenate(last, axis=1)


def _pitch(tc):
    return tc // SUBLANES + SUBLANES


def _permute_in(e, ext_s, tc, n_tiles):
    sub = tc // SUBLANES
    pitch = _pitch(tc)
    row = lax.broadcasted_iota(jnp.int32, (SUBLANES, LANES), 0)
    for k in range(N_SLABS):
        lanes = slice(k * LANES, (k + 1) * LANES)
        ext_s[k, 0:HALO, :] = e[0:HALO, lanes]
        for s in range(SUBLANES):
            base = HALO + s * pitch
            end = HALO + (s + 1) * sub
            ext_s[k, base:base + sub, :] = e[end - sub:end, lanes]
            ext_s[k, base + sub:base + pitch, :] = jnp.where(
                row < SUBLANES // 2, e[end:end + SUBLANES, lanes], e[end - SUBLANES:end, lanes])
    tiles = [jnp.concatenate([ext_s[k, pl.ds(HALO - 2 + q, SUBLANES, stride=pitch), :] for k in range(N_SLABS)],
                             axis=1) for q in range(n_tiles)]
    return jnp.concatenate(tiles, axis=0)


def _permute_out(r, o_s, tc):
    sub = tc // SUBLANES
    pitch = _pitch(tc)
    for k in range(N_SLABS):
        for j in range(sub):
            o_s[k, pl.ds(j, SUBLANES, stride=pitch), :] = r[j * SUBLANES:(j + 1) * SUBLANES, k * LANES:(k + 1) * LANES]
    return jnp.concatenate(
        [jnp.concatenate([o_s[k, s * pitch:s * pitch + sub, :] for s in range(SUBLANES)], axis=0)
         for k in range(N_SLABS)], axis=1)


def _fwd_body(*refs, nc, tc, has_pos):
    if has_pos:
        (x_ref, xp_ref, xn_ref, pos_ref, posp_ref, posn_ref, mod_ref, g_ref, w_ref, cw_ref, cb_ref, wg_ref, bg_ref,
         lam_ref, h0_ref, zg_ref, hu_ref, hf_ref, sf_ref, carry_ref, ext_s) = refs
    else:
        (x_ref, xp_ref, xn_ref, mod_ref, g_ref, w_ref, cw_ref, cb_ref, wg_ref, bg_ref,
         lam_ref, h0_ref, zg_ref, hu_ref, hf_ref, sf_ref, carry_ref, ext_s) = refs
    c = pl.program_id(1)
    sub = tc // SUBLANES
    n_tiles = sub + CONV_W

    @pl.when(c == 0)
    def _():
        carry_ref[...] = h0_ref[0]

    x, xp, xn = x_ref[0], xp_ref[0], xn_ref[0]
    if has_pos:
        x, xp, xn = x + pos_ref[...], xp + posp_ref[...], xn + posn_ref[...]
    gs = g_ref[...] * (1.0 + mod_ref[0, 1:2, :])
    sh = mod_ref[0, 0:1, :]
    e = jnp.concatenate([_rms(xp) * gs + sh, _rms(x) * gs + sh, _rms(xn) * gs + sh], axis=0)
    hp = _permute_in(e, ext_s, tc, n_tiles).astype(BF16)
    zg_ref[0] = _dot(hp[2 * SUBLANES:2 * SUBLANES + tc], w_ref[:, :D_RNN])
    rec = _dot(hp, w_ref[:, D_RNN:])
    row = lax.broadcasted_iota(jnp.int32, (2 * SUBLANES, D_RNN), 0) & (SUBLANES - 1)
    head = jnp.where(row < jnp.where(c == 0, 1, 0), 0.0, rec[0:2 * SUBLANES])
    tail = jnp.where(row > jnp.where(c == nc - 1, SUBLANES - 2, SUBLANES - 1), 0.0, rec[(sub + 2) * SUBLANES:])
    rec = jnp.concatenate([head, rec[2 * SUBLANES:(sub + 2) * SUBLANES], tail], axis=0)
    hu = rec[0:tc] * cw_ref[0:1, :]
    for k in range(1, CONV_W):
        hu = hu + rec[k * SUBLANES:k * SUBLANES + tc] * cw_ref[k:k + 1, :]
    hu = hu + cb_ref[...]
    hu_ref[0] = hu
    lasts = []
    for n in range(N_LRU_BLOCKS):
        cols = slice(n * LRU_BLOCK, (n + 1) * LRU_BLOCK)
        a, b = _gate_ab(hu[:, cols], n, wg_ref, bg_ref, lam_ref)
        hf, h_last = _scan(a, b, carry_ref[:, cols], False)
        hf_ref[0, :, cols] = hf
        lasts.append(h_last)
    h_last = jnp.concatenate(lasts, axis=1)
    carry_ref[...] = h_last
    sf_ref[0] = h_last


def _ffn_head(x, mod_ref, pre_ref):
    sh = mod_ref[0, 3:4, :]
    sc = mod_ref[0, 4:5, :]
    return (_rms(x) * (pre_ref[...] * (1.0 + sc)) + sh).astype(BF16)


def _ffn_steps(h, wi_ref, wo_ref, act_ref, out):
    def chunk(n):
        def f():
            g = _dot(h, wi_ref[:, n * FF_CHUNK:(n + 1) * FF_CHUNK])
            v = _dot(h, wi_ref[:, D_FF + n * FF_CHUNK:D_FF + (n + 1) * FF_CHUNK])
            act_ref[:, n * FF_CHUNK:(n + 1) * FF_CHUNK] = (g * _sigmoid(g) * v).astype(BF16)
        return f

    def down():
        out.append(_dot(act_ref[...], wo_ref[...]))

    return [chunk(n) for n in range(N_FF_CHUNKS)] + [down]


def _ffn_tail(x, y, mod_ref, post_ref):
    return x + _rms(y) * (post_ref[...] * mod_ref[0, 5:6, :])


def _trace_interleaved(major, minor):
    done = 0
    for k, step in enumerate(major):
        step()
        while done < (k + 1) * len(minor) // len(major):
            minor[done]()
            done += 1


def _bwd_ffn_body(*refs, nc, tc, n_chunks, has_pos):
    if has_pos:
        (hu_ref, zg_ref, hf_ref, x_ref, pos_ref, mod_ref, wg_ref, bg_ref, lam_ref, h0_ref, wo_ref, pg_ref,
         fmod_ref, pre_ref, post_ref, wi_ref, wo2_ref, x2_ref, sb_ref,
         carry_ref, o_s, x1_s, act_ref, mixed_ref) = refs
    else:
        (hu_ref, zg_ref, hf_ref, x_ref, mod_ref, wg_ref, bg_ref, lam_ref, h0_ref, wo_ref, pg_ref,
         fmod_ref, pre_ref, post_ref, wi_ref, wo2_ref, x2_ref, sb_ref,
         carry_ref, o_s, x1_s, act_ref, mixed_ref) = refs
    i = pl.program_id(0)
    c = jnp.minimum(i, n_chunks - 1) % nc

    @pl.when(i == 0)
    def _():
        x1_s[...] = jnp.zeros(x1_s.shape, F32)

    @pl.when(c == 0)
    def _():
        carry_ref[...] = h0_ref[0]

    st = {}
    lasts = [None] * N_LRU_BLOCKS

    def block(n):
        def f():
            cols = slice(n * LRU_BLOCK, (n + 1) * LRU_BLOCK)
            a, b = _gate_ab(hu_ref[0, :, cols], n, wg_ref, bg_ref, lam_ref)
            hb, lasts[n] = _scan(a, b, carry_ref[:, cols], True)
            mixed_ref[:, cols] = ((hf_ref[0, :, cols] + hb) * jax.nn.gelu(zg_ref[0, :, cols])).astype(BF16)
        return f

    def out_proj():
        st['y'] = _dot(mixed_ref[...], wo_ref[...])

    def residual():
        x = x_ref[0]
        if has_pos:
            x = x + pos_ref[...]
        x1_s[i % 2] = x + _permute_out(_rms(st['y']) * (pg_ref[...] * mod_ref[0, 2:3, :]), o_s, tc)

    mixer = [block(n) for n in range(N_LRU_BLOCKS)] + [out_proj, residual]

    xf = x1_s[(i + 1) % 2]
    yf = []
    _trace_interleaved(_ffn_steps(_ffn_head(xf, fmod_ref, pre_ref), wi_ref, wo2_ref, act_ref, yf), mixer)
    x2_ref[0] = _ffn_tail(xf, yf[0], fmod_ref, post_ref)
    h_last = jnp.concatenate(lasts, axis=1)

    @pl.when(i < n_chunks)
    def _():
        carry_ref[...] = h_last
        sb_ref[0] = h_last


def _pool_group(ext, gi, c, pw_ref, t_len):
    n = ext.shape[0]
    tc = n - 2 * HALO
    win = POOL_WINDOWS[gi]
    e = ext[:, gi * POOL_GROUP:(gi + 1) * POOL_GROUP]
    w = e + pltpu.roll(e, 1, 0)
    half = 1
    while 2 * half < win:
        w = pltpu.roll(w, half, 0) + pltpu.roll(w, n - half, 0)
        half *= 2
    t = c * tc + lax.broadcasted_iota(jnp.int32, (tc, POOL_GROUP), 0)
    cnt = (jnp.minimum(t + win // 2, t_len) - jnp.maximum(t - win // 2, 0)).astype(F32)
    pooled = w[HALO:n - HALO] / cnt - e[HALO:n - HALO]
    return _dot(pooled.astype(BF16), pw_ref[gi])


def _pool_ffn_body(x_ref, xp_ref, xn_ref, mod_ref, mpre_ref, mpost_ref, pw_ref, pb_ref, ps_ref,
                   fmod_ref, pre_ref, post_ref, wi_ref, wo_ref, o_ref, x3_s, act_ref, *, nc, t_len, n_chunks):
    i = pl.program_id(0)
    c = jnp.minimum(i, n_chunks - 1) % nc

    @pl.when(i == 0)
    def _():
        x3_s[...] = jnp.zeros(x3_s.shape, F32)

    st = {}
    ys = [None] * len(POOL_WINDOWS)

    def pre_norm():
        gs = mpre_ref[...] * (1.0 + mod_ref[0, 1:2, :])
        sh = mod_ref[0, 0:1, :]
        st['ext'] = jnp.concatenate([jnp.where(c > 0, _rms(xp_ref[0]) * gs + sh, 0.0),
                                     _rms(x_ref[0]) * gs + sh,
                                     jnp.where(c < nc - 1, _rms(xn_ref[0]) * gs + sh, 0.0)], axis=0)

    def group(gi):
        def f():
            ys[gi] = _pool_group(st['ext'], gi, c, pw_ref, t_len)
        return f

    def residual():
        y = (jnp.concatenate(ys, axis=1) + pb_ref[...]) * ps_ref[...]
        x3_s[i % 2] = x_ref[0] + _rms(y) * (mpost_ref[...] * mod_ref[0, 2:3, :])

    mixer = [pre_norm] + [group(gi) for gi in range(len(POOL_WINDOWS))] + [residual]

    xf = x3_s[(i + 1) % 2]
    yf = []
    _trace_interleaved(_ffn_steps(_ffn_head(xf, fmod_ref, pre_ref), wi_ref, wo_ref, act_ref, yf), mixer)
    o_ref[0] = _ffn_tail(xf, yf[0], fmod_ref, post_ref)


def _const_spec(shape, single=False):
    nd = len(shape)
    if single:
        return pl.BlockSpec(shape, lambda *_: (0,) * nd, pipeline_mode=pl.Buffered(1))
    return pl.BlockSpec(shape, lambda *_: (0,) * nd)


def _ffn_specs(d):
    return [_const_spec((1, d)), _const_spec((1, d)),
            _const_spec((d, 2 * D_FF), single=True), _const_spec((D_FF, d), single=True)]


def _layer0(x, pos, mod, mod_row, h0, p, tc):
    bsz, t_len, d = x.shape
    nc = t_len // tc
    n_chunks = bsz * nc
    hb8 = tc // HALO
    n_hblk = t_len // HALO
    has_pos = pos is not None
    act = jax.ShapeDtypeStruct((bsz, t_len, d), F32)
    st = jax.ShapeDtypeStruct((bsz, 1, d), F32)

    prev_blk = lambda c: jnp.maximum(c * hb8 - 1, 0)
    next_blk = lambda c: jnp.minimum((c + 1) * hb8, n_hblk - 1)
    row_spec = pl.BlockSpec((1, d), lambda b, c: (0, 0))
    chunk = pl.BlockSpec((1, tc, d), lambda b, c: (b, c, 0))
    state_spec = pl.BlockSpec((1, 1, d), lambda b, c: (b, 0, 0))
    pos_specs = [pl.BlockSpec((tc, d), lambda b, c: (c, 0)),
                 pl.BlockSpec((HALO, d), lambda b, c: (prev_blk(c), 0)),
                 pl.BlockSpec((HALO, d), lambda b, c: (next_blk(c), 0))] if has_pos else []
    in_specs = ([chunk,
                 pl.BlockSpec((1, HALO, d), lambda b, c: (b, prev_blk(c), 0)),
                 pl.BlockSpec((1, HALO, d), lambda b, c: (b, next_blk(c), 0))]
                + pos_specs
                + [pl.BlockSpec((1, N_MOD, d), lambda b, c: (mod_row(b), 0, 0)), row_spec,
                   _const_spec((d, 2 * D_RNN)), _const_spec((CONV_W, d)), row_spec,
                   _const_spec((N_LRU_BLOCKS, LRU_BLOCK, 2 * LRU_BLOCK)), _const_spec((2, d)), row_spec, state_spec])
    args = ([x, x, x] + ([pos, pos, pos] if has_pos else [])
            + [mod, p['mix_pre_g'], p['w_in'], p['conv_w_half'], p['conv_b_half'], p['wg'][0], p['bg_half'][0],
               p['lam'][0], h0[:, 0:1]])
    zg, hu, hf, sf = pl.pallas_call(
        functools.partial(_fwd_body, nc=nc, tc=tc, has_pos=has_pos),
        grid=(bsz, nc), in_specs=in_specs, out_specs=[chunk, chunk, chunk, state_spec],
        out_shape=[act, act, act, st],
        scratch_shapes=[pltpu.VMEM((1, d), F32),
                        pltpu.VMEM((N_SLABS, HALO + SUBLANES * _pitch(tc), LANES), F32)],
        compiler_params=pltpu.CompilerParams(
            dimension_semantics=("parallel", "arbitrary"), vmem_limit_bytes=VMEM_LIMIT),
        name="l0_fwd",
    )(*args)

    def mix_at(i):
        im = jnp.minimum(i, n_chunks - 1)
        return im // nc, nc - 1 - im % nc

    def ffn_at(i):
        return mix_at(jnp.maximum(i - 1, 0))

    mchunk = pl.BlockSpec((1, tc, d), lambda i: (*mix_at(i), 0))
    in_specs = ([mchunk, mchunk, mchunk, mchunk]
                + ([pl.BlockSpec((tc, d), lambda i: (mix_at(i)[1], 0))] if has_pos else [])
                + [pl.BlockSpec((1, N_MOD, d), lambda i: (mod_row(mix_at(i)[0]), 0, 0)),
                   _const_spec((N_LRU_BLOCKS, LRU_BLOCK, 2 * LRU_BLOCK)), _const_spec((2, d)), _const_spec((1, d)),
                   pl.BlockSpec((1, 1, d), lambda i: (mix_at(i)[0], 0, 0)),
                   _const_spec((D_RNN, d)), _const_spec((1, d)),
                   pl.BlockSpec((1, N_MOD, d), lambda i: (mod_row(ffn_at(i)[0]), 0, 0))]
                + _ffn_specs(d))
    args = ([hu, zg, hf, x] + ([pos] if has_pos else [])
            + [mod, p['wg'][1], p['bg_half'][1], p['lam'][1], h0[:, 1:2], p['w_out'], p['mix_post_g'],
               mod, p['ffn_pre_g'], p['ffn_post_g'], p['ffn_w_in'], p['ffn_w_out']])
    x2, sb = pl.pallas_call(
        functools.partial(_bwd_ffn_body, nc=nc, tc=tc, n_chunks=n_chunks, has_pos=has_pos),
        grid=(n_chunks + 1,), in_specs=in_specs,
        out_specs=[pl.BlockSpec((1, tc, d), lambda i: (*ffn_at(i), 0)),
                   pl.BlockSpec((1, 1, d), lambda i: (mix_at(i)[0], 0, 0))],
        out_shape=[act, st],
        scratch_shapes=[pltpu.VMEM((1, d), F32), pltpu.VMEM((N_SLABS, SUBLANES * _pitch(tc), LANES), F32),
                        pltpu.VMEM((2, tc, d), F32), pltpu.VMEM((tc, D_FF), BF16), pltpu.VMEM((tc, D_RNN), BF16)],
        compiler_params=pltpu.CompilerParams(dimension_semantics=("arbitrary",), vmem_limit_bytes=VMEM_LIMIT),
        name="l0_bwd_ffn",
    )(*args)
    return x2, jnp.concatenate([sf, sb], axis=1)


def _layer1(x, mod, mod_row, p, tc):
    bsz, t_len, d = x.shape
    nc = t_len // tc
    n_chunks = bsz * nc
    hb8 = tc // HALO
    n_hblk = t_len // HALO

    def mix_at(i):
        im = jnp.minimum(i, n_chunks - 1)
        return im // nc, im % nc

    def ffn_at(i):
        return mix_at(jnp.maximum(i - 1, 0))

    def halo_spec(blk_of):
        return pl.BlockSpec((1, HALO, d), lambda i: (mix_at(i)[0], blk_of(mix_at(i)[1]), 0))

    return pl.pallas_call(
        functools.partial(_pool_ffn_body, nc=nc, t_len=t_len, n_chunks=n_chunks),
        grid=(n_chunks + 1,),
        in_specs=[pl.BlockSpec((1, tc, d), lambda i: (*mix_at(i), 0)),
                  halo_spec(lambda c: jnp.maximum(c * hb8 - 1, 0)),
                  halo_spec(lambda c: jnp.minimum((c + 1) * hb8, n_hblk - 1)),
                  pl.BlockSpec((1, N_MOD, d), lambda i: (mod_row(mix_at(i)[0]), 0, 0)),
                  _const_spec((1, d)), _const_spec((1, d)),
                  _const_spec((len(POOL_WINDOWS), POOL_GROUP, POOL_GROUP)), _const_spec((1, d)), _const_spec((1, d)),
                  pl.BlockSpec((1, N_MOD, d), lambda i: (mod_row(ffn_at(i)[0]), 0, 0))] + _ffn_specs(d),
        out_specs=pl.BlockSpec((1, tc, d), lambda i: (*ffn_at(i), 0)),
        out_shape=jax.ShapeDtypeStruct((bsz, t_len, d), F32),
        scratch_shapes=[pltpu.VMEM((2, tc, d), F32), pltpu.VMEM((tc, D_FF), BF16)],
        compiler_params=pltpu.CompilerParams(dimension_semantics=("arbitrary",), vmem_limit_bytes=VMEM_LIMIT),
        name="l1_ffn",
    )(x, x, x, mod, p['mix_pre_g'], p['mix_post_g'], p['pool_w'], p['pool_b'], p['pool_scale'],
      mod, p['ffn_pre_g'], p['ffn_post_g'], p['ffn_w_in'], p['ffn_w_out'])


def _grid_pos_emb(t_len):
    rows = t_len // GRID_W
    quarter = D_MODEL // 4
    omega = 1.0 / (POS_THETA ** (jnp.arange(quarter, dtype=F32) / quarter))
    ang = jnp.arange(max(rows, GRID_W), dtype=F32)[:, None] * omega[None, :]
    table = jnp.concatenate([jnp.sin(ang), jnp.cos(ang)], axis=-1)
    half = D_MODEL // 2
    by_row = jnp.broadcast_to(table[:rows, None, :], (rows, GRID_W, half))
    by_col = jnp.broadcast_to(table[None, :GRID_W, :], (rows, GRID_W, half))
    return jnp.concatenate([by_row, by_col], axis=-1).reshape(t_len, D_MODEL)


def _row(v):
    return v.reshape(1, -1)


def kernel(x_prompt, x_sample, state_l0_rglru, c, c_ctx, l0_mod_w, l0_mod_b, l0_mix_pre_g, l0_mix_post_g, l0_w_in, l0_conv_w, l0_conv_b, l0_gate_a_w, l0_gate_a_b, l0_gate_x_w, l0_gate_x_b, l0_lambda, l0_w_out, l0_ffn_pre_g, l0_ffn_post_g, l0_ffn_w_in, l0_ffn_w_out, l1_mod_w, l1_mod_b, l1_mix_pre_g, l1_mix_post_g, l1_pool_w, l1_pool_b, l1_pool_scale, l1_ffn_pre_g, l1_ffn_post_g, l1_ffn_w_in, l1_ffn_w_out):
    n_ctx, t_ctx, d = x_prompt.shape
    n_lat, t_lat, _ = x_sample.shape

    cond = jnp.concatenate(
        [c, c_ctx[None, :], jnp.zeros((MOD_ROWS - n_lat - 1, d), F32)], axis=0)
    mod0 = _modulation(cond, l0_mod_w, l0_mod_b)
    mod1 = _modulation(cond, l1_mod_w, l1_mod_b)

    p0 = dict(
        mix_pre_g=_row(l0_mix_pre_g), mix_post_g=_row(l0_mix_post_g), w_in=l0_w_in.astype(BF16),
        conv_w_half=0.5 * l0_conv_w, conv_b_half=_row(0.5 * l0_conv_b),
        wg=[jnp.concatenate([l0_gate_a_w[k], l0_gate_x_w[k]], axis=-1).astype(BF16) for k in range(2)],
        bg_half=[0.5 * jnp.stack([l0_gate_a_b[k], l0_gate_x_b[k]], axis=0) for k in range(2)],
        lam=[_row(l0_lambda[k]) for k in range(2)],
        w_out=l0_w_out.astype(BF16),
        ffn_pre_g=_row(l0_ffn_pre_g), ffn_post_g=_row(l0_ffn_post_g),
        ffn_w_in=l0_ffn_w_in.astype(BF16), ffn_w_out=l0_ffn_w_out.astype(BF16))
    p1 = dict(
        mix_pre_g=_row(l1_mix_pre_g), mix_post_g=_row(l1_mix_post_g), pool_w=l1_pool_w.astype(BF16),
        pool_b=_row(l1_pool_b), pool_scale=_row(l1_pool_scale),
        ffn_pre_g=_row(l1_ffn_pre_g), ffn_post_g=_row(l1_ffn_post_g),
        ffn_w_in=l1_ffn_w_in.astype(BF16), ffn_w_out=l1_ffn_w_out.astype(BF16))

    def run(x, pos, mod_row, h0):
        t_len = x.shape[1]
        x2, state = _layer0(x, pos, mod0, mod_row, h0, p0, min(CHUNK_L0, t_len))
        return _layer1(x2, mod1, mod_row, p1, min(CHUNK_L1, t_len)), state

    y_prompt, new_state = run(x_prompt, None, lambda b: CTX_ROW, jnp.zeros((n_ctx, 2, D_RNN), F32))
    y_sample, _ = run(x_sample, _grid_pos_emb(t_lat), lambda b: b, state_l0_rglru)
    return y_prompt, y_sample, new_state
```

```python
import functools

import jax
import jax.numpy as jnp
from jax import lax
from jax.experimental import pallas as pl
from jax.experimental.pallas import tpu as pltpu

D_MODEL = 1024
D_RNN = D_MODEL
N_LRU_BLOCKS = 4
LRU_BLOCK = D_RNN // N_LRU_BLOCKS
CONV_W = 4
LRU_C = 8.0
POOL_WINDOWS = (2, 4, 8, 16)
POOL_GROUP = D_MODEL // len(POOL_WINDOWS)
D_FF = 2816
N_MOD = 6
EPS = 1e-6
POS_THETA = 10000.0
GRID_W = 64

LANES = 128
SUBLANES = 8
HALO = SUBLANES
N_SLABS = D_RNN // LANES
FF_CHUNK = 256
N_FF_CHUNKS = D_FF // FF_CHUNK
MOD_ROWS = 16
CTX_ROW = 8
MOD_TN = 1536
SQRT_FLOOR = 1e-36
CHUNK_L0 = 256
CHUNK_L1 = 512
FWD_CHUNKS = 2
BWD_CHUNKS = 2

VMEM_LIMIT = 56 * 1024 * 1024
VMEM_LIMIT_BIG = 62 * 1024 * 1024

F32 = jnp.float32
BF16 = jnp.bfloat16


def _dot(a, b):
    return jnp.dot(a, b, preferred_element_type=F32)


def _rms(x):
    return x * lax.rsqrt(jnp.mean(x * x, axis=-1, keepdims=True) + EPS)


def _sigmoid(x):
    return 1.0 / (1.0 + jnp.exp(-x))


def _mod_body(cond_ref, w_ref, b_ref, o_ref):
    c = cond_ref[...]
    s = (c * _sigmoid(c)).astype(BF16)
    o_ref[...] = _dot(s, w_ref[...].astype(BF16)) + b_ref[...]


def _modulation(cond, w, b):
    n = N_MOD * D_MODEL
    out = pl.pallas_call(
        _mod_body,
        grid=(n // MOD_TN,),
        in_specs=[
            pl.BlockSpec((MOD_ROWS, D_MODEL), lambda j: (0, 0)),
            pl.BlockSpec((D_MODEL, MOD_TN), lambda j: (0, j)),
            pl.BlockSpec((1, MOD_TN), lambda j: (0, j)),
        ],
        out_specs=pl.BlockSpec((MOD_ROWS, MOD_TN), lambda j: (0, j)),
        out_shape=jax.ShapeDtypeStruct((MOD_ROWS, n), F32),
        compiler_params=pltpu.CompilerParams(
            dimension_semantics=("parallel",), vmem_limit_bytes=VMEM_LIMIT),
        name="modulation",
    )(cond, w, b.reshape(1, n))
    return out.reshape(MOD_ROWS, N_MOD, D_MODEL)


def _gate_ab(hu, n, wg_ref, bg_ref, lam_ref):
    cols = slice(n * LRU_BLOCK, (n + 1) * LRU_BLOCK)
    o = _dot(hu.astype(BF16), wg_ref[n])
    t_r = jnp.tanh(o[:, :LRU_BLOCK] + bg_ref[0:1, cols])
    t_i = jnp.tanh(o[:, LRU_BLOCK:] + bg_ref[1:2, cols])
    nl = -lam_ref[:, cols]
    softplus = jnp.maximum(nl, 0.0) + jnp.log1p(jnp.exp(-jnp.abs(nl)))
    log_a = (t_r + 1.0) * ((-0.5 * LRU_C) * softplus)
    a = jnp.exp(log_a)
    s = jnp.tanh(log_a) * (-1.0 - a * a)
    root = s * lax.rsqrt(jnp.maximum(s, SQRT_FLOOR))
    b = root * (hu + hu * t_i)
    return a, b


def _scan(a, b, h_in, reverse):
    tc = a.shape[0]
    sub = tc // SUBLANES
    sub_id = lax.broadcasted_iota(jnp.int32, (SUBLANES, LANES), 0)
    steps = range(sub - 1, -1, -1) if reverse else range(sub)
    order = range(SUBLANES - 1, -1, -1) if reverse else range(SUBLANES)
    slabs, last = [], []
    for k in range(a.shape[1] // LANES):
        lanes = slice(k * LANES, (k + 1) * LANES)
        hs, ps = [None] * sub, [None] * sub
        h = p = None
        for j in steps:
            av = a[j * SUBLANES:(j + 1) * SUBLANES, lanes]
            bv = b[j * SUBLANES:(j + 1) * SUBLANES, lanes]
            h = bv if h is None else av * h + bv
            p = av if p is None else av * p
            hs[j], ps[j] = h, p
        carry = h_in[:, lanes]
        carry_in = jnp.zeros((SUBLANES, LANES), F32)
        for s in order:
            carry_in = jnp.where(sub_id == s, carry, carry_in)
            carry = p[s:s + 1, :] * carry + h[s:s + 1, :]
        slabs.append(jnp.concatenate([ps[j] * carry_in + hs[j] for j in range(sub)], axis=0))
        last.append(carry)
    return jnp.concatenate(slabs, axis=1), jnp.concatenate(last, axis=1)


def _pitch(tc):
    return tc // SUBLANES + SUBLANES


def _permute_in(e, ext_s, tc, n_tiles):
    sub = tc // SUBLANES
    pitch = _pitch(tc)
    row = lax.broadcasted_iota(jnp.int32, (SUBLANES, LANES), 0)
    for k in range(N_SLABS):
        lanes = slice(k * LANES, (k + 1) * LANES)
        ext_s[k, 0:HALO, :] = e[0:HALO, lanes]
        for s in range(SUBLANES):
            base = HALO + s * pitch
            end = HALO + (s + 1) * sub
            ext_s[k, base:base + sub, :] = e[end - sub:end, lanes]
            ext_s[k, base + sub:base + pitch, :] = jnp.where(
                row < SUBLANES // 2, e[end:end + SUBLANES, lanes], e[end - SUBLANES:end, lanes])
    tiles = [jnp.concatenate([ext_s[k, pl.ds(HALO - 2 + q, SUBLANES, stride=pitch), :] for k in range(N_SLABS)],
                             axis=1) for q in range(n_tiles)]
    return jnp.concatenate(tiles, axis=0)


def _permute_out(r, o_s, tc):
    sub = tc // SUBLANES
    pitch = _pitch(tc)
    for k in range(N_SLABS):
        for j in range(sub):
            o_s[k, pl.ds(j, SUBLANES, stride=pitch), :] = r[j * SUBLANES:(j + 1) * SUBLANES, k * LANES:(k + 1) * LANES]
    return jnp.concatenate(
        [jnp.concatenate([o_s[k, s * pitch:s * pitch + sub, :] for s in range(SUBLANES)], axis=0)
         for k in range(N_SLABS)], axis=1)


def _fwd_body(*refs, nc, tc, has_pos):
    if has_pos:
        (x_ref, xp_ref, xn_ref, pos_ref, posp_ref, posn_ref, mod_ref, g_ref, w_ref, cw_ref, cb_ref, wg_ref, bg_ref,
         lam_ref, h0_ref, zg_ref, hu_ref, hf_ref, sf_ref, carry_ref, ext_s) = refs
    else:
        (x_ref, xp_ref, xn_ref, mod_ref, g_ref, w_ref, cw_ref, cb_ref, wg_ref, bg_ref,
         lam_ref, h0_ref, zg_ref, hu_ref, hf_ref, sf_ref, carry_ref, ext_s) = refs
    c = pl.program_id(1)
    n_sub = x_ref.shape[1] // tc
    sub = tc // SUBLANES
    n_tiles = sub + CONV_W

    @pl.when(c == 0)
    def _():
        carry_ref[...] = h0_ref[0]

    x, xp, xn = x_ref[0], xp_ref[0], xn_ref[0]
    if has_pos:
        x, xp, xn = x + pos_ref[...], xp + posp_ref[...], xn + posn_ref[...]
    gs = g_ref[...] * (1.0 + mod_ref[0, 1:2, :])
    sh = mod_ref[0, 0:1, :]
    h_all = jnp.concatenate([_rms(xp) * gs + sh, _rms(x) * gs + sh, _rms(xn) * gs + sh], axis=0)
    row = lax.broadcasted_iota(jnp.int32, (2 * SUBLANES, D_RNN), 0) & (SUBLANES - 1)
    carry = carry_ref[...]
    for q in range(n_sub):
        rows = slice(q * tc, (q + 1) * tc)
        e = h_all[q * tc:(q + 1) * tc + 2 * HALO]
        hp = _permute_in(e, ext_s, tc, n_tiles).astype(BF16)
        zg_ref[0, rows, :] = _dot(hp[2 * SUBLANES:2 * SUBLANES + tc], w_ref[:, :D_RNN])
        rec = _dot(hp, w_ref[:, D_RNN:])
        if q == 0:
            head = jnp.where(row < jnp.where(c == 0, 1, 0), 0.0, rec[0:2 * SUBLANES])
            rec = jnp.concatenate([head, rec[2 * SUBLANES:]], axis=0)
        if q == n_sub - 1:
            tail = jnp.where(row > jnp.where(c == nc - 1, SUBLANES - 2, SUBLANES - 1), 0.0,
                             rec[(sub + 2) * SUBLANES:])
            rec = jnp.concatenate([rec[:(sub + 2) * SUBLANES], tail], axis=0)
        hu = rec[0:tc] * cw_ref[0:1, :]
        for k in range(1, CONV_W):
            hu = hu + rec[k * SUBLANES:k * SUBLANES + tc] * cw_ref[k:k + 1, :]
        hu = hu + cb_ref[...]
        hu_ref[0, rows, :] = hu
        lasts = []
        for n in range(N_LRU_BLOCKS):
            cols = slice(n * LRU_BLOCK, (n + 1) * LRU_BLOCK)
            a, b = _gate_ab(hu[:, cols], n, wg_ref, bg_ref, lam_ref)
            hf, h_last = _scan(a, b, carry[:, cols], False)
            hf_ref[0, rows, cols] = hf
            lasts.append(h_last)
        carry = jnp.concatenate(lasts, axis=1)
    carry_ref[...] = carry
    sf_ref[0] = carry


def _ffn_head(x, mod_ref, pre_ref):
    sh = mod_ref[0, 3:4, :]
    sc = mod_ref[0, 4:5, :]
    return (_rms(x) * (pre_ref[...] * (1.0 + sc)) + sh).astype(BF16)


def _ffn_steps(h, wi_ref, wo_ref, act_ref, out):
    def chunk(n):
        def f():
            g = _dot(h, wi_ref[:, n * FF_CHUNK:(n + 1) * FF_CHUNK])
            v = _dot(h, wi_ref[:, D_FF + n * FF_CHUNK:D_FF + (n + 1) * FF_CHUNK])
            act_ref[:, n * FF_CHUNK:(n + 1) * FF_CHUNK] = (g * _sigmoid(g) * v).astype(BF16)
        return f

    def down():
        out.append(_dot(act_ref[...], wo_ref[...]))

    return [chunk(n) for n in range(N_FF_CHUNKS)] + [down]


def _ffn_tail(x, y, mod_ref, post_ref):
    return x + _rms(y) * (post_ref[...] * mod_ref[0, 5:6, :])


def _trace_interleaved(major, minor):
    done = 0
    for k, step in enumerate(major):
        step()
        while done < (k + 1) * len(minor) // len(major):
            minor[done]()
            done += 1


def _bwd_ffn_body(*refs, nc, tc, n_chunks, has_pos):
    if has_pos:
        (hu_ref, zg_ref, hf_ref, x_ref, pos_ref, mod_ref, wg_ref, bg_ref, lam_ref, h0_ref, wo_ref, pg_ref,
         fmod_ref, pre_ref, post_ref, wi_ref, wo2_ref, x2_ref, sb_ref,
         carry_ref, o_s, x1_s, act_ref, mixed_ref) = refs
    else:
        (hu_ref, zg_ref, hf_ref, x_ref, mod_ref, wg_ref, bg_ref, lam_ref, h0_ref, wo_ref, pg_ref,
         fmod_ref, pre_ref, post_ref, wi_ref, wo2_ref, x2_ref, sb_ref,
         carry_ref, o_s, x1_s, act_ref, mixed_ref) = refs
    i = pl.program_id(0)
    c = jnp.minimum(i, n_chunks - 1) % nc
    n_sub = x_ref.shape[1] // tc

    @pl.when(i == 0)
    def _():
        x1_s[...] = jnp.zeros(x1_s.shape, F32)

    @pl.when(c == 0)
    def _():
        carry_ref[...] = h0_ref[0]

    st = {}
    lasts = [None] * N_LRU_BLOCKS

    def block(n, q):
        def f():
            rows = slice(q * tc, (q + 1) * tc)
            cols = slice(n * LRU_BLOCK, (n + 1) * LRU_BLOCK)
            a, b = _gate_ab(hu_ref[0, rows, cols], n, wg_ref, bg_ref, lam_ref)
            h_in = carry_ref[:, cols] if lasts[n] is None else lasts[n]
            hb, lasts[n] = _scan(a, b, h_in, True)
            mixed_ref[rows, cols] = ((hf_ref[0, rows, cols] + hb) * jax.nn.gelu(zg_ref[0, rows, cols])).astype(BF16)
        return f

    def out_proj():
        st['y'] = _dot(mixed_ref[...], wo_ref[...])

    def residual():
        x = x_ref[0]
        if has_pos:
            x = x + pos_ref[...]
        r = _rms(st['y']) * (pg_ref[...] * mod_ref[0, 2:3, :])
        x1_s[i % 2] = x + jnp.concatenate(
            [_permute_out(r[q * tc:(q + 1) * tc], o_s.at[q], tc) for q in range(n_sub)], axis=0)

    mixer = ([block(n, q) for q in range(n_sub - 1, -1, -1) for n in range(N_LRU_BLOCKS)]
             + [out_proj, residual])

    xf = x1_s[(i + 1) % 2]
    yf = []
    _trace_interleaved(_ffn_steps(_ffn_head(xf, fmod_ref, pre_ref), wi_ref, wo2_ref, act_ref, yf), mixer)
    x2_ref[0] = _ffn_tail(xf, yf[0], fmod_ref, post_ref)
    h_last = jnp.concatenate(lasts, axis=1)

    @pl.when(i < n_chunks)
    def _():
        carry_ref[...] = h_last
        sb_ref[0] = h_last


def _pool_group(ext, gi, c, pw_ref, t_len):
    n = ext.shape[0]
    tc = n - 2 * HALO
    win = POOL_WINDOWS[gi]
    e = ext[:, gi * POOL_GROUP:(gi + 1) * POOL_GROUP]
    w = e + pltpu.roll(e, 1, 0)
    half = 1
    while 2 * half < win:
        w = pltpu.roll(w, half, 0) + pltpu.roll(w, n - half, 0)
        half *= 2
    t = c * tc + lax.broadcasted_iota(jnp.int32, (tc, POOL_GROUP), 0)
    cnt = (jnp.minimum(t + win // 2, t_len) - jnp.maximum(t - win // 2, 0)).astype(F32)
    pooled = w[HALO:n - HALO] / cnt - e[HALO:n - HALO]
    return _dot(pooled.astype(BF16), pw_ref[gi])


def _pool_ffn_body(x_ref, xp_ref, xn_ref, mod_ref, mpre_ref, mpost_ref, pw_ref, pb_ref, ps_ref,
                   fmod_ref, pre_ref, post_ref, wi_ref, wo_ref, o_ref, x3_s, act_ref, *, nc, t_len, n_chunks):
    i = pl.program_id(0)
    c = jnp.minimum(i, n_chunks - 1) % nc

    @pl.when(i == 0)
    def _():
        x3_s[...] = jnp.zeros(x3_s.shape, F32)

    st = {}
    ys = [None] * len(POOL_WINDOWS)

    def pre_norm():
        gs = mpre_ref[...] * (1.0 + mod_ref[0, 1:2, :])
        sh = mod_ref[0, 0:1, :]
        st['ext'] = jnp.concatenate([jnp.where(c > 0, _rms(xp_ref[0]) * gs + sh, 0.0),
                                     _rms(x_ref[0]) * gs + sh,
                                     jnp.where(c < nc - 1, _rms(xn_ref[0]) * gs + sh, 0.0)], axis=0)

    def group(gi):
        def f():
            ys[gi] = _pool_group(st['ext'], gi, c, pw_ref, t_len)
        return f

    def residual():
        y = (jnp.concatenate(ys, axis=1) + pb_ref[...]) * ps_ref[...]
        x3_s[i % 2] = x_ref[0] + _rms(y) * (mpost_ref[...] * mod_ref[0, 2:3, :])

    mixer = [pre_norm] + [group(gi) for gi in range(len(POOL_WINDOWS))] + [residual]

    xf = x3_s[(i + 1) % 2]
    yf = []
    _trace_interleaved(_ffn_steps(_ffn_head(xf, fmod_ref, pre_ref), wi_ref, wo_ref, act_ref, yf), mixer)
    o_ref[0] = _ffn_tail(xf, yf[0], fmod_ref, post_ref)


def _const_spec(shape, single=False):
    nd = len(shape)
    if single:
        return pl.BlockSpec(shape, lambda *_: (0,) * nd, pipeline_mode=pl.Buffered(1))
    return pl.BlockSpec(shape, lambda *_: (0,) * nd)


def _ffn_specs(d):
    return [_const_spec((1, d)), _const_spec((1, d)),
            _const_spec((d, 2 * D_FF), single=True), _const_spec((D_FF, d), single=True)]


def _layer0(x, pos, mod, mod_row, h0, p, tc):
    bsz, t_len, d = x.shape
    nc = t_len // tc
    n_chunks = bsz * nc
    n_hblk = t_len // HALO
    has_pos = pos is not None
    act = jax.ShapeDtypeStruct((bsz, t_len, d), F32)
    st = jax.ShapeDtypeStruct((bsz, 1, d), F32)

    tcb = tc * min(FWD_CHUNKS, nc)
    hb8 = tcb // HALO
    prev_blk = lambda c: jnp.maximum(c * hb8 - 1, 0)
    next_blk = lambda c: jnp.minimum((c + 1) * hb8, n_hblk - 1)
    row_spec = pl.BlockSpec((1, d), lambda b, c: (0, 0))
    chunk = pl.BlockSpec((1, tcb, d), lambda b, c: (b, c, 0))
    state_spec = pl.BlockSpec((1, 1, d), lambda b, c: (b, 0, 0))
    pos_specs = [pl.BlockSpec((tcb, d), lambda b, c: (c, 0)),
                 pl.BlockSpec((HALO, d), lambda b, c: (prev_blk(c), 0)),
                 pl.BlockSpec((HALO, d), lambda b, c: (next_blk(c), 0))] if has_pos else []
    in_specs = ([chunk,
                 pl.BlockSpec((1, HALO, d), lambda b, c: (b, prev_blk(c), 0)),
                 pl.BlockSpec((1, HALO, d), lambda b, c: (b, next_blk(c), 0))]
                + pos_specs
                + [pl.BlockSpec((1, N_MOD, d), lambda b, c: (mod_row(b), 0, 0)), row_spec,
                   _const_spec((d, 2 * D_RNN)), _const_spec((CONV_W, d)), row_spec,
                   _const_spec((N_LRU_BLOCKS, LRU_BLOCK, 2 * LRU_BLOCK)), _const_spec((2, d)), row_spec, state_spec])
    args = ([x, x, x] + ([pos, pos, pos] if has_pos else [])
            + [mod, p['mix_pre_g'], p['w_in'], p['conv_w_half'], p['conv_b_half'], p['wg'][0], p['bg_half'][0],
               p['lam'][0], h0[:, 0:1]])
    zg, hu, hf, sf = pl.pallas_call(
        functools.partial(_fwd_body, nc=t_len // tcb, tc=tc, has_pos=has_pos),
        grid=(bsz, t_len // tcb), in_specs=in_specs, out_specs=[chunk, chunk, chunk, state_spec],
        out_shape=[act, act, act, st],
        scratch_shapes=[pltpu.VMEM((1, d), F32),
                        pltpu.VMEM((N_SLABS, HALO + SUBLANES * _pitch(tc), LANES), F32)],
        compiler_params=pltpu.CompilerParams(
            dimension_semantics=("parallel", "arbitrary"), vmem_limit_bytes=VMEM_LIMIT),
        name="l0_fwd",
    )(*args)

    n_sub = min(BWD_CHUNKS, nc)
    tcb = tc * n_sub
    nb = t_len // tcb
    n_blocks = bsz * nb

    def mix_at(i):
        im = jnp.minimum(i, n_blocks - 1)
        return im // nb, nb - 1 - im % nb

    def ffn_at(i):
        return mix_at(jnp.maximum(i - 1, 0))

    mchunk = pl.BlockSpec((1, tcb, d), lambda i: (*mix_at(i), 0))
    in_specs = ([mchunk, mchunk, mchunk, mchunk]
                + ([pl.BlockSpec((tcb, d), lambda i: (mix_at(i)[1], 0))] if has_pos else [])
                + [pl.BlockSpec((1, N_MOD, d), lambda i: (mod_row(mix_at(i)[0]), 0, 0)),
                   _const_spec((N_LRU_BLOCKS, LRU_BLOCK, 2 * LRU_BLOCK)), _const_spec((2, d)), _const_spec((1, d)),
                   pl.BlockSpec((1, 1, d), lambda i: (mix_at(i)[0], 0, 0)),
                   _const_spec((D_RNN, d)), _const_spec((1, d)),
                   pl.BlockSpec((1, N_MOD, d), lambda i: (mod_row(ffn_at(i)[0]), 0, 0))]
                + _ffn_specs(d))
    args = ([hu, zg, hf, x] + ([pos] if has_pos else [])
            + [mod, p['wg'][1], p['bg_half'][1], p['lam'][1], h0[:, 1:2], p['w_out'], p['mix_post_g'],
               mod, p['ffn_pre_g'], p['ffn_post_g'], p['ffn_w_in'], p['ffn_w_out']])
    x2, sb = pl.pallas_call(
        functools.partial(_bwd_ffn_body, nc=nb, tc=tc, n_chunks=n_blocks, has_pos=has_pos),
        grid=(n_blocks + 1,), in_specs=in_specs,
        out_specs=[pl.BlockSpec((1, tcb, d), lambda i: (*ffn_at(i), 0)),
                   pl.BlockSpec((1, 1, d), lambda i: (mix_at(i)[0], 0, 0))],
        out_shape=[act, st],
        scratch_shapes=[pltpu.VMEM((1, d), F32),
                        pltpu.VMEM((n_sub, N_SLABS, SUBLANES * _pitch(tc), LANES), F32),
                        pltpu.VMEM((2, tcb, d), F32), pltpu.VMEM((tcb, D_FF), BF16), pltpu.VMEM((tcb, D_RNN), BF16)],
        compiler_params=pltpu.CompilerParams(dimension_semantics=("arbitrary",), vmem_limit_bytes=VMEM_LIMIT_BIG),
        name="l0_bwd_ffn",
    )(*args)
    return x2, jnp.concatenate([sf, sb], axis=1)


def _layer1(x, mod, mod_row, p, tc):
    bsz, t_len, d = x.shape
    nc = t_len // tc
    n_chunks = bsz * nc
    hb8 = tc // HALO
    n_hblk = t_len // HALO

    def mix_at(i):
        im = jnp.minimum(i, n_chunks - 1)
        return im // nc, im % nc

    def ffn_at(i):
        return mix_at(jnp.maximum(i - 1, 0))

    def halo_spec(blk_of):
        return pl.BlockSpec((1, HALO, d), lambda i: (mix_at(i)[0], blk_of(mix_at(i)[1]), 0))

    return pl.pallas_call(
        functools.partial(_pool_ffn_body, nc=nc, t_len=t_len, n_chunks=n_chunks),
        grid=(n_chunks + 1,),
        in_specs=[pl.BlockSpec((1, tc, d), lambda i: (*mix_at(i), 0)),
                  halo_spec(lambda c: jnp.maximum(c * hb8 - 1, 0)),
                  halo_spec(lambda c: jnp.minimum((c + 1) * hb8, n_hblk - 1)),
                  pl.BlockSpec((1, N_MOD, d), lambda i: (mod_row(mix_at(i)[0]), 0, 0)),
                  _const_spec((1, d)), _const_spec((1, d)),
                  _const_spec((len(POOL_WINDOWS), POOL_GROUP, POOL_GROUP)), _const_spec((1, d)), _const_spec((1, d)),
                  pl.BlockSpec((1, N_MOD, d), lambda i: (mod_row(ffn_at(i)[0]), 0, 0))] + _ffn_specs(d),
        out_specs=pl.BlockSpec((1, tc, d), lambda i: (*ffn_at(i), 0)),
        out_shape=jax.ShapeDtypeStruct((bsz, t_len, d), F32),
        scratch_shapes=[pltpu.VMEM((2, tc, d), F32), pltpu.VMEM((tc, D_FF), BF16)],
        compiler_params=pltpu.CompilerParams(dimension_semantics=("arbitrary",), vmem_limit_bytes=VMEM_LIMIT),
        name="l1_ffn",
    )(x, x, x, mod, p['mix_pre_g'], p['mix_post_g'], p['pool_w'], p['pool_b'], p['pool_scale'],
      mod, p['ffn_pre_g'], p['ffn_post_g'], p['ffn_w_in'], p['ffn_w_out'])


def _grid_pos_emb(t_len):
    rows = t_len // GRID_W
    quarter = D_MODEL // 4
    omega = 1.0 / (POS_THETA ** (jnp.arange(quarter, dtype=F32) / quarter))
    ang = jnp.arange(max(rows, GRID_W), dtype=F32)[:, None] * omega[None, :]
    table = jnp.concatenate([jnp.sin(ang), jnp.cos(ang)], axis=-1)
    half = D_MODEL // 2
    by_row = jnp.broadcast_to(table[:rows, None, :], (rows, GRID_W, half))
    by_col = jnp.broadcast_to(table[None, :GRID_W, :], (rows, GRID_W, half))
    return jnp.concatenate([by_row, by_col], axis=-1).reshape(t_len, D_MODEL)


def _row(v):
    return v.reshape(1, -1)


def kernel(x_prompt, x_sample, state_l0_rglru, c, c_ctx, l0_mod_w, l0_mod_b, l0_mix_pre_g, l0_mix_post_g, l0_w_in, l0_conv_w, l0_conv_b, l0_gate_a_w, l0_gate_a_b, l0_gate_x_w, l0_gate_x_b, l0_lambda, l0_w_out, l0_ffn_pre_g, l0_ffn_post_g, l0_ffn_w_in, l0_ffn_w_out, l1_mod_w, l1_mod_b, l1_mix_pre_g, l1_mix_post_g, l1_pool_w, l1_pool_b, l1_pool_scale, l1_ffn_pre_g, l1_ffn_post_g, l1_ffn_w_in, l1_ffn_w_out):
    n_ctx, t_ctx, d = x_prompt.shape
    n_lat, t_lat, _ = x_sample.shape

    cond = jnp.concatenate(
        [c, c_ctx[None, :], jnp.zeros((MOD_ROWS - n_lat - 1, d), F32)], axis=0)
    mod0 = _modulation(cond, l0_mod_w, l0_mod_b)
    mod1 = _modulation(cond, l1_mod_w, l1_mod_b)

    p0 = dict(
        mix_pre_g=_row(l0_mix_pre_g), mix_post_g=_row(l0_mix_post_g), w_in=l0_w_in.astype(BF16),
        conv_w_half=0.5 * l0_conv_w, conv_b_half=_row(0.5 * l0_conv_b),
        wg=[jnp.concatenate([l0_gate_a_w[k], l0_gate_x_w[k]], axis=-1).astype(BF16) for k in range(2)],
        bg_half=[0.5 * jnp.stack([l0_gate_a_b[k], l0_gate_x_b[k]], axis=0) for k in range(2)],
        lam=[_row(l0_lambda[k]) for k in range(2)],
        w_out=l0_w_out.astype(BF16),
        ffn_pre_g=_row(l0_ffn_pre_g), ffn_post_g=_row(l0_ffn_post_g),
        ffn_w_in=l0_ffn_w_in.astype(BF16), ffn_w_out=l0_ffn_w_out.astype(BF16))
    p1 = dict(
        mix_pre_g=_row(l1_mix_pre_g), mix_post_g=_row(l1_mix_post_g), pool_w=l1_pool_w.astype(BF16),
        pool_b=_row(l1_pool_b), pool_scale=_row(l1_pool_scale),
        ffn_pre_g=_row(l1_ffn_pre_g), ffn_post_g=_row(l1_ffn_post_g),
        ffn_w_in=l1_ffn_w_in.astype(BF16), ffn_w_out=l1_ffn_w_out.astype(BF16))

    def run(x, pos, mod_row, h0):
        t_len = x.shape[1]
        x2, state = _layer0(x, pos, mod0, mod_row, h0, p0, min(CHUNK_L0, t_len))
        return _layer1(x2, mod1, mod_row, p1, min(CHUNK_L1, t_len)), state

    y_prompt, new_state = run(x_prompt, None, lambda b: CTX_ROW, jnp.zeros((n_ctx, 2, D_RNN), F32))
    y_sample, _ = run(x_sample, _grid_pos_emb(t_lat), lambda b: b, state_l0_rglru)
    return y_prompt, y_sample, new_state
```

```python
import functools

import jax
import jax.numpy as jnp
from jax import lax
from jax.experimental import pallas as pl
from jax.experimental.pallas import tpu as pltpu

D_MODEL = 1024
D_RNN = D_MODEL
N_LRU_BLOCKS = 4
LRU_BLOCK = D_RNN // N_LRU_BLOCKS
CONV_W = 4
LRU_C = 8.0
POOL_WINDOWS = (2, 4, 8, 16)
POOL_GROUP = D_MODEL // len(POOL_WINDOWS)
D_FF = 2816
N_MOD = 6
EPS = 1e-6
POS_THETA = 10000.0
GRID_W = 64

LANES = 128
SUBLANES = 8
HALO = SUBLANES
N_SLABS = D_RNN // LANES
FF_CHUNK = 256
N_FF_CHUNKS = D_FF // FF_CHUNK
MOD_ROWS = 16
CTX_ROW = 8
MOD_TN = 1536
SQRT_FLOOR = 1e-36
CHUNK_L0 = 256
CHUNK_L1 = 512
FWD_CHUNKS = 2
BWD_CHUNKS = 2

VMEM_LIMIT = 56 * 1024 * 1024
VMEM_LIMIT_BIG = 62 * 1024 * 1024

F32 = jnp.float32
BF16 = jnp.bfloat16


def _dot(a, b):
    return jnp.dot(a, b, preferred_element_type=F32)


def _rms(x):
    return x * lax.rsqrt(jnp.mean(x * x, axis=-1, keepdims=True) + EPS)


def _sigmoid(x):
    return 1.0 / (1.0 + jnp.exp(-x))


def _mod_body(cond_ref, w_ref, b_ref, o_ref):
    c = cond_ref[...]
    s = (c * _sigmoid(c)).astype(BF16)
    o_ref[...] = _dot(s, w_ref[...].astype(BF16)) + b_ref[...]


def _modulation(cond, w, b):
    n = N_MOD * D_MODEL
    out = pl.pallas_call(
        _mod_body,
        grid=(n // MOD_TN,),
        in_specs=[
            pl.BlockSpec((MOD_ROWS, D_MODEL), lambda j: (0, 0)),
            pl.BlockSpec((D_MODEL, MOD_TN), lambda j: (0, j)),
            pl.BlockSpec((1, MOD_TN), lambda j: (0, j)),
        ],
        out_specs=pl.BlockSpec((MOD_ROWS, MOD_TN), lambda j: (0, j)),
        out_shape=jax.ShapeDtypeStruct((MOD_ROWS, n), F32),
        compiler_params=pltpu.CompilerParams(
            dimension_semantics=("parallel",), vmem_limit_bytes=VMEM_LIMIT),
        name="modulation",
    )(cond, w, b.reshape(1, n))
    return out.reshape(MOD_ROWS, N_MOD, D_MODEL)


def _gate_ab(hu, n, wg_ref, bg_ref, lam_ref):
    cols = slice(n * LRU_BLOCK, (n + 1) * LRU_BLOCK)
    o = _dot(hu.astype(BF16), wg_ref[n])
    t_r = jnp.tanh(o[:, :LRU_BLOCK] + bg_ref[0:1, cols])
    t_i = jnp.tanh(o[:, LRU_BLOCK:] + bg_ref[1:2, cols])
    nl = -lam_ref[:, cols]
    softplus = jnp.maximum(nl, 0.0) + jnp.log1p(jnp.exp(-jnp.abs(nl)))
    log_a = (t_r + 1.0) * ((-0.5 * LRU_C) * softplus)
    a = jnp.exp(log_a)
    s = jnp.tanh(log_a) * (-1.0 - a * a)
    root = s * lax.rsqrt(jnp.maximum(s, SQRT_FLOOR))
    b = root * (hu + hu * t_i)
    return a, b


def _scan(a, b, h_in, reverse):
    tc = a.shape[0]
    sub = tc // SUBLANES
    sub_id = lax.broadcasted_iota(jnp.int32, (SUBLANES, LANES), 0)
    steps = range(sub - 1, -1, -1) if reverse else range(sub)
    order = range(SUBLANES - 1, -1, -1) if reverse else range(SUBLANES)
    n_slabs = a.shape[1] // LANES
    hs = [[None] * sub for _ in range(n_slabs)]
    ps = [[None] * sub for _ in range(n_slabs)]
    h, p = [None] * n_slabs, [None] * n_slabs
    for j in steps:
        for k in range(n_slabs):
            av = a[j * SUBLANES:(j + 1) * SUBLANES, k * LANES:(k + 1) * LANES]
            bv = b[j * SUBLANES:(j + 1) * SUBLANES, k * LANES:(k + 1) * LANES]
            h[k] = bv if h[k] is None else av * h[k] + bv
            p[k] = av if p[k] is None else av * p[k]
            hs[k][j], ps[k][j] = h[k], p[k]
    slabs, last = [], []
    for k in range(n_slabs):
        carry = h_in[:, k * LANES:(k + 1) * LANES]
        carry_in = jnp.zeros((SUBLANES, LANES), F32)
        for s in order:
            carry_in = jnp.where(sub_id == s, carry, carry_in)
            carry = p[k][s:s + 1, :] * carry + h[k][s:s + 1, :]
        slabs.append(jnp.concatenate([ps[k][j] * carry_in + hs[k][j] for j in range(sub)], axis=0))
        last.append(carry)
    return jnp.concatenate(slabs, axis=1), jnp.concatenate(last, axis=1)


def _pitch(tc):
    return tc // SUBLANES + SUBLANES


def _permute_in(e, ext_s, tc, n_tiles):
    sub = tc // SUBLANES
    pitch = _pitch(tc)
    row = lax.broadcasted_iota(jnp.int32, (SUBLANES, LANES), 0)
    for k in range(N_SLABS):
        lanes = slice(k * LANES, (k + 1) * LANES)
        ext_s[k, 0:HALO, :] = e[0:HALO, lanes]
        for s in range(SUBLANES):
            base = HALO + s * pitch
            end = HALO + (s + 1) * sub
            ext_s[k, base:base + sub, :] = e[end - sub:end, lanes]
            ext_s[k, base + sub:base + pitch, :] = jnp.where(
                row < SUBLANES // 2, e[end:end + SUBLANES, lanes], e[end - SUBLANES:end, lanes])
    tiles = [jnp.concatenate([ext_s[k, pl.ds(HALO - 2 + q, SUBLANES, stride=pitch), :] for k in range(N_SLABS)],
                             axis=1) for q in range(n_tiles)]
    return jnp.concatenate(tiles, axis=0)


def _permute_out(r, o_s, tc):
    sub = tc // SUBLANES
    pitch = _pitch(tc)
    for k in range(N_SLABS):
        for j in range(sub):
            o_s[k, pl.ds(j, SUBLANES, stride=pitch), :] = r[j * SUBLANES:(j + 1) * SUBLANES, k * LANES:(k + 1) * LANES]
    return jnp.concatenate(
        [jnp.concatenate([o_s[k, s * pitch:s * pitch + sub, :] for s in range(SUBLANES)], axis=0)
         for k in range(N_SLABS)], axis=1)


def _pos_rows(ptab_ref, grid_row, col0, n):
    half = ptab_ref.shape[1]
    by_row = jnp.broadcast_to(ptab_ref[pl.ds(grid_row, 1), :], (n, half))
    return jnp.concatenate([by_row, ptab_ref[col0:col0 + n, :]], axis=1)


def _pos_block(ptab_ref, first_grid_row, n_rows):
    return jnp.concatenate([_pos_rows(ptab_ref, first_grid_row + k, 0, GRID_W) for k in range(n_rows // GRID_W)],
                           axis=0)


def _fwd_body(*refs, nc, tc, has_pos):
    if has_pos:
        (x_ref, xp_ref, xn_ref, ptab_ref, mod_ref, g_ref, w_ref, cw_ref, cb_ref, wg_ref, bg_ref,
         lam_ref, h0_ref, zg_ref, hu_ref, hf_ref, sf_ref, carry_ref, ext_s) = refs
    else:
        (x_ref, xp_ref, xn_ref, mod_ref, g_ref, w_ref, cw_ref, cb_ref, wg_ref, bg_ref,
         lam_ref, h0_ref, zg_ref, hu_ref, hf_ref, sf_ref, carry_ref, ext_s) = refs
    c = pl.program_id(1)
    n_sub = x_ref.shape[1] // tc
    sub = tc // SUBLANES
    n_tiles = sub + CONV_W

    @pl.when(c == 0)
    def _():
        carry_ref[...] = h0_ref[0]

    x, xp, xn = x_ref[0], xp_ref[0], xn_ref[0]
    if has_pos:
        rows_per_block = x.shape[0] // GRID_W
        n_grid_rows = nc * rows_per_block
        r0 = c * rows_per_block
        x = x + _pos_block(ptab_ref, r0, x.shape[0])
        xp = xp + _pos_rows(ptab_ref, jnp.maximum(r0 - 1, 0), GRID_W - HALO, HALO)
        xn = xn + _pos_rows(ptab_ref, jnp.minimum(r0 + rows_per_block, n_grid_rows - 1), 0, HALO)
    gs = g_ref[...] * (1.0 + mod_ref[0, 1:2, :])
    sh = mod_ref[0, 0:1, :]
    h_all = jnp.concatenate([_rms(xp) * gs + sh, _rms(x) * gs + sh, _rms(xn) * gs + sh], axis=0)
    row = lax.broadcasted_iota(jnp.int32, (2 * SUBLANES, D_RNN), 0) & (SUBLANES - 1)
    carry = carry_ref[...]
    for q in range(n_sub):
        rows = slice(q * tc, (q + 1) * tc)
        e = h_all[q * tc:(q + 1) * tc + 2 * HALO]
        hp = _permute_in(e, ext_s, tc, n_tiles).astype(BF16)
        zg_ref[0, rows, :] = _dot(hp[2 * SUBLANES:2 * SUBLANES + tc], w_ref[:, :D_RNN])
        rec = _dot(hp, w_ref[:, D_RNN:])
        if q == 0:
            head = jnp.where(row < jnp.where(c == 0, 1, 0), 0.0, rec[0:2 * SUBLANES])
            rec = jnp.concatenate([head, rec[2 * SUBLANES:]], axis=0)
        if q == n_sub - 1:
            tail = jnp.where(row > jnp.where(c == nc - 1, SUBLANES - 2, SUBLANES - 1), 0.0,
                             rec[(sub + 2) * SUBLANES:])
            rec = jnp.concatenate([rec[:(sub + 2) * SUBLANES], tail], axis=0)
        hu = rec[0:tc] * cw_ref[0:1, :]
        for k in range(1, CONV_W):
            hu = hu + rec[k * SUBLANES:k * SUBLANES + tc] * cw_ref[k:k + 1, :]
        hu = hu + cb_ref[...]
        hu_ref[0, rows, :] = hu
        ab = [_gate_ab(hu[:, n * LRU_BLOCK:(n + 1) * LRU_BLOCK], n, wg_ref, bg_ref, lam_ref)
              for n in range(N_LRU_BLOCKS)]
        hf, carry = _scan(jnp.concatenate([a for a, _ in ab], axis=1),
                          jnp.concatenate([b for _, b in ab], axis=1), carry, False)
        hf_ref[0, rows, :] = hf
    carry_ref[...] = carry
    sf_ref[0] = carry


def _ffn_head(x, mod_ref, pre_ref):
    sh = mod_ref[0, 3:4, :]
    sc = mod_ref[0, 4:5, :]
    return (_rms(x) * (pre_ref[...] * (1.0 + sc)) + sh).astype(BF16)


def _ffn_steps(h, wi_ref, wo_ref, act_ref, out):
    def chunk(n):
        def f():
            g = _dot(h, wi_ref[:, n * FF_CHUNK:(n + 1) * FF_CHUNK])
            v = _dot(h, wi_ref[:, D_FF + n * FF_CHUNK:D_FF + (n + 1) * FF_CHUNK])
            act_ref[:, n * FF_CHUNK:(n + 1) * FF_CHUNK] = (g * _sigmoid(g) * v).astype(BF16)
        return f

    def down():
        out.append(_dot(act_ref[...], wo_ref[...]))

    return [chunk(n) for n in range(N_FF_CHUNKS)] + [down]


def _ffn_tail(x, y, mod_ref, post_ref):
    return x + _rms(y) * (post_ref[...] * mod_ref[0, 5:6, :])


def _trace_interleaved(major, minor):
    done = 0
    for k, step in enumerate(major):
        step()
        while done < (k + 1) * len(minor) // len(major):
            minor[done]()
            done += 1


def _bwd_ffn_body(*refs, nc, tc, n_chunks, has_pos):
    if has_pos:
        (hu_ref, zg_ref, hf_ref, x_ref, ptab_ref, mod_ref, wg_ref, bg_ref, lam_ref, h0_ref, wo_ref, pg_ref,
         fmod_ref, pre_ref, post_ref, wi_ref, wo2_ref, x2_ref, sb_ref,
         carry_ref, o_s, x1_s, act_ref, mixed_ref) = refs
    else:
        (hu_ref, zg_ref, hf_ref, x_ref, mod_ref, wg_ref, bg_ref, lam_ref, h0_ref, wo_ref, pg_ref,
         fmod_ref, pre_ref, post_ref, wi_ref, wo2_ref, x2_ref, sb_ref,
         carry_ref, o_s, x1_s, act_ref, mixed_ref) = refs
    i = pl.program_id(0)
    c = jnp.minimum(i, n_chunks - 1) % nc
    n_sub = x_ref.shape[1] // tc

    @pl.when(i == 0)
    def _():
        x1_s[...] = jnp.zeros(x1_s.shape, F32)

    @pl.when(c == 0)
    def _():
        carry_ref[...] = h0_ref[0]

    st = {}
    lasts = [None] * N_LRU_BLOCKS

    def block(n, q):
        def f():
            rows = slice(q * tc, (q + 1) * tc)
            cols = slice(n * LRU_BLOCK, (n + 1) * LRU_BLOCK)
            a, b = _gate_ab(hu_ref[0, rows, cols], n, wg_ref, bg_ref, lam_ref)
            h_in = carry_ref[:, cols] if lasts[n] is None else lasts[n]
            hb, lasts[n] = _scan(a, b, h_in, True)
            mixed_ref[rows, cols] = ((hf_ref[0, rows, cols] + hb) * jax.nn.gelu(zg_ref[0, rows, cols])).astype(BF16)
        return f

    def out_proj():
        st['y'] = _dot(mixed_ref[...], wo_ref[...])

    def residual():
        x = x_ref[0]
        if has_pos:
            x = x + _pos_block(ptab_ref, (nc - 1 - c) * (x.shape[0] // GRID_W), x.shape[0])
        r = _rms(st['y']) * (pg_ref[...] * mod_ref[0, 2:3, :])
        x1_s[i % 2] = x + jnp.concatenate(
            [_permute_out(r[q * tc:(q + 1) * tc], o_s.at[q], tc) for q in range(n_sub)], axis=0)

    mixer = ([block(n, q) for q in range(n_sub - 1, -1, -1) for n in range(N_LRU_BLOCKS)]
             + [out_proj, residual])

    xf = x1_s[(i + 1) % 2]
    yf = []
    _trace_interleaved(_ffn_steps(_ffn_head(xf, fmod_ref, pre_ref), wi_ref, wo2_ref, act_ref, yf), mixer)
    x2_ref[0] = _ffn_tail(xf, yf[0], fmod_ref, post_ref)
    h_last = jnp.concatenate(lasts, axis=1)

    @pl.when(i < n_chunks)
    def _():
        carry_ref[...] = h_last
        sb_ref[0] = h_last


def _pool_group(ext, gi, c, pw_ref, t_len):
    n = ext.shape[0]
    tc = n - 2 * HALO
    win = POOL_WINDOWS[gi]
    e = ext[:, gi * POOL_GROUP:(gi + 1) * POOL_GROUP]
    w = e + pltpu.roll(e, 1, 0)
    half = 1
    while 2 * half < win:
        w = pltpu.roll(w, half, 0) + pltpu.roll(w, n - half, 0)
        half *= 2
    t = c * tc + lax.broadcasted_iota(jnp.int32, (tc, POOL_GROUP), 0)
    cnt = (jnp.minimum(t + win // 2, t_len) - jnp.maximum(t - win // 2, 0)).astype(F32)
    pooled = w[HALO:n - HALO] / cnt - e[HALO:n - HALO]
    return _dot(pooled.astype(BF16), pw_ref[gi])


def _pool_ffn_body(x_ref, xp_ref, xn_ref, mod_ref, mpre_ref, mpost_ref, pw_ref, pb_ref, ps_ref,
                   fmod_ref, pre_ref, post_ref, wi_ref, wo_ref, o_ref, x3_s, act_ref, *, nc, t_len, n_chunks):
    i = pl.program_id(0)
    c = jnp.minimum(i, n_chunks - 1) % nc

    @pl.when(i == 0)
    def _():
        x3_s[...] = jnp.zeros(x3_s.shape, F32)

    st = {}
    ys = [None] * len(POOL_WINDOWS)

    def pre_norm():
        gs = mpre_ref[...] * (1.0 + mod_ref[0, 1:2, :])
        sh = mod_ref[0, 0:1, :]
        st['ext'] = jnp.concatenate([jnp.where(c > 0, _rms(xp_ref[0]) * gs + sh, 0.0),
                                     _rms(x_ref[0]) * gs + sh,
                                     jnp.where(c < nc - 1, _rms(xn_ref[0]) * gs + sh, 0.0)], axis=0)

    def group(gi):
        def f():
            ys[gi] = _pool_group(st['ext'], gi, c, pw_ref, t_len)
        return f

    def residual():
        y = (jnp.concatenate(ys, axis=1) + pb_ref[...]) * ps_ref[...]
        x3_s[i % 2] = x_ref[0] + _rms(y) * (mpost_ref[...] * mod_ref[0, 2:3, :])

    mixer = [pre_norm] + [group(gi) for gi in range(len(POOL_WINDOWS))] + [residual]

    xf = x3_s[(i + 1) % 2]
    yf = []
    _trace_interleaved(_ffn_steps(_ffn_head(xf, fmod_ref, pre_ref), wi_ref, wo_ref, act_ref, yf), mixer)
    o_ref[0] = _ffn_tail(xf, yf[0], fmod_ref, post_ref)


def _const_spec(shape, single=False):
    nd = len(shape)
    if single:
        return pl.BlockSpec(shape, lambda *_: (0,) * nd, pipeline_mode=pl.Buffered(1))
    return pl.BlockSpec(shape, lambda *_: (0,) * nd)


def _ffn_specs(d):
    return [_const_spec((1, d)), _const_spec((1, d)),
            _const_spec((d, 2 * D_FF), single=True), _const_spec((D_FF, d), single=True)]


def _layer0(x, pos, mod, mod_row, h0, p, tc):
    bsz, t_len, d = x.shape
    nc = t_len // tc
    n_chunks = bsz * nc
    n_hblk = t_len // HALO
    has_pos = pos is not None
    act = jax.ShapeDtypeStruct((bsz, t_len, d), F32)
    st = jax.ShapeDtypeStruct((bsz, 1, d), F32)

    tcb = tc * min(FWD_CHUNKS, nc)
    hb8 = tcb // HALO
    prev_blk = lambda c: jnp.maximum(c * hb8 - 1, 0)
    next_blk = lambda c: jnp.minimum((c + 1) * hb8, n_hblk - 1)
    row_spec = pl.BlockSpec((1, d), lambda b, c: (0, 0))
    chunk = pl.BlockSpec((1, tcb, d), lambda b, c: (b, c, 0))
    state_spec = pl.BlockSpec((1, 1, d), lambda b, c: (b, 0, 0))
    pos_specs = [_const_spec(pos.shape)] if has_pos else []
    in_specs = ([chunk,
                 pl.BlockSpec((1, HALO, d), lambda b, c: (b, prev_blk(c), 0)),
                 pl.BlockSpec((1, HALO, d), lambda b, c: (b, next_blk(c), 0))]
                + pos_specs
                + [pl.BlockSpec((1, N_MOD, d), lambda b, c: (mod_row(b), 0, 0)), row_spec,
                   _const_spec((d, 2 * D_RNN)), _const_spec((CONV_W, d)), row_spec,
                   _const_spec((N_LRU_BLOCKS, LRU_BLOCK, 2 * LRU_BLOCK)), _const_spec((2, d)), row_spec, state_spec])
    args = ([x, x, x] + ([pos] if has_pos else [])
            + [mod, p['mix_pre_g'], p['w_in'], p['conv_w_half'], p['conv_b_half'], p['wg'][0], p['bg_half'][0],
               p['lam'][0], h0[:, 0:1]])
    zg, hu, hf, sf = pl.pallas_call(
        functools.partial(_fwd_body, nc=t_len // tcb, tc=tc, has_pos=has_pos),
        grid=(bsz, t_len // tcb), in_specs=in_specs, out_specs=[chunk, chunk, chunk, state_spec],
        out_shape=[act, act, act, st],
        scratch_shapes=[pltpu.VMEM((1, d), F32),
                        pltpu.VMEM((N_SLABS, HALO + SUBLANES * _pitch(tc), LANES), F32)],
        compiler_params=pltpu.CompilerParams(
            dimension_semantics=("parallel", "arbitrary"), vmem_limit_bytes=VMEM_LIMIT),
        name="l0_fwd",
    )(*args)

    n_sub = min(BWD_CHUNKS, nc)
    tcb = tc * n_sub
    nb = t_len // tcb
    n_blocks = bsz * nb

    def mix_at(i):
        im = jnp.minimum(i, n_blocks - 1)
        return im // nb, nb - 1 - im % nb

    def ffn_at(i):
        return mix_at(jnp.maximum(i - 1, 0))

    mchunk = pl.BlockSpec((1, tcb, d), lambda i: (*mix_at(i), 0))
    in_specs = ([mchunk, mchunk, mchunk, mchunk]
                + pos_specs
                + [pl.BlockSpec((1, N_MOD, d), lambda i: (mod_row(mix_at(i)[0]), 0, 0)),
                   _const_spec((N_LRU_BLOCKS, LRU_BLOCK, 2 * LRU_BLOCK)), _const_spec((2, d)), _const_spec((1, d)),
                   pl.BlockSpec((1, 1, d), lambda i: (mix_at(i)[0], 0, 0)),
                   _const_spec((D_RNN, d)), _const_spec((1, d)),
                   pl.BlockSpec((1, N_MOD, d), lambda i: (mod_row(ffn_at(i)[0]), 0, 0))]
                + _ffn_specs(d))
    args = ([hu, zg, hf, x] + ([pos] if has_pos else [])
            + [mod, p['wg'][1], p['bg_half'][1], p['lam'][1], h0[:, 1:2], p['w_out'], p['mix_post_g'],
               mod, p['ffn_pre_g'], p['ffn_post_g'], p['ffn_w_in'], p['ffn_w_out']])
    x2, sb = pl.pallas_call(
        functools.partial(_bwd_ffn_body, nc=nb, tc=tc, n_chunks=n_blocks, has_pos=has_pos),
        grid=(n_blocks + 1,), in_specs=in_specs,
        out_specs=[pl.BlockSpec((1, tcb, d), lambda i: (*ffn_at(i), 0)),
                   pl.BlockSpec((1, 1, d), lambda i: (mix_at(i)[0], 0, 0))],
        out_shape=[act, st],
        scratch_shapes=[pltpu.VMEM((1, d), F32),
                        pltpu.VMEM((n_sub, N_SLABS, SUBLANES * _pitch(tc), LANES), F32),
                        pltpu.VMEM((2, tcb, d), F32), pltpu.VMEM((tcb, D_FF), BF16), pltpu.VMEM((tcb, D_RNN), BF16)],
        compiler_params=pltpu.CompilerParams(dimension_semantics=("arbitrary",), vmem_limit_bytes=VMEM_LIMIT_BIG),
        name="l0_bwd_ffn",
    )(*args)
    return x2, jnp.concatenate([sf, sb], axis=1)


def _layer1(x, mod, mod_row, p, tc):
    bsz, t_len, d = x.shape
    nc = t_len // tc
    n_chunks = bsz * nc
    hb8 = tc // HALO
    n_hblk = t_len // HALO

    def mix_at(i):
        im = jnp.minimum(i, n_chunks - 1)
        return im // nc, im % nc

    def ffn_at(i):
        return mix_at(jnp.maximum(i - 1, 0))

    def halo_spec(blk_of):
        return pl.BlockSpec((1, HALO, d), lambda i: (mix_at(i)[0], blk_of(mix_at(i)[1]), 0))

    return pl.pallas_call(
        functools.partial(_pool_ffn_body, nc=nc, t_len=t_len, n_chunks=n_chunks),
        grid=(n_chunks + 1,),
        in_specs=[pl.BlockSpec((1, tc, d), lambda i: (*mix_at(i), 0)),
                  halo_spec(lambda c: jnp.maximum(c * hb8 - 1, 0)),
                  halo_spec(lambda c: jnp.minimum((c + 1) * hb8, n_hblk - 1)),
                  pl.BlockSpec((1, N_MOD, d), lambda i: (mod_row(mix_at(i)[0]), 0, 0)),
                  _const_spec((1, d)), _const_spec((1, d)),
                  _const_spec((len(POOL_WINDOWS), POOL_GROUP, POOL_GROUP)), _const_spec((1, d)), _const_spec((1, d)),
                  pl.BlockSpec((1, N_MOD, d), lambda i: (mod_row(ffn_at(i)[0]), 0, 0))] + _ffn_specs(d),
        out_specs=pl.BlockSpec((1, tc, d), lambda i: (*ffn_at(i), 0)),
        out_shape=jax.ShapeDtypeStruct((bsz, t_len, d), F32),
        scratch_shapes=[pltpu.VMEM((2, tc, d), F32), pltpu.VMEM((tc, D_FF), BF16)],
        compiler_params=pltpu.CompilerParams(dimension_semantics=("arbitrary",), vmem_limit_bytes=VMEM_LIMIT),
        name="l1_ffn",
    )(x, x, x, mod, p['mix_pre_g'], p['mix_post_g'], p['pool_w'], p['pool_b'], p['pool_scale'],
      mod, p['ffn_pre_g'], p['ffn_post_g'], p['ffn_w_in'], p['ffn_w_out'])


def _grid_pos_table(t_len):
    rows = t_len // GRID_W
    quarter = D_MODEL // 4
    omega = 1.0 / (POS_THETA ** (jnp.arange(quarter, dtype=F32) / quarter))
    ang = jnp.arange(max(rows, GRID_W), dtype=F32)[:, None] * omega[None, :]
    return jnp.concatenate([jnp.sin(ang), jnp.cos(ang)], axis=-1)


def _row(v):
    return v.reshape(1, -1)


def kernel(x_prompt, x_sample, state_l0_rglru, c, c_ctx, l0_mod_w, l0_mod_b, l0_mix_pre_g, l0_mix_post_g, l0_w_in, l0_conv_w, l0_conv_b, l0_gate_a_w, l0_gate_a_b, l0_gate_x_w, l0_gate_x_b, l0_lambda, l0_w_out, l0_ffn_pre_g, l0_ffn_post_g, l0_ffn_w_in, l0_ffn_w_out, l1_mod_w, l1_mod_b, l1_mix_pre_g, l1_mix_post_g, l1_pool_w, l1_pool_b, l1_pool_scale, l1_ffn_pre_g, l1_ffn_post_g, l1_ffn_w_in, l1_ffn_w_out):
    n_ctx, t_ctx, d = x_prompt.shape
    n_lat, t_lat, _ = x_sample.shape

    cond = jnp.concatenate(
        [c, c_ctx[None, :], jnp.zeros((MOD_ROWS - n_lat - 1, d), F32)], axis=0)
    mod0 = _modulation(cond, l0_mod_w, l0_mod_b)
    mod1 = _modulation(cond, l1_mod_w, l1_mod_b)

    p0 = dict(
        mix_pre_g=_row(l0_mix_pre_g), mix_post_g=_row(l0_mix_post_g), w_in=l0_w_in.astype(BF16),
        conv_w_half=0.5 * l0_conv_w, conv_b_half=_row(0.5 * l0_conv_b),
        wg=[jnp.concatenate([l0_gate_a_w[k], l0_gate_x_w[k]], axis=-1).astype(BF16) for k in range(2)],
        bg_half=[0.5 * jnp.stack([l0_gate_a_b[k], l0_gate_x_b[k]], axis=0) for k in range(2)],
        lam=[_row(l0_lambda[k]) for k in range(2)],
        w_out=l0_w_out.astype(BF16),
        ffn_pre_g=_row(l0_ffn_pre_g), ffn_post_g=_row(l0_ffn_post_g),
        ffn_w_in=l0_ffn_w_in.astype(BF16), ffn_w_out=l0_ffn_w_out.astype(BF16))
    p1 = dict(
        mix_pre_g=_row(l1_mix_pre_g), mix_post_g=_row(l1_mix_post_g), pool_w=l1_pool_w.astype(BF16),
        pool_b=_row(l1_pool_b), pool_scale=_row(l1_pool_scale),
        ffn_pre_g=_row(l1_ffn_pre_g), ffn_post_g=_row(l1_ffn_post_g),
        ffn_w_in=l1_ffn_w_in.astype(BF16), ffn_w_out=l1_ffn_w_out.astype(BF16))

    def run(x, pos, mod_row, h0):
        t_len = x.shape[1]
        x2, state = _layer0(x, pos, mod0, mod_row, h0, p0, min(CHUNK_L0, t_len))
        return _layer1(x2, mod1, mod_row, p1, min(CHUNK_L1, t_len)), state

    y_prompt, new_state = run(x_prompt, None, lambda b: CTX_ROW, jnp.zeros((n_ctx, 2, D_RNN), F32))
    y_sample, _ = run(x_sample, _grid_pos_table(t_lat), lambda b: b, state_l0_rglru)
    return y_prompt, y_sample, new_state
```

```python
import functools

import jax
import jax.numpy as jnp
from jax import lax
from jax.experimental import pallas as pl
from jax.experimental.pallas import tpu as pltpu

D_MODEL = 1024
D_RNN = D_MODEL
N_LRU_BLOCKS = 4
LRU_BLOCK = D_RNN // N_LRU_BLOCKS
CONV_W = 4
LRU_C = 8.0
POOL_WINDOWS = (2, 4, 8, 16)
POOL_GROUP = D_MODEL // len(POOL_WINDOWS)
D_FF = 2816
N_MOD = 6
EPS = 1e-6
POS_THETA = 10000.0
GRID_W = 64

LANES = 128
SUBLANES = 8
HALO = SUBLANES
N_SLABS = D_RNN // LANES
FF_CHUNK = 256
N_FF_CHUNKS = D_FF // FF_CHUNK
MOD_ROWS = 16
CTX_ROW = 8
MOD_TN = 1536
SQRT_FLOOR = 1e-36
CHUNK_L0 = 256
CHUNK_L1 = 512
FWD_CHUNKS = 2
BWD_CHUNKS = 2

VMEM_LIMIT = 56 * 1024 * 1024
VMEM_LIMIT_BIG = 62 * 1024 * 1024

F32 = jnp.float32
BF16 = jnp.bfloat16


def _dot(a, b):
    return jnp.dot(a, b, preferred_element_type=F32)


def _rms(x):
    return x * lax.rsqrt(jnp.mean(x * x, axis=-1, keepdims=True) + EPS)


def _sigmoid(x):
    return 1.0 / (1.0 + jnp.exp(-x))


def _mod_body(cond_ref, w_ref, b_ref, o_ref):
    c = cond_ref[...]
    s = (c * _sigmoid(c)).astype(BF16)
    o_ref[...] = _dot(s, w_ref[...].astype(BF16)) + b_ref[...]


def _modulation(cond, w, b):
    n = N_MOD * D_MODEL
    out = pl.pallas_call(
        _mod_body,
        grid=(n // MOD_TN,),
        in_specs=[
            pl.BlockSpec((MOD_ROWS, D_MODEL), lambda j: (0, 0)),
            pl.BlockSpec((D_MODEL, MOD_TN), lambda j: (0, j)),
            pl.BlockSpec((1, MOD_TN), lambda j: (0, j)),
        ],
        out_specs=pl.BlockSpec((MOD_ROWS, MOD_TN), lambda j: (0, j)),
        out_shape=jax.ShapeDtypeStruct((MOD_ROWS, n), F32),
        compiler_params=pltpu.CompilerParams(
            dimension_semantics=("parallel",), vmem_limit_bytes=VMEM_LIMIT),
        name="modulation",
    )(cond, w, b.reshape(1, n))
    return out.reshape(MOD_ROWS, N_MOD, D_MODEL)


def _gate_ab(hu, n, wg_ref, bg_ref, lam_ref):
    cols = slice(n * LRU_BLOCK, (n + 1) * LRU_BLOCK)
    o = _dot(hu.astype(BF16), wg_ref[n])
    t_r = jnp.tanh(o[:, :LRU_BLOCK] + bg_ref[0:1, cols])
    t_i = jnp.tanh(o[:, LRU_BLOCK:] + bg_ref[1:2, cols])
    nl = -lam_ref[:, cols]
    softplus = jnp.maximum(nl, 0.0) + jnp.log1p(jnp.exp(-jnp.abs(nl)))
    log_a = (t_r + 1.0) * ((-0.5 * LRU_C) * softplus)
    a = jnp.exp(log_a)
    s = jnp.tanh(log_a) * (-1.0 - a * a)
    root = s * lax.rsqrt(jnp.maximum(s, SQRT_FLOOR))
    b = root * (hu + hu * t_i)
    return a, b


def _scan(a, b, h_in, reverse):
    tc = a.shape[0]
    sub = tc // SUBLANES
    sub_id = lax.broadcasted_iota(jnp.int32, (SUBLANES, LANES), 0)
    steps = range(sub - 1, -1, -1) if reverse else range(sub)
    order = range(SUBLANES - 1, -1, -1) if reverse else range(SUBLANES)
    n_slabs = a.shape[1] // LANES
    hs = [[None] * sub for _ in range(n_slabs)]
    ps = [[None] * sub for _ in range(n_slabs)]
    h, p = [None] * n_slabs, [None] * n_slabs
    for j in steps:
        for k in range(n_slabs):
            av = a[j * SUBLANES:(j + 1) * SUBLANES, k * LANES:(k + 1) * LANES]
            bv = b[j * SUBLANES:(j + 1) * SUBLANES, k * LANES:(k + 1) * LANES]
            h[k] = bv if h[k] is None else av * h[k] + bv
            p[k] = av if p[k] is None else av * p[k]
            hs[k][j], ps[k][j] = h[k], p[k]
    slabs, last = [], []
    for k in range(n_slabs):
        carry = h_in[:, k * LANES:(k + 1) * LANES]
        carry_in = jnp.zeros((SUBLANES, LANES), F32)
        for s in order:
            carry_in = jnp.where(sub_id == s, carry, carry_in)
            carry = p[k][s:s + 1, :] * carry + h[k][s:s + 1, :]
        slabs.append(jnp.concatenate([ps[k][j] * carry_in + hs[k][j] for j in range(sub)], axis=0))
        last.append(carry)
    return jnp.concatenate(slabs, axis=1), jnp.concatenate(last, axis=1)


def _pitch(tc):
    return tc // SUBLANES + SUBLANES


def _permute_in(e, ext_s, tc, n_tiles):
    sub = tc // SUBLANES
    pitch = _pitch(tc)
    row = lax.broadcasted_iota(jnp.int32, (SUBLANES, LANES), 0)
    for k in range(N_SLABS):
        lanes = slice(k * LANES, (k + 1) * LANES)
        ext_s[k, 0:HALO, :] = e[0:HALO, lanes]
        for s in range(SUBLANES):
            base = HALO + s * pitch
            end = HALO + (s + 1) * sub
            ext_s[k, base:base + sub, :] = e[end - sub:end, lanes]
            ext_s[k, base + sub:base + pitch, :] = jnp.where(
                row < SUBLANES // 2, e[end:end + SUBLANES, lanes], e[end - SUBLANES:end, lanes])
    tiles = [jnp.concatenate([ext_s[k, pl.ds(HALO - 2 + q, SUBLANES, stride=pitch), :] for k in range(N_SLABS)],
                             axis=1) for q in range(n_tiles)]
    return jnp.concatenate(tiles, axis=0)


def _permute_out(r, o_s, tc):
    sub = tc // SUBLANES
    pitch = _pitch(tc)
    for k in range(N_SLABS):
        for j in range(sub):
            o_s[k, pl.ds(j, SUBLANES, stride=pitch), :] = r[j * SUBLANES:(j + 1) * SUBLANES, k * LANES:(k + 1) * LANES]
    return jnp.concatenate(
        [jnp.concatenate([o_s[k, s * pitch:s * pitch + sub, :] for s in range(SUBLANES)], axis=0)
         for k in range(N_SLABS)], axis=1)


def _pos_rows(ptab_ref, grid_row, col0, n):
    half = ptab_ref.shape[1]
    by_row = jnp.broadcast_to(ptab_ref[pl.ds(grid_row, 1), :], (n, half))
    return jnp.concatenate([by_row, ptab_ref[col0:col0 + n, :]], axis=1)


def _pos_block(ptab_ref, first_grid_row, n_rows):
    return jnp.concatenate([_pos_rows(ptab_ref, first_grid_row + k, 0, GRID_W) for k in range(n_rows // GRID_W)],
                           axis=0)


def _fwd_body(*refs, nc, tc, has_pos):
    if has_pos:
        (x_ref, xp_ref, xn_ref, ptab_ref, mod_ref, g_ref, w_ref, cw_ref, cb_ref, wg_ref, bg_ref,
         lam_ref, h0_ref, zg_ref, hu_ref, hf_ref, sf_ref, carry_ref, ext_s) = refs
    else:
        (x_ref, xp_ref, xn_ref, mod_ref, g_ref, w_ref, cw_ref, cb_ref, wg_ref, bg_ref,
         lam_ref, h0_ref, zg_ref, hu_ref, hf_ref, sf_ref, carry_ref, ext_s) = refs
    carry_ref = carry_ref.at[0:1]
    c = pl.program_id(1)
    n_sub = x_ref.shape[1] // tc
    sub = tc // SUBLANES
    n_tiles = sub + CONV_W

    @pl.when(c == 0)
    def _():
        carry_ref[...] = h0_ref[0]

    x, xp, xn = x_ref[0], xp_ref[0], xn_ref[0]
    if has_pos:
        rows_per_block = x.shape[0] // GRID_W
        n_grid_rows = nc * rows_per_block
        r0 = c * rows_per_block
        x = x + _pos_block(ptab_ref, r0, x.shape[0])
        xp = xp + _pos_rows(ptab_ref, jnp.maximum(r0 - 1, 0), GRID_W - HALO, HALO)
        xn = xn + _pos_rows(ptab_ref, jnp.minimum(r0 + rows_per_block, n_grid_rows - 1), 0, HALO)
    gs = g_ref[...] * (1.0 + mod_ref[0, 1:2, :])
    sh = mod_ref[0, 0:1, :]
    h_all = jnp.concatenate([_rms(xp) * gs + sh, _rms(x) * gs + sh, _rms(xn) * gs + sh], axis=0)
    row = lax.broadcasted_iota(jnp.int32, (2 * SUBLANES, D_RNN), 0) & (SUBLANES - 1)
    carry = carry_ref[...]
    for q in range(n_sub):
        rows = slice(q * tc, (q + 1) * tc)
        e = h_all[q * tc:(q + 1) * tc + 2 * HALO]
        hp = _permute_in(e, ext_s, tc, n_tiles).astype(BF16)
        zg_ref[0, rows, :] = _dot(hp[2 * SUBLANES:2 * SUBLANES + tc], w_ref[:, :D_RNN])
        rec = _dot(hp, w_ref[:, D_RNN:])
        if q == 0:
            head = jnp.where(row < jnp.where(c == 0, 1, 0), 0.0, rec[0:2 * SUBLANES])
            rec = jnp.concatenate([head, rec[2 * SUBLANES:]], axis=0)
        if q == n_sub - 1:
            tail = jnp.where(row > jnp.where(c == nc - 1, SUBLANES - 2, SUBLANES - 1), 0.0,
                             rec[(sub + 2) * SUBLANES:])
            rec = jnp.concatenate([rec[:(sub + 2) * SUBLANES], tail], axis=0)
        hu = rec[0:tc] * cw_ref[0:1, :]
        for k in range(1, CONV_W):
            hu = hu + rec[k * SUBLANES:k * SUBLANES + tc] * cw_ref[k:k + 1, :]
        hu = hu + cb_ref[...]
        hu_ref[0, rows, :] = hu
        ab = [_gate_ab(hu[:, n * LRU_BLOCK:(n + 1) * LRU_BLOCK], n, wg_ref, bg_ref, lam_ref)
              for n in range(N_LRU_BLOCKS)]
        hf, carry = _scan(jnp.concatenate([a for a, _ in ab], axis=1),
                          jnp.concatenate([b for _, b in ab], axis=1), carry, False)
        hf_ref[0, rows, :] = hf
    carry_ref[...] = carry
    sf_ref[0] = carry


def _ffn_head(x, mod_ref, pre_ref):
    sh = mod_ref[0, 3:4, :]
    sc = mod_ref[0, 4:5, :]
    return (_rms(x) * (pre_ref[...] * (1.0 + sc)) + sh).astype(BF16)


def _ffn_steps(h, wi_ref, wo_ref, act_ref, out):
    def chunk(n):
        def f():
            g = _dot(h, wi_ref[:, n * FF_CHUNK:(n + 1) * FF_CHUNK])
            v = _dot(h, wi_ref[:, D_FF + n * FF_CHUNK:D_FF + (n + 1) * FF_CHUNK])
            act_ref[:, n * FF_CHUNK:(n + 1) * FF_CHUNK] = (g * _sigmoid(g) * v).astype(BF16)
        return f

    def down():
        out.append(_dot(act_ref[...], wo_ref[...]))

    return [chunk(n) for n in range(N_FF_CHUNKS)] + [down]


def _ffn_tail(x, y, mod_ref, post_ref):
    return x + _rms(y) * (post_ref[...] * mod_ref[0, 5:6, :])


def _trace_interleaved(major, minor):
    done = 0
    for k, step in enumerate(major):
        step()
        while done < (k + 1) * len(minor) // len(major):
            minor[done]()
            done += 1


def _bwd_ffn_body(*refs, nc, tc, n_chunks, has_pos):
    if has_pos:
        (hu_ref, zg_ref, hf_ref, x_ref, ptab_ref, mod_ref, wg_ref, bg_ref, lam_ref, h0_ref, wo_ref, pg_ref,
         fmod_ref, pre_ref, post_ref, wi_ref, wo2_ref, x2_ref, sb_ref,
         carry_ref, o_s, x1_s, act_ref, mixed_ref) = refs
    else:
        (hu_ref, zg_ref, hf_ref, x_ref, mod_ref, wg_ref, bg_ref, lam_ref, h0_ref, wo_ref, pg_ref,
         fmod_ref, pre_ref, post_ref, wi_ref, wo2_ref, x2_ref, sb_ref,
         carry_ref, o_s, x1_s, act_ref, mixed_ref) = refs
    carry_ref = carry_ref.at[0:1]
    i = pl.program_id(0)
    c = jnp.minimum(i, n_chunks - 1) % nc
    n_sub = x_ref.shape[1] // tc

    @pl.when(i == 0)
    def _():
        x1_s[...] = jnp.zeros(x1_s.shape, F32)

    @pl.when(c == 0)
    def _():
        carry_ref[...] = h0_ref[0]

    st = {}
    lasts = [None] * N_LRU_BLOCKS

    def block(n, q):
        def f():
            rows = slice(q * tc, (q + 1) * tc)
            cols = slice(n * LRU_BLOCK, (n + 1) * LRU_BLOCK)
            a, b = _gate_ab(hu_ref[0, rows, cols], n, wg_ref, bg_ref, lam_ref)
            h_in = carry_ref[:, cols] if lasts[n] is None else lasts[n]
            hb, lasts[n] = _scan(a, b, h_in, True)
            mixed_ref[rows, cols] = ((hf_ref[0, rows, cols] + hb) * jax.nn.gelu(zg_ref[0, rows, cols])).astype(BF16)
        return f

    def out_proj():
        st['y'] = _dot(mixed_ref[...], wo_ref[...])

    def residual():
        x = x_ref[0]
        if has_pos:
            x = x + _pos_block(ptab_ref, (nc - 1 - c) * (x.shape[0] // GRID_W), x.shape[0])
        r = _rms(st['y']) * (pg_ref[...] * mod_ref[0, 2:3, :])
        x1_s[i % 2] = x + jnp.concatenate(
            [_permute_out(r[q * tc:(q + 1) * tc], o_s.at[q], tc) for q in range(n_sub)], axis=0)

    mixer = ([block(n, q) for q in range(n_sub - 1, -1, -1) for n in range(N_LRU_BLOCKS)]
             + [out_proj, residual])

    xf = x1_s[(i + 1) % 2]
    yf = []
    _trace_interleaved(_ffn_steps(_ffn_head(xf, fmod_ref, pre_ref), wi_ref, wo2_ref, act_ref, yf), mixer)
    x2_ref[0] = _ffn_tail(xf, yf[0], fmod_ref, post_ref)
    h_last = jnp.concatenate(lasts, axis=1)

    @pl.when(i < n_chunks)
    def _():
        carry_ref[...] = h_last
        sb_ref[0] = h_last


def _pool_group(ext, gi, c, pw_ref, t_len):
    n = ext.shape[0]
    tc = n - 2 * HALO
    win = POOL_WINDOWS[gi]
    e = ext[:, gi * POOL_GROUP:(gi + 1) * POOL_GROUP]
    w = e + pltpu.roll(e, 1, 0)
    half = 1
    while 2 * half < win:
        w = pltpu.roll(w, half, 0) + pltpu.roll(w, n - half, 0)
        half *= 2
    t = c * tc + lax.broadcasted_iota(jnp.int32, (tc, POOL_GROUP), 0)
    cnt = (jnp.minimum(t + win // 2, t_len) - jnp.maximum(t - win // 2, 0)).astype(F32)
    pooled = w[HALO:n - HALO] / cnt - e[HALO:n - HALO]
    return _dot(pooled.astype(BF16), pw_ref[gi])


def _pool_ffn_body(x_ref, xp_ref, xn_ref, mod_ref, mpre_ref, mpost_ref, pw_ref, pb_ref, ps_ref,
                   fmod_ref, pre_ref, post_ref, wi_ref, wo_ref, o_ref, x3_s, act_ref, *, nc, t_len, n_chunks):
    i = pl.program_id(0)
    c = jnp.minimum(i, n_chunks - 1) % nc

    @pl.when(i == 0)
    def _():
        x3_s[...] = jnp.zeros(x3_s.shape, F32)

    st = {}
    ys = [None] * len(POOL_WINDOWS)

    def pre_norm():
        gs = mpre_ref[...] * (1.0 + mod_ref[0, 1:2, :])
        sh = mod_ref[0, 0:1, :]
        st['ext'] = jnp.concatenate([jnp.where(c > 0, _rms(xp_ref[0]) * gs + sh, 0.0),
                                     _rms(x_ref[0]) * gs + sh,
                                     jnp.where(c < nc - 1, _rms(xn_ref[0]) * gs + sh, 0.0)], axis=0)

    def group(gi):
        def f():
            ys[gi] = _pool_group(st['ext'], gi, c, pw_ref, t_len)
        return f

    def residual():
        y = (jnp.concatenate(ys, axis=1) + pb_ref[...]) * ps_ref[...]
        x3_s[i % 2] = x_ref[0] + _rms(y) * (mpost_ref[...] * mod_ref[0, 2:3, :])

    mixer = [pre_norm] + [group(gi) for gi in range(len(POOL_WINDOWS))] + [residual]

    xf = x3_s[(i + 1) % 2]
    yf = []
    _trace_interleaved(_ffn_steps(_ffn_head(xf, fmod_ref, pre_ref), wi_ref, wo_ref, act_ref, yf), mixer)
    o_ref[0] = _ffn_tail(xf, yf[0], fmod_ref, post_ref)


def _const_spec(shape, single=False):
    nd = len(shape)
    if single:
        return pl.BlockSpec(shape, lambda *_: (0,) * nd, pipeline_mode=pl.Buffered(1))
    return pl.BlockSpec(shape, lambda *_: (0,) * nd)


def _ffn_specs(d):
    return [_const_spec((1, d)), _const_spec((1, d)),
            _const_spec((d, 2 * D_FF), single=True), _const_spec((D_FF, d), single=True)]


def _layer0(x, pos, mod, mod_row, h0, p, tc):
    bsz, t_len, d = x.shape
    nc = t_len // tc
    n_chunks = bsz * nc
    n_hblk = t_len // HALO
    has_pos = pos is not None
    act = jax.ShapeDtypeStruct((bsz, t_len, d), F32)
    st = jax.ShapeDtypeStruct((bsz, 1, d), F32)

    tcb = tc * min(FWD_CHUNKS, nc)
    hb8 = tcb // HALO
    prev_blk = lambda c: jnp.maximum(c * hb8 - 1, 0)
    next_blk = lambda c: jnp.minimum((c + 1) * hb8, n_hblk - 1)
    row_spec = pl.BlockSpec((1, d), lambda b, c: (0, 0))
    chunk = pl.BlockSpec((1, tcb, d), lambda b, c: (b, c, 0))
    state_spec = pl.BlockSpec((1, 1, d), lambda b, c: (b, 0, 0))
    pos_specs = [_const_spec(pos.shape)] if has_pos else []
    in_specs = ([chunk,
                 pl.BlockSpec((1, HALO, d), lambda b, c: (b, prev_blk(c), 0)),
                 pl.BlockSpec((1, HALO, d), lambda b, c: (b, next_blk(c), 0))]
                + pos_specs
                + [pl.BlockSpec((1, N_MOD, d), lambda b, c: (mod_row(b), 0, 0)), row_spec,
                   _const_spec((d, 2 * D_RNN)), _const_spec((CONV_W, d)), row_spec,
                   _const_spec((N_LRU_BLOCKS, LRU_BLOCK, 2 * LRU_BLOCK)), _const_spec((2, d)), row_spec, state_spec])
    args = ([x, x, x] + ([pos] if has_pos else [])
            + [mod, p['mix_pre_g'], p['w_in'], p['conv_w_half'], p['conv_b_half'], p['wg'][0], p['bg_half'][0],
               p['lam'][0], h0[:, 0:1]])
    zg, hu, hf, sf = pl.pallas_call(
        functools.partial(_fwd_body, nc=t_len // tcb, tc=tc, has_pos=has_pos),
        grid=(bsz, t_len // tcb), in_specs=in_specs, out_specs=[chunk, chunk, chunk, state_spec],
        out_shape=[act, act, act, st],
        scratch_shapes=[pltpu.VMEM((SUBLANES, d), F32),
                        pltpu.VMEM((N_SLABS, HALO + SUBLANES * _pitch(tc), LANES), F32)],
        compiler_params=pltpu.CompilerParams(
            dimension_semantics=("parallel", "arbitrary"), vmem_limit_bytes=VMEM_LIMIT),
        name="l0_fwd",
    )(*args)

    n_sub = min(BWD_CHUNKS, nc)
    tcb = tc * n_sub
    nb = t_len // tcb
    n_blocks = bsz * nb

    def mix_at(i):
        im = jnp.minimum(i, n_blocks - 1)
        return im // nb, nb - 1 - im % nb

    def ffn_at(i):
        return mix_at(jnp.maximum(i - 1, 0))

    mchunk = pl.BlockSpec((1, tcb, d), lambda i: (*mix_at(i), 0))
    in_specs = ([mchunk, mchunk, mchunk, mchunk]
                + pos_specs
                + [pl.BlockSpec((1, N_MOD, d), lambda i: (mod_row(mix_at(i)[0]), 0, 0)),
                   _const_spec((N_LRU_BLOCKS, LRU_BLOCK, 2 * LRU_BLOCK)), _const_spec((2, d)), _const_spec((1, d)),
                   pl.BlockSpec((1, 1, d), lambda i: (mix_at(i)[0], 0, 0)),
                   _const_spec((D_RNN, d)), _const_spec((1, d)),
                   pl.BlockSpec((1, N_MOD, d), lambda i: (mod_row(ffn_at(i)[0]), 0, 0))]
                + _ffn_specs(d))
    args = ([hu, zg, hf, x] + ([pos] if has_pos else [])
            + [mod, p['wg'][1], p['bg_half'][1], p['lam'][1], h0[:, 1:2], p['w_out'], p['mix_post_g'],
               mod, p['ffn_pre_g'], p['ffn_post_g'], p['ffn_w_in'], p['ffn_w_out']])
    x2, sb = pl.pallas_call(
        functools.partial(_bwd_ffn_body, nc=nb, tc=tc, n_chunks=n_blocks, has_pos=has_pos),
        grid=(n_blocks + 1,), in_specs=in_specs,
        out_specs=[pl.BlockSpec((1, tcb, d), lambda i: (*ffn_at(i), 0)),
                   pl.BlockSpec((1, 1, d), lambda i: (mix_at(i)[0], 0, 0))],
        out_shape=[act, st],
        scratch_shapes=[pltpu.VMEM((SUBLANES, d), F32),
                        pltpu.VMEM((n_sub, N_SLABS, SUBLANES * _pitch(tc), LANES), F32),
                        pltpu.VMEM((2, tcb, d), F32), pltpu.VMEM((tcb, D_FF), BF16), pltpu.VMEM((tcb, D_RNN), BF16)],
        compiler_params=pltpu.CompilerParams(dimension_semantics=("arbitrary",), vmem_limit_bytes=VMEM_LIMIT_BIG),
        name="l0_bwd_ffn",
    )(*args)
    return x2, jnp.concatenate([sf, sb], axis=1)


def _layer1(x, mod, mod_row, p, tc):
    bsz, t_len, d = x.shape
    nc = t_len // tc
    n_chunks = bsz * nc
    hb8 = tc // HALO
    n_hblk = t_len // HALO

    def mix_at(i):
        im = jnp.minimum(i, n_chunks - 1)
        return im // nc, im % nc

    def ffn_at(i):
        return mix_at(jnp.maximum(i - 1, 0))

    def halo_spec(blk_of):
        return pl.BlockSpec((1, HALO, d), lambda i: (mix_at(i)[0], blk_of(mix_at(i)[1]), 0))

    return pl.pallas_call(
        functools.partial(_pool_ffn_body, nc=nc, t_len=t_len, n_chunks=n_chunks),
        grid=(n_chunks + 1,),
        in_specs=[pl.BlockSpec((1, tc, d), lambda i: (*mix_at(i), 0)),
                  halo_spec(lambda c: jnp.maximum(c * hb8 - 1, 0)),
                  halo_spec(lambda c: jnp.minimum((c + 1) * hb8, n_hblk - 1)),
                  pl.BlockSpec((1, N_MOD, d), lambda i: (mod_row(mix_at(i)[0]), 0, 0)),
                  _const_spec((1, d)), _const_spec((1, d)),
                  _const_spec((len(POOL_WINDOWS), POOL_GROUP, POOL_GROUP)), _const_spec((1, d)), _const_spec((1, d)),
                  pl.BlockSpec((1, N_MOD, d), lambda i: (mod_row(ffn_at(i)[0]), 0, 0))] + _ffn_specs(d),
        out_specs=pl.BlockSpec((1, tc, d), lambda i: (*ffn_at(i), 0)),
        out_shape=jax.ShapeDtypeStruct((bsz, t_len, d), F32),
        scratch_shapes=[pltpu.VMEM((2, tc, d), F32), pltpu.VMEM((tc, D_FF), BF16)],
        compiler_params=pltpu.CompilerParams(dimension_semantics=("arbitrary",), vmem_limit_bytes=VMEM_LIMIT),
        name="l1_ffn",
    )(x, x, x, mod, p['mix_pre_g'], p['mix_post_g'], p['pool_w'], p['pool_b'], p['pool_scale'],
      mod, p['ffn_pre_g'], p['ffn_post_g'], p['ffn_w_in'], p['ffn_w_out'])


def _grid_pos_table(t_len):
    rows = t_len // GRID_W
    quarter = D_MODEL // 4
    omega = 1.0 / (POS_THETA ** (jnp.arange(quarter, dtype=F32) / quarter))
    ang = jnp.arange(max(rows, GRID_W), dtype=F32)[:, None] * omega[None, :]
    return jnp.concatenate([jnp.sin(ang), jnp.cos(ang)], axis=-1)


def _row(v):
    return v.reshape(1, -1)


def kernel(x_prompt, x_sample, state_l0_rglru, c, c_ctx, l0_mod_w, l0_mod_b, l0_mix_pre_g, l0_mix_post_g, l0_w_in, l0_conv_w, l0_conv_b, l0_gate_a_w, l0_gate_a_b, l0_gate_x_w, l0_gate_x_b, l0_lambda, l0_w_out, l0_ffn_pre_g, l0_ffn_post_g, l0_ffn_w_in, l0_ffn_w_out, l1_mod_w, l1_mod_b, l1_mix_pre_g, l1_mix_post_g, l1_pool_w, l1_pool_b, l1_pool_scale, l1_ffn_pre_g, l1_ffn_post_g, l1_ffn_w_in, l1_ffn_w_out):
    n_ctx, t_ctx, d = x_prompt.shape
    n_lat, t_lat, _ = x_sample.shape

    cond = jnp.concatenate(
        [c, c_ctx[None, :], jnp.zeros((MOD_ROWS - n_lat - 1, d), F32)], axis=0)
    mod0 = _modulation(cond, l0_mod_w, l0_mod_b)
    mod1 = _modulation(cond, l1_mod_w, l1_mod_b)

    p0 = dict(
        mix_pre_g=_row(l0_mix_pre_g), mix_post_g=_row(l0_mix_post_g), w_in=l0_w_in.astype(BF16),
        conv_w_half=0.5 * l0_conv_w, conv_b_half=_row(0.5 * l0_conv_b),
        wg=[jnp.concatenate([l0_gate_a_w[k], l0_gate_x_w[k]], axis=-1).astype(BF16) for k in range(2)],
        bg_half=[0.5 * jnp.stack([l0_gate_a_b[k], l0_gate_x_b[k]], axis=0) for k in range(2)],
        lam=[_row(l0_lambda[k]) for k in range(2)],
        w_out=l0_w_out.astype(BF16),
        ffn_pre_g=_row(l0_ffn_pre_g), ffn_post_g=_row(l0_ffn_post_g),
        ffn_w_in=l0_ffn_w_in.astype(BF16), ffn_w_out=l0_ffn_w_out.astype(BF16))
    p1 = dict(
        mix_pre_g=_row(l1_mix_pre_g), mix_post_g=_row(l1_mix_post_g), pool_w=l1_pool_w.astype(BF16),
        pool_b=_row(l1_pool_b), pool_scale=_row(l1_pool_scale),
        ffn_pre_g=_row(l1_ffn_pre_g), ffn_post_g=_row(l1_ffn_post_g),
        ffn_w_in=l1_ffn_w_in.astype(BF16), ffn_w_out=l1_ffn_w_out.astype(BF16))

    def run(x, pos, mod_row, h0):
        t_len = x.shape[1]
        x2, state = _layer0(x, pos, mod0, mod_row, h0, p0, min(CHUNK_L0, t_len))
        return _layer1(x2, mod1, mod_row, p1, min(CHUNK_L1, t_len)), state

    y_prompt, new_state = run(x_prompt, None, lambda b: CTX_ROW, jnp.zeros((n_ctx, 2, D_RNN), F32))
    y_sample, _ = run(x_sample, _grid_pos_table(t_lat), lambda b: b, state_l0_rglru)
    return y_prompt, y_sample, new_state
```

```python
import functools

import jax
import jax.numpy as jnp
from jax import lax
from jax.experimental import pallas as pl
from jax.experimental.pallas import tpu as pltpu

D_MODEL = 1024
D_RNN = D_MODEL
N_LRU_BLOCKS = 4
LRU_BLOCK = D_RNN // N_LRU_BLOCKS
CONV_W = 4
LRU_C = 8.0
POOL_WINDOWS = (2, 4, 8, 16)
POOL_GROUP = D_MODEL // len(POOL_WINDOWS)
D_FF = 2816
N_MOD = 6
EPS = 1e-6
POS_THETA = 10000.0
GRID_W = 64

LANES = 128
SUBLANES = 8
HALO = SUBLANES
N_SLABS = D_RNN // LANES
FF_CHUNK = 256
N_FF_CHUNKS = D_FF // FF_CHUNK
MOD_ROWS = 16
CTX_ROW = 8
MOD_TN = 1536
SQRT_FLOOR = 1e-36
CHUNK_L0 = 256
CHUNK_L1 = 512
FWD_CHUNKS = 4
BWD_CHUNKS = 2

VMEM_LIMIT = 56 * 1024 * 1024
VMEM_LIMIT_BIG = 62 * 1024 * 1024

F32 = jnp.float32
BF16 = jnp.bfloat16


def _dot(a, b):
    return jnp.dot(a, b, preferred_element_type=F32)


def _rms(x):
    return x * lax.rsqrt(jnp.mean(x * x, axis=-1, keepdims=True) + EPS)


def _sigmoid(x):
    return 1.0 / (1.0 + jnp.exp(-x))


def _mod_body(cond_ref, w_ref, b_ref, o_ref):
    c = cond_ref[...]
    s = (c * _sigmoid(c)).astype(BF16)
    o_ref[...] = _dot(s, w_ref[...].astype(BF16)) + b_ref[...]


def _modulation(cond, w, b):
    n = N_MOD * D_MODEL
    out = pl.pallas_call(
        _mod_body,
        grid=(n // MOD_TN,),
        in_specs=[
            pl.BlockSpec((MOD_ROWS, D_MODEL), lambda j: (0, 0)),
            pl.BlockSpec((D_MODEL, MOD_TN), lambda j: (0, j)),
            pl.BlockSpec((1, MOD_TN), lambda j: (0, j)),
        ],
        out_specs=pl.BlockSpec((MOD_ROWS, MOD_TN), lambda j: (0, j)),
        out_shape=jax.ShapeDtypeStruct((MOD_ROWS, n), F32),
        compiler_params=pltpu.CompilerParams(
            dimension_semantics=("parallel",), vmem_limit_bytes=VMEM_LIMIT),
        name="modulation",
    )(cond, w, b.reshape(1, n))
    return out.reshape(MOD_ROWS, N_MOD, D_MODEL)


def _gate_ab(hu, n, wg_ref, bg_ref, lam_ref):
    cols = slice(n * LRU_BLOCK, (n + 1) * LRU_BLOCK)
    o = _dot(hu.astype(BF16), wg_ref[n])
    t_r = jnp.tanh(o[:, :LRU_BLOCK] + bg_ref[0:1, cols])
    t_i = jnp.tanh(o[:, LRU_BLOCK:] + bg_ref[1:2, cols])
    nl = -lam_ref[:, cols]
    softplus = jnp.maximum(nl, 0.0) + jnp.log1p(jnp.exp(-jnp.abs(nl)))
    log_a = (t_r + 1.0) * ((-0.5 * LRU_C) * softplus)
    a = jnp.exp(log_a)
    s = jnp.tanh(log_a) * (-1.0 - a * a)
    root = s * lax.rsqrt(jnp.maximum(s, SQRT_FLOOR))
    b = root * (hu + hu * t_i)
    return a, b


def _scan(a, b, h_in, reverse):
    tc = a.shape[0]
    sub = tc // SUBLANES
    sub_id = lax.broadcasted_iota(jnp.int32, (SUBLANES, LANES), 0)
    steps = range(sub - 1, -1, -1) if reverse else range(sub)
    order = range(SUBLANES - 1, -1, -1) if reverse else range(SUBLANES)
    n_slabs = a.shape[1] // LANES
    hs = [[None] * sub for _ in range(n_slabs)]
    ps = [[None] * sub for _ in range(n_slabs)]
    h, p = [None] * n_slabs, [None] * n_slabs
    for j in steps:
        for k in range(n_slabs):
            av = a[j * SUBLANES:(j + 1) * SUBLANES, k * LANES:(k + 1) * LANES]
            bv = b[j * SUBLANES:(j + 1) * SUBLANES, k * LANES:(k + 1) * LANES]
            h[k] = bv if h[k] is None else av * h[k] + bv
            p[k] = av if p[k] is None else av * p[k]
            hs[k][j], ps[k][j] = h[k], p[k]
    slabs, last = [], []
    for k in range(n_slabs):
        carry = h_in[:, k * LANES:(k + 1) * LANES]
        carry_in = jnp.zeros((SUBLANES, LANES), F32)
        for s in order:
            carry_in = jnp.where(sub_id == s, carry, carry_in)
            carry = p[k][s:s + 1, :] * carry + h[k][s:s + 1, :]
        slabs.append(jnp.concatenate([ps[k][j] * carry_in + hs[k][j] for j in range(sub)], axis=0))
        last.append(carry)
    return jnp.concatenate(slabs, axis=1), jnp.concatenate(last, axis=1)


def _pitch(tc):
    return tc // SUBLANES + SUBLANES


def _permute_in(e, ext_s, tc, n_tiles):
    sub = tc // SUBLANES
    pitch = _pitch(tc)
    row = lax.broadcasted_iota(jnp.int32, (SUBLANES, LANES), 0)
    for k in range(N_SLABS):
        lanes = slice(k * LANES, (k + 1) * LANES)
        ext_s[k, 0:HALO, :] = e[0:HALO, lanes]
        for s in range(SUBLANES):
            base = HALO + s * pitch
            end = HALO + (s + 1) * sub
            ext_s[k, base:base + sub, :] = e[end - sub:end, lanes]
            ext_s[k, base + sub:base + pitch, :] = jnp.where(
                row < SUBLANES // 2, e[end:end + SUBLANES, lanes], e[end - SUBLANES:end, lanes])
    tiles = [jnp.concatenate([ext_s[k, pl.ds(HALO - 2 + q, SUBLANES, stride=pitch), :] for k in range(N_SLABS)],
                             axis=1) for q in range(n_tiles)]
    return jnp.concatenate(tiles, axis=0)


def _permute_out(r, o_s, tc):
    sub = tc // SUBLANES
    pitch = _pitch(tc)
    for k in range(N_SLABS):
        for j in range(sub):
            o_s[k, pl.ds(j, SUBLANES, stride=pitch), :] = r[j * SUBLANES:(j + 1) * SUBLANES, k * LANES:(k + 1) * LANES]
    return jnp.concatenate(
        [jnp.concatenate([o_s[k, s * pitch:s * pitch + sub, :] for s in range(SUBLANES)], axis=0)
         for k in range(N_SLABS)], axis=1)


def _pos_rows(ptab_ref, grid_row, col0, n):
    half = ptab_ref.shape[1]
    by_row = jnp.broadcast_to(ptab_ref[pl.ds(grid_row, 1), :], (n, half))
    return jnp.concatenate([by_row, ptab_ref[col0:col0 + n, :]], axis=1)


def _pos_block(ptab_ref, first_grid_row, n_rows):
    return jnp.concatenate([_pos_rows(ptab_ref, first_grid_row + k, 0, GRID_W) for k in range(n_rows // GRID_W)],
                           axis=0)


def _fwd_body(*refs, nc, tc, has_pos):
    if has_pos:
        (x_ref, xp_ref, xn_ref, ptab_ref, mod_ref, g_ref, w_ref, cw_ref, cb_ref, wg_ref, bg_ref,
         lam_ref, h0_ref, zg_ref, hu_ref, hf_ref, sf_ref, carry_ref, ext_s) = refs
    else:
        (x_ref, xp_ref, xn_ref, mod_ref, g_ref, w_ref, cw_ref, cb_ref, wg_ref, bg_ref,
         lam_ref, h0_ref, zg_ref, hu_ref, hf_ref, sf_ref, carry_ref, ext_s) = refs
    carry_ref = carry_ref.at[0:1]
    c = pl.program_id(1)
    n_sub = x_ref.shape[1] // tc
    sub = tc // SUBLANES
    n_tiles = sub + CONV_W

    @pl.when(c == 0)
    def _():
        carry_ref[...] = h0_ref[0]

    x, xp, xn = x_ref[0], xp_ref[0], xn_ref[0]
    if has_pos:
        rows_per_block = x.shape[0] // GRID_W
        n_grid_rows = nc * rows_per_block
        r0 = c * rows_per_block
        x = x + _pos_block(ptab_ref, r0, x.shape[0])
        xp = xp + _pos_rows(ptab_ref, jnp.maximum(r0 - 1, 0), GRID_W - HALO, HALO)
        xn = xn + _pos_rows(ptab_ref, jnp.minimum(r0 + rows_per_block, n_grid_rows - 1), 0, HALO)
    gs = g_ref[...] * (1.0 + mod_ref[0, 1:2, :])
    sh = mod_ref[0, 0:1, :]
    h_all = jnp.concatenate([_rms(xp) * gs + sh, _rms(x) * gs + sh, _rms(xn) * gs + sh], axis=0)
    row = lax.broadcasted_iota(jnp.int32, (2 * SUBLANES, D_RNN), 0) & (SUBLANES - 1)
    carry = carry_ref[...]
    for q in range(n_sub):
        rows = slice(q * tc, (q + 1) * tc)
        e = h_all[q * tc:(q + 1) * tc + 2 * HALO]
        hp = _permute_in(e, ext_s, tc, n_tiles).astype(BF16)
        zg_ref[0, rows, :] = _dot(hp[2 * SUBLANES:2 * SUBLANES + tc], w_ref[:, :D_RNN])
        rec = _dot(hp, w_ref[:, D_RNN:])
        if q == 0:
            head = jnp.where(row < jnp.where(c == 0, 1, 0), 0.0, rec[0:2 * SUBLANES])
            rec = jnp.concatenate([head, rec[2 * SUBLANES:]], axis=0)
        if q == n_sub - 1:
            tail = jnp.where(row > jnp.where(c == nc - 1, SUBLANES - 2, SUBLANES - 1), 0.0,
                             rec[(sub + 2) * SUBLANES:])
            rec = jnp.concatenate([rec[:(sub + 2) * SUBLANES], tail], axis=0)
        hu = rec[0:tc] * cw_ref[0:1, :]
        for k in range(1, CONV_W):
            hu = hu + rec[k * SUBLANES:k * SUBLANES + tc] * cw_ref[k:k + 1, :]
        hu = hu + cb_ref[...]
        hu_ref[0, rows, :] = hu
        ab = [_gate_ab(hu[:, n * LRU_BLOCK:(n + 1) * LRU_BLOCK], n, wg_ref, bg_ref, lam_ref)
              for n in range(N_LRU_BLOCKS)]
        hf, carry = _scan(jnp.concatenate([a for a, _ in ab], axis=1),
                          jnp.concatenate([b for _, b in ab], axis=1), carry, False)
        hf_ref[0, rows, :] = hf
    carry_ref[...] = carry
    sf_ref[0] = carry


def _ffn_head(x, mod_ref, pre_ref):
    sh = mod_ref[0, 3:4, :]
    sc = mod_ref[0, 4:5, :]
    return (_rms(x) * (pre_ref[...] * (1.0 + sc)) + sh).astype(BF16)


def _ffn_steps(h, wi_ref, wo_ref, act_ref, out):
    def chunk(n):
        def f():
            g = _dot(h, wi_ref[:, n * FF_CHUNK:(n + 1) * FF_CHUNK])
            v = _dot(h, wi_ref[:, D_FF + n * FF_CHUNK:D_FF + (n + 1) * FF_CHUNK])
            act_ref[:, n * FF_CHUNK:(n + 1) * FF_CHUNK] = (g * _sigmoid(g) * v).astype(BF16)
        return f

    def down():
        out.append(_dot(act_ref[...], wo_ref[...]))

    return [chunk(n) for n in range(N_FF_CHUNKS)] + [down]


def _ffn_tail(x, y, mod_ref, post_ref):
    return x + _rms(y) * (post_ref[...] * mod_ref[0, 5:6, :])


def _trace_interleaved(major, minor):
    done = 0
    for k, step in enumerate(major):
        step()
        while done < (k + 1) * len(minor) // len(major):
            minor[done]()
            done += 1


def _bwd_ffn_body(*refs, nc, tc, n_chunks, has_pos):
    if has_pos:
        (hu_ref, zg_ref, hf_ref, x_ref, ptab_ref, mod_ref, wg_ref, bg_ref, lam_ref, h0_ref, wo_ref, pg_ref,
         fmod_ref, pre_ref, post_ref, wi_ref, wo2_ref, x2_ref, sb_ref,
         carry_ref, o_s, x1_s, act_ref, mixed_ref) = refs
    else:
        (hu_ref, zg_ref, hf_ref, x_ref, mod_ref, wg_ref, bg_ref, lam_ref, h0_ref, wo_ref, pg_ref,
         fmod_ref, pre_ref, post_ref, wi_ref, wo2_ref, x2_ref, sb_ref,
         carry_ref, o_s, x1_s, act_ref, mixed_ref) = refs
    carry_ref = carry_ref.at[0:1]
    i = pl.program_id(0)
    c = jnp.minimum(i, n_chunks - 1) % nc
    n_sub = x_ref.shape[1] // tc

    @pl.when(i == 0)
    def _():
        x1_s[...] = jnp.zeros(x1_s.shape, F32)

    @pl.when(c == 0)
    def _():
        carry_ref[...] = h0_ref[0]

    st = {}
    lasts = [None] * N_LRU_BLOCKS

    def block(n, q):
        def f():
            rows = slice(q * tc, (q + 1) * tc)
            cols = slice(n * LRU_BLOCK, (n + 1) * LRU_BLOCK)
            a, b = _gate_ab(hu_ref[0, rows, cols], n, wg_ref, bg_ref, lam_ref)
            h_in = carry_ref[:, cols] if lasts[n] is None else lasts[n]
            hb, lasts[n] = _scan(a, b, h_in, True)
            mixed_ref[rows, cols] = ((hf_ref[0, rows, cols] + hb) * jax.nn.gelu(zg_ref[0, rows, cols])).astype(BF16)
        return f

    def out_proj():
        st['y'] = _dot(mixed_ref[...], wo_ref[...])

    def residual():
        x = x_ref[0]
        if has_pos:
            x = x + _pos_block(ptab_ref, (nc - 1 - c) * (x.shape[0] // GRID_W), x.shape[0])
        r = _rms(st['y']) * (pg_ref[...] * mod_ref[0, 2:3, :])
        x1_s[i % 2] = x + jnp.concatenate(
            [_permute_out(r[q * tc:(q + 1) * tc], o_s.at[q], tc) for q in range(n_sub)], axis=0)

    mixer = ([block(n, q) for q in range(n_sub - 1, -1, -1) for n in range(N_LRU_BLOCKS)]
             + [out_proj, residual])

    xf = x1_s[(i + 1) % 2]
    yf = []
    _trace_interleaved(_ffn_steps(_ffn_head(xf, fmod_ref, pre_ref), wi_ref, wo2_ref, act_ref, yf), mixer)
    x2_ref[0] = _ffn_tail(xf, yf[0], fmod_ref, post_ref)
    h_last = jnp.concatenate(lasts, axis=1)

    @pl.when(i < n_chunks)
    def _():
        carry_ref[...] = h_last
        sb_ref[0] = h_last


def _pool_group(ext, gi, c, pw_ref, t_len):
    n = ext.shape[0]
    tc = n - 2 * HALO
    win = POOL_WINDOWS[gi]
    e = ext[:, gi * POOL_GROUP:(gi + 1) * POOL_GROUP]
    w = e + pltpu.roll(e, 1, 0)
    half = 1
    while 2 * half < win:
        w = pltpu.roll(w, half, 0) + pltpu.roll(w, n - half, 0)
        half *= 2
    t = c * tc + lax.broadcasted_iota(jnp.int32, (tc, POOL_GROUP), 0)
    cnt = (jnp.minimum(t + win // 2, t_len) - jnp.maximum(t - win // 2, 0)).astype(F32)
    pooled = w[HALO:n - HALO] / cnt - e[HALO:n - HALO]
    return _dot(pooled.astype(BF16), pw_ref[gi])


def _pool_ffn_body(x_ref, xp_ref, xn_ref, mod_ref, mpre_ref, mpost_ref, pw_ref, pb_ref, ps_ref,
                   fmod_ref, pre_ref, post_ref, wi_ref, wo_ref, o_ref, x3_s, act_ref, *, nc, t_len, n_chunks):
    i = pl.program_id(0)
    c = jnp.minimum(i, n_chunks - 1) % nc

    @pl.when(i == 0)
    def _():
        x3_s[...] = jnp.zeros(x3_s.shape, F32)

    st = {}
    ys = [None] * len(POOL_WINDOWS)

    def pre_norm():
        gs = mpre_ref[...] * (1.0 + mod_ref[0, 1:2, :])
        sh = mod_ref[0, 0:1, :]
        st['ext'] = jnp.concatenate([jnp.where(c > 0, _rms(xp_ref[0]) * gs + sh, 0.0),
                                     _rms(x_ref[0]) * gs + sh,
                                     jnp.where(c < nc - 1, _rms(xn_ref[0]) * gs + sh, 0.0)], axis=0)

    def group(gi):
        def f():
            ys[gi] = _pool_group(st['ext'], gi, c, pw_ref, t_len)
        return f

    def residual():
        y = (jnp.concatenate(ys, axis=1) + pb_ref[...]) * ps_ref[...]
        x3_s[i % 2] = x_ref[0] + _rms(y) * (mpost_ref[...] * mod_ref[0, 2:3, :])

    mixer = [pre_norm] + [group(gi) for gi in range(len(POOL_WINDOWS))] + [residual]

    xf = x3_s[(i + 1) % 2]
    yf = []
    _trace_interleaved(_ffn_steps(_ffn_head(xf, fmod_ref, pre_ref), wi_ref, wo_ref, act_ref, yf), mixer)
    o_ref[0] = _ffn_tail(xf, yf[0], fmod_ref, post_ref)


def _const_spec(shape, single=False):
    nd = len(shape)
    if single:
        return pl.BlockSpec(shape, lambda *_: (0,) * nd, pipeline_mode=pl.Buffered(1))
    return pl.BlockSpec(shape, lambda *_: (0,) * nd)


def _ffn_specs(d):
    return [_const_spec((1, d)), _const_spec((1, d)),
            _const_spec((d, 2 * D_FF), single=True), _const_spec((D_FF, d), single=True)]


def _layer0(x, pos, mod, mod_row, h0, p, tc):
    bsz, t_len, d = x.shape
    nc = t_len // tc
    n_hblk = t_len // HALO
    has_pos = pos is not None
    assert t_len % tc == 0 and tc % (2 * SUBLANES * SUBLANES) == 0, (t_len, tc)
    assert not has_pos or tc % GRID_W == 0, "position rows are built per whole grid row"
    act = jax.ShapeDtypeStruct((bsz, t_len, d), F32)
    st = jax.ShapeDtypeStruct((bsz, 1, d), F32)

    tcb = tc * min(FWD_CHUNKS, nc)
    hb8 = tcb // HALO
    prev_blk = lambda c: jnp.maximum(c * hb8 - 1, 0)
    next_blk = lambda c: jnp.minimum((c + 1) * hb8, n_hblk - 1)
    row_spec = pl.BlockSpec((1, d), lambda b, c: (0, 0))
    chunk = pl.BlockSpec((1, tcb, d), lambda b, c: (b, c, 0))
    state_spec = pl.BlockSpec((1, 1, d), lambda b, c: (b, 0, 0))
    pos_specs = [_const_spec(pos.shape)] if has_pos else []
    in_specs = ([chunk,
                 pl.BlockSpec((1, HALO, d), lambda b, c: (b, prev_blk(c), 0)),
                 pl.BlockSpec((1, HALO, d), lambda b, c: (b, next_blk(c), 0))]
                + pos_specs
                + [pl.BlockSpec((1, N_MOD, d), lambda b, c: (mod_row(b), 0, 0)), row_spec,
                   _const_spec((d, 2 * D_RNN)), _const_spec((CONV_W, d)), row_spec,
                   _const_spec((N_LRU_BLOCKS, LRU_BLOCK, 2 * LRU_BLOCK)), _const_spec((2, d)), row_spec, state_spec])
    args = ([x, x, x] + ([pos] if has_pos else [])
            + [mod, p['mix_pre_g'], p['w_in'], p['conv_w_half'], p['conv_b_half'], p['wg'][0], p['bg_half'][0],
               p['lam'][0], h0[:, 0:1]])
    zg, hu, hf, sf = pl.pallas_call(
        functools.partial(_fwd_body, nc=t_len // tcb, tc=tc, has_pos=has_pos),
        grid=(bsz, t_len // tcb), in_specs=in_specs, out_specs=[chunk, chunk, chunk, state_spec],
        out_shape=[act, act, act, st],
        scratch_shapes=[pltpu.VMEM((SUBLANES, d), F32),
                        pltpu.VMEM((N_SLABS, HALO + SUBLANES * _pitch(tc), LANES), F32)],
        compiler_params=pltpu.CompilerParams(
            dimension_semantics=("parallel", "arbitrary"), vmem_limit_bytes=VMEM_LIMIT),
        name="l0_fwd",
    )(*args)

    n_sub = min(BWD_CHUNKS, nc)
    tcb = tc * n_sub
    nb = t_len // tcb
    n_blocks = bsz * nb

    def mix_at(i):
        im = jnp.minimum(i, n_blocks - 1)
        return im // nb, nb - 1 - im % nb

    def ffn_at(i):
        return mix_at(jnp.maximum(i - 1, 0))

    mchunk = pl.BlockSpec((1, tcb, d), lambda i: (*mix_at(i), 0))
    in_specs = ([mchunk, mchunk, mchunk, mchunk]
                + pos_specs
                + [pl.BlockSpec((1, N_MOD, d), lambda i: (mod_row(mix_at(i)[0]), 0, 0)),
                   _const_spec((N_LRU_BLOCKS, LRU_BLOCK, 2 * LRU_BLOCK)), _const_spec((2, d)), _const_spec((1, d)),
                   pl.BlockSpec((1, 1, d), lambda i: (mix_at(i)[0], 0, 0)),
                   _const_spec((D_RNN, d)), _const_spec((1, d)),
                   pl.BlockSpec((1, N_MOD, d), lambda i: (mod_row(ffn_at(i)[0]), 0, 0))]
                + _ffn_specs(d))
    args = ([hu, zg, hf, x] + ([pos] if has_pos else [])
            + [mod, p['wg'][1], p['bg_half'][1], p['lam'][1], h0[:, 1:2], p['w_out'], p['mix_post_g'],
               mod, p['ffn_pre_g'], p['ffn_post_g'], p['ffn_w_in'], p['ffn_w_out']])
    x2, sb = pl.pallas_call(
        functools.partial(_bwd_ffn_body, nc=nb, tc=tc, n_chunks=n_blocks, has_pos=has_pos),
        grid=(n_blocks + 1,), in_specs=in_specs,
        out_specs=[pl.BlockSpec((1, tcb, d), lambda i: (*ffn_at(i), 0)),
                   pl.BlockSpec((1, 1, d), lambda i: (mix_at(i)[0], 0, 0))],
        out_shape=[act, st],
        scratch_shapes=[pltpu.VMEM((SUBLANES, d), F32),
                        pltpu.VMEM((n_sub, N_SLABS, SUBLANES * _pitch(tc), LANES), F32),
                        pltpu.VMEM((2, tcb, d), F32), pltpu.VMEM((tcb, D_FF), BF16), pltpu.VMEM((tcb, D_RNN), BF16)],
        compiler_params=pltpu.CompilerParams(dimension_semantics=("arbitrary",), vmem_limit_bytes=VMEM_LIMIT_BIG),
        name="l0_bwd_ffn",
    )(*args)
    return x2, jnp.concatenate([sf, sb], axis=1)


def _layer1(x, mod, mod_row, p, tc):
    bsz, t_len, d = x.shape
    nc = t_len // tc
    n_chunks = bsz * nc
    hb8 = tc // HALO
    n_hblk = t_len // HALO

    def mix_at(i):
        im = jnp.minimum(i, n_chunks - 1)
        return im // nc, im % nc

    def ffn_at(i):
        return mix_at(jnp.maximum(i - 1, 0))

    def halo_spec(blk_of):
        return pl.BlockSpec((1, HALO, d), lambda i: (mix_at(i)[0], blk_of(mix_at(i)[1]), 0))

    return pl.pallas_call(
        functools.partial(_pool_ffn_body, nc=nc, t_len=t_len, n_chunks=n_chunks),
        grid=(n_chunks + 1,),
        in_specs=[pl.BlockSpec((1, tc, d), lambda i: (*mix_at(i), 0)),
                  halo_spec(lambda c: jnp.maximum(c * hb8 - 1, 0)),
                  halo_spec(lambda c: jnp.minimum((c + 1) * hb8, n_hblk - 1)),
                  pl.BlockSpec((1, N_MOD, d), lambda i: (mod_row(mix_at(i)[0]), 0, 0)),
                  _const_spec((1, d)), _const_spec((1, d)),
                  _const_spec((len(POOL_WINDOWS), POOL_GROUP, POOL_GROUP)), _const_spec((1, d)), _const_spec((1, d)),
                  pl.BlockSpec((1, N_MOD, d), lambda i: (mod_row(ffn_at(i)[0]), 0, 0))] + _ffn_specs(d),
        out_specs=pl.BlockSpec((1, tc, d), lambda i: (*ffn_at(i), 0)),
        out_shape=jax.ShapeDtypeStruct((bsz, t_len, d), F32),
        scratch_shapes=[pltpu.VMEM((2, tc, d), F32), pltpu.VMEM((tc, D_FF), BF16)],
        compiler_params=pltpu.CompilerParams(dimension_semantics=("arbitrary",), vmem_limit_bytes=VMEM_LIMIT),
        name="l1_ffn",
    )(x, x, x, mod, p['mix_pre_g'], p['mix_post_g'], p['pool_w'], p['pool_b'], p['pool_scale'],
      mod, p['ffn_pre_g'], p['ffn_post_g'], p['ffn_w_in'], p['ffn_w_out'])


def _grid_pos_table(t_len):
    rows = t_len // GRID_W
    quarter = D_MODEL // 4
    omega = 1.0 / (POS_THETA ** (jnp.arange(quarter, dtype=F32) / quarter))
    ang = jnp.arange(max(rows, GRID_W), dtype=F32)[:, None] * omega[None, :]
    return jnp.concatenate([jnp.sin(ang), jnp.cos(ang)], axis=-1)


def _row(v):
    return v.reshape(1, -1)


def kernel(x_prompt, x_sample, state_l0_rglru, c, c_ctx, l0_mod_w, l0_mod_b, l0_mix_pre_g, l0_mix_post_g, l0_w_in, l0_conv_w, l0_conv_b, l0_gate_a_w, l0_gate_a_b, l0_gate_x_w, l0_gate_x_b, l0_lambda, l0_w_out, l0_ffn_pre_g, l0_ffn_post_g, l0_ffn_w_in, l0_ffn_w_out, l1_mod_w, l1_mod_b, l1_mix_pre_g, l1_mix_post_g, l1_pool_w, l1_pool_b, l1_pool_scale, l1_ffn_pre_g, l1_ffn_post_g, l1_ffn_w_in, l1_ffn_w_out):
    n_ctx, t_ctx, d = x_prompt.shape
    n_lat, t_lat, _ = x_sample.shape

    cond = jnp.concatenate(
        [c, c_ctx[None, :], jnp.zeros((MOD_ROWS - n_lat - 1, d), F32)], axis=0)
    mod0 = _modulation(cond, l0_mod_w, l0_mod_b)
    mod1 = _modulation(cond, l1_mod_w, l1_mod_b)

    p0 = dict(
        mix_pre_g=_row(l0_mix_pre_g), mix_post_g=_row(l0_mix_post_g), w_in=l0_w_in.astype(BF16),
        conv_w_half=0.5 * l0_conv_w, conv_b_half=_row(0.5 * l0_conv_b),
        wg=[jnp.concatenate([l0_gate_a_w[k], l0_gate_x_w[k]], axis=-1).astype(BF16) for k in range(2)],
        bg_half=[0.5 * jnp.stack([l0_gate_a_b[k], l0_gate_x_b[k]], axis=0) for k in range(2)],
        lam=[_row(l0_lambda[k]) for k in range(2)],
        w_out=l0_w_out.astype(BF16),
        ffn_pre_g=_row(l0_ffn_pre_g), ffn_post_g=_row(l0_ffn_post_g),
        ffn_w_in=l0_ffn_w_in.astype(BF16), ffn_w_out=l0_ffn_w_out.astype(BF16))
    p1 = dict(
        mix_pre_g=_row(l1_mix_pre_g), mix_post_g=_row(l1_mix_post_g), pool_w=l1_pool_w.astype(BF16),
        pool_b=_row(l1_pool_b), pool_scale=_row(l1_pool_scale),
        ffn_pre_g=_row(l1_ffn_pre_g), ffn_post_g=_row(l1_ffn_post_g),
        ffn_w_in=l1_ffn_w_in.astype(BF16), ffn_w_out=l1_ffn_w_out.astype(BF16))

    def run(x, pos, mod_row, h0):
        t_len = x.shape[1]
        x2, state = _layer0(x, pos, mod0, mod_row, h0, p0, min(CHUNK_L0, t_len))
        return _layer1(x2, mod1, mod_row, p1, min(CHUNK_L1, t_len)), state

    y_prompt, new_state = run(x_prompt, None, lambda b: CTX_ROW, jnp.zeros((n_ctx, 2, D_RNN), F32))
    y_sample, _ = run(x_sample, _grid_pos_table(t_lat), lambda b: b, state_l0_rglru)
    return y_prompt, y_sample, new_state
```

```python
import functools

import jax
import jax.numpy as jnp
from jax import lax
from jax.experimental import pallas as pl
from jax.experimental.pallas import tpu as pltpu

D_MODEL = 1024
D_RNN = D_MODEL
N_LRU_BLOCKS = 4
LRU_BLOCK = D_RNN // N_LRU_BLOCKS
CONV_W = 4
LRU_C = 8.0
POOL_WINDOWS = (2, 4, 8, 16)
POOL_GROUP = D_MODEL // len(POOL_WINDOWS)
D_FF = 2816
N_MOD = 6
EPS = 1e-6
POS_THETA = 10000.0
GRID_W = 64

LANES = 128
SUBLANES = 8
HALO = SUBLANES
N_SLABS = D_RNN // LANES
FF_CHUNK = 256
N_FF_CHUNKS = D_FF // FF_CHUNK
MOD_ROWS = 16
CTX_ROW = 8
MOD_TN = 1536
SQRT_FLOOR = 1e-36
CHUNK_L0 = 256
CHUNK_L1 = 512
FWD_CHUNKS = 2
BWD_CHUNKS = 2

VMEM_LIMIT = 56 * 1024 * 1024
VMEM_LIMIT_BIG = 62 * 1024 * 1024

F32 = jnp.float32
BF16 = jnp.bfloat16


def _dot(a, b):
    return jnp.dot(a, b, preferred_element_type=F32)


def _rms(x):
    return x * lax.rsqrt(jnp.mean(x * x, axis=-1, keepdims=True) + EPS)


def _sigmoid(x):
    return 1.0 / (1.0 + jnp.exp(-x))


def _mod_body(cond_ref, w_ref, b_ref, o_ref):
    c = cond_ref[...]
    s = (c * _sigmoid(c)).astype(BF16)
    o_ref[...] = _dot(s, w_ref[...].astype(BF16)) + b_ref[...]


def _modulation(cond, w, b):
    n = N_MOD * D_MODEL
    out = pl.pallas_call(
        _mod_body,
        grid=(n // MOD_TN,),
        in_specs=[
            pl.BlockSpec((MOD_ROWS, D_MODEL), lambda j: (0, 0)),
            pl.BlockSpec((D_MODEL, MOD_TN), lambda j: (0, j)),
            pl.BlockSpec((1, MOD_TN), lambda j: (0, j)),
        ],
        out_specs=pl.BlockSpec((MOD_ROWS, MOD_TN), lambda j: (0, j)),
        out_shape=jax.ShapeDtypeStruct((MOD_ROWS, n), F32),
        compiler_params=pltpu.CompilerParams(
            dimension_semantics=("parallel",), vmem_limit_bytes=VMEM_LIMIT),
        name="modulation",
    )(cond, w, b.reshape(1, n))
    return out.reshape(MOD_ROWS, N_MOD, D_MODEL)


def _gate_ab(hu, n, wg_ref, bg_ref, lam_ref):
    cols = slice(n * LRU_BLOCK, (n + 1) * LRU_BLOCK)
    o = _dot(hu.astype(BF16), wg_ref[n])
    t_r = jnp.tanh(o[:, :LRU_BLOCK] + bg_ref[0:1, cols])
    t_i = jnp.tanh(o[:, LRU_BLOCK:] + bg_ref[1:2, cols])
    nl = -lam_ref[:, cols]
    softplus = jnp.maximum(nl, 0.0) + jnp.log1p(jnp.exp(-jnp.abs(nl)))
    log_a = (t_r + 1.0) * ((-0.5 * LRU_C) * softplus)
    a = jnp.exp(log_a)
    s = jnp.tanh(log_a) * (-1.0 - a * a)
    root = s * lax.rsqrt(jnp.maximum(s, SQRT_FLOOR))
    b = root * (hu + hu * t_i)
    return a, b


def _scan(a, b, h_in, reverse):
    tc = a.shape[0]
    sub = tc // SUBLANES
    sub_id = lax.broadcasted_iota(jnp.int32, (SUBLANES, LANES), 0)
    steps = range(sub - 1, -1, -1) if reverse else range(sub)
    order = range(SUBLANES - 1, -1, -1) if reverse else range(SUBLANES)
    n_slabs = a.shape[1] // LANES
    hs = [[None] * sub for _ in range(n_slabs)]
    ps = [[None] * sub for _ in range(n_slabs)]
    h, p = [None] * n_slabs, [None] * n_slabs
    for j in steps:
        for k in range(n_slabs):
            av = a[j * SUBLANES:(j + 1) * SUBLANES, k * LANES:(k + 1) * LANES]
            bv = b[j * SUBLANES:(j + 1) * SUBLANES, k * LANES:(k + 1) * LANES]
            h[k] = bv if h[k] is None else av * h[k] + bv
            p[k] = av if p[k] is None else av * p[k]
            hs[k][j], ps[k][j] = h[k], p[k]
    slabs, last = [], []
    for k in range(n_slabs):
        carry = h_in[:, k * LANES:(k + 1) * LANES]
        carry_in = jnp.zeros((SUBLANES, LANES), F32)
        for s in order:
            carry_in = jnp.where(sub_id == s, carry, carry_in)
            carry = p[k][s:s + 1, :] * carry + h[k][s:s + 1, :]
        slabs.append(jnp.concatenate([ps[k][j] * carry_in + hs[k][j] for j in range(sub)], axis=0))
        last.append(carry)
    return jnp.concatenate(slabs, axis=1), jnp.concatenate(last, axis=1)


def _pitch(tc):
    return tc // SUBLANES + SUBLANES


def _permute_in(e, ext_s, tc, n_tiles):
    sub = tc // SUBLANES
    pitch = _pitch(tc)
    row = lax.broadcasted_iota(jnp.int32, (SUBLANES, LANES), 0)
    for k in range(N_SLABS):
        lanes = slice(k * LANES, (k + 1) * LANES)
        ext_s[k, 0:HALO, :] = e[0:HALO, lanes]
        for s in range(SUBLANES):
            base = HALO + s * pitch
            end = HALO + (s + 1) * sub
            ext_s[k, base:base + sub, :] = e[end - sub:end, lanes]
            ext_s[k, base + sub:base + pitch, :] = jnp.where(
                row < SUBLANES // 2, e[end:end + SUBLANES, lanes], e[end - SUBLANES:end, lanes])
    tiles = [jnp.concatenate([ext_s[k, pl.ds(HALO - 2 + q, SUBLANES, stride=pitch), :] for k in range(N_SLABS)],
                             axis=1) for q in range(n_tiles)]
    return jnp.concatenate(tiles, axis=0)


def _permute_out(r, o_s, tc):
    sub = tc // SUBLANES
    pitch = _pitch(tc)
    for k in range(N_SLABS):
        for j in range(sub):
            o_s[k, pl.ds(j, SUBLANES, stride=pitch), :] = r[j * SUBLANES:(j + 1) * SUBLANES, k * LANES:(k + 1) * LANES]
    return jnp.concatenate(
        [jnp.concatenate([o_s[k, s * pitch:s * pitch + sub, :] for s in range(SUBLANES)], axis=0)
         for k in range(N_SLABS)], axis=1)


def _pos_rows(ptab_ref, grid_row, col0, n):
    half = ptab_ref.shape[1]
    by_row = jnp.broadcast_to(ptab_ref[pl.ds(grid_row, 1), :], (n, half))
    return jnp.concatenate([by_row, ptab_ref[col0:col0 + n, :]], axis=1)


def _pos_block(ptab_ref, first_grid_row, n_rows):
    return jnp.concatenate([_pos_rows(ptab_ref, first_grid_row + k, 0, GRID_W) for k in range(n_rows // GRID_W)],
                           axis=0)


def _fwd_body(*refs, nc, tc, has_pos):
    if has_pos:
        (x_ref, xp_ref, xn_ref, ptab_ref, mod_ref, g_ref, w_ref, cw_ref, cb_ref, wg_ref, bg_ref,
         lam_ref, h0_ref, zg_ref, hu_ref, hf_ref, sf_ref, carry_ref, ext_s) = refs
    else:
        (x_ref, xp_ref, xn_ref, mod_ref, g_ref, w_ref, cw_ref, cb_ref, wg_ref, bg_ref,
         lam_ref, h0_ref, zg_ref, hu_ref, hf_ref, sf_ref, carry_ref, ext_s) = refs
    carry_ref = carry_ref.at[0:1]
    c = pl.program_id(1)
    n_sub = x_ref.shape[1] // tc
    sub = tc // SUBLANES
    n_tiles = sub + CONV_W

    @pl.when(c == 0)
    def _():
        carry_ref[...] = h0_ref[0]

    x, xp, xn = x_ref[0], xp_ref[0], xn_ref[0]
    if has_pos:
        rows_per_block = x.shape[0] // GRID_W
        n_grid_rows = nc * rows_per_block
        r0 = c * rows_per_block
        x = x + _pos_block(ptab_ref, r0, x.shape[0])
        xp = xp + _pos_rows(ptab_ref, jnp.maximum(r0 - 1, 0), GRID_W - HALO, HALO)
        xn = xn + _pos_rows(ptab_ref, jnp.minimum(r0 + rows_per_block, n_grid_rows - 1), 0, HALO)
    gs = g_ref[...] * (1.0 + mod_ref[0, 1:2, :])
    sh = mod_ref[0, 0:1, :]
    h_all = jnp.concatenate([_rms(xp) * gs + sh, _rms(x) * gs + sh, _rms(xn) * gs + sh], axis=0)
    row = lax.broadcasted_iota(jnp.int32, (2 * SUBLANES, D_RNN), 0) & (SUBLANES - 1)
    carry = carry_ref[...]
    for q in range(n_sub):
        rows = slice(q * tc, (q + 1) * tc)
        e = h_all[q * tc:(q + 1) * tc + 2 * HALO]
        hp = _permute_in(e, ext_s, tc, n_tiles).astype(BF16)
        zg_ref[0, rows, :] = _dot(hp[2 * SUBLANES:2 * SUBLANES + tc], w_ref[:, :D_RNN])
        rec = _dot(hp, w_ref[:, D_RNN:])
        if q == 0:
            head = jnp.where(row < jnp.where(c == 0, 1, 0), 0.0, rec[0:2 * SUBLANES])
            rec = jnp.concatenate([head, rec[2 * SUBLANES:]], axis=0)
        if q == n_sub - 1:
            tail = jnp.where(row > jnp.where(c == nc - 1, SUBLANES - 2, SUBLANES - 1), 0.0,
                             rec[(sub + 2) * SUBLANES:])
            rec = jnp.concatenate([rec[:(sub + 2) * SUBLANES], tail], axis=0)
        hu = rec[0:tc] * cw_ref[0:1, :]
        for k in range(1, CONV_W):
            hu = hu + rec[k * SUBLANES:k * SUBLANES + tc] * cw_ref[k:k + 1, :]
        hu = hu + cb_ref[...]
        hu_ref[0, rows, :] = hu
        ab = [_gate_ab(hu[:, n * LRU_BLOCK:(n + 1) * LRU_BLOCK], n, wg_ref, bg_ref, lam_ref)
              for n in range(N_LRU_BLOCKS)]
        hf, carry = _scan(jnp.concatenate([a for a, _ in ab], axis=1),
                          jnp.concatenate([b for _, b in ab], axis=1), carry, False)
        hf_ref[0, rows, :] = hf
    carry_ref[...] = carry
    sf_ref[0] = carry


def _ffn_head(x, mod_ref, pre_ref):
    sh = mod_ref[0, 3:4, :]
    sc = mod_ref[0, 4:5, :]
    return (_rms(x) * (pre_ref[...] * (1.0 + sc)) + sh).astype(BF16)


def _ffn_steps(h, wi_ref, wo_ref, act_ref, out):
    def chunk(n):
        def f():
            g = _dot(h, wi_ref[:, n * FF_CHUNK:(n + 1) * FF_CHUNK])
            v = _dot(h, wi_ref[:, D_FF + n * FF_CHUNK:D_FF + (n + 1) * FF_CHUNK])
            hg = 0.5 * g
            act_ref[:, n * FF_CHUNK:(n + 1) * FF_CHUNK] = ((hg + hg * jnp.tanh(hg)) * v).astype(BF16)
        return f

    def down():
        out.append(_dot(act_ref[...], wo_ref[...]))

    return [chunk(n) for n in range(N_FF_CHUNKS)] + [down]


def _ffn_tail(x, y, mod_ref, post_ref):
    return x + _rms(y) * (post_ref[...] * mod_ref[0, 5:6, :])


def _trace_interleaved(major, minor):
    done = 0
    for k, step in enumerate(major):
        step()
        while done < (k + 1) * len(minor) // len(major):
            minor[done]()
            done += 1


def _bwd_ffn_body(*refs, nc, tc, n_chunks, has_pos):
    if has_pos:
        (hu_ref, zg_ref, hf_ref, x_ref, ptab_ref, mod_ref, wg_ref, bg_ref, lam_ref, h0_ref, wo_ref, pg_ref,
         fmod_ref, pre_ref, post_ref, wi_ref, wo2_ref, x2_ref, sb_ref,
         carry_ref, o_s, x1_s, act_ref, mixed_ref) = refs
    else:
        (hu_ref, zg_ref, hf_ref, x_ref, mod_ref, wg_ref, bg_ref, lam_ref, h0_ref, wo_ref, pg_ref,
         fmod_ref, pre_ref, post_ref, wi_ref, wo2_ref, x2_ref, sb_ref,
         carry_ref, o_s, x1_s, act_ref, mixed_ref) = refs
    carry_ref = carry_ref.at[0:1]
    i = pl.program_id(0)
    c = jnp.minimum(i, n_chunks - 1) % nc
    n_sub = x_ref.shape[1] // tc

    @pl.when(i == 0)
    def _():
        x1_s[...] = jnp.zeros(x1_s.shape, F32)

    @pl.when(c == 0)
    def _():
        carry_ref[...] = h0_ref[0]

    st = {}
    lasts = [None] * N_LRU_BLOCKS

    def block(n, q):
        def f():
            rows = slice(q * tc, (q + 1) * tc)
            cols = slice(n * LRU_BLOCK, (n + 1) * LRU_BLOCK)
            a, b = _gate_ab(hu_ref[0, rows, cols], n, wg_ref, bg_ref, lam_ref)
            h_in = carry_ref[:, cols] if lasts[n] is None else lasts[n]
            hb, lasts[n] = _scan(a, b, h_in, True)
            mixed_ref[rows, cols] = ((hf_ref[0, rows, cols] + hb) * jax.nn.gelu(zg_ref[0, rows, cols])).astype(BF16)
        return f

    def out_proj():
        st['y'] = _dot(mixed_ref[...], wo_ref[...])

    def residual():
        x = x_ref[0]
        if has_pos:
            x = x + _pos_block(ptab_ref, (nc - 1 - c) * (x.shape[0] // GRID_W), x.shape[0])
        r = _rms(st['y']) * (pg_ref[...] * mod_ref[0, 2:3, :])
        x1_s[i % 2] = x + jnp.concatenate(
            [_permute_out(r[q * tc:(q + 1) * tc], o_s.at[q], tc) for q in range(n_sub)], axis=0)

    mixer = ([block(n, q) for q in range(n_sub - 1, -1, -1) for n in range(N_LRU_BLOCKS)]
             + [out_proj, residual])

    xf = x1_s[(i + 1) % 2]
    yf = []
    _trace_interleaved(_ffn_steps(_ffn_head(xf, fmod_ref, pre_ref), wi_ref, wo2_ref, act_ref, yf), mixer)
    x2_ref[0] = _ffn_tail(xf, yf[0], fmod_ref, post_ref)
    h_last = jnp.concatenate(lasts, axis=1)

    @pl.when(i < n_chunks)
    def _():
        carry_ref[...] = h_last
        sb_ref[0] = h_last


def _pool_group(ext, gi, c, pw_ref, t_len):
    n = ext.shape[0]
    tc = n - 2 * HALO
    win = POOL_WINDOWS[gi]
    e = ext[:, gi * POOL_GROUP:(gi + 1) * POOL_GROUP]
    w = e + pltpu.roll(e, 1, 0)
    half = 1
    while 2 * half < win:
        w = pltpu.roll(w, half, 0) + pltpu.roll(w, n - half, 0)
        half *= 2
    t = c * tc + lax.broadcasted_iota(jnp.int32, (tc, POOL_GROUP), 0)
    cnt = (jnp.minimum(t + win // 2, t_len) - jnp.maximum(t - win // 2, 0)).astype(F32)
    pooled = w[HALO:n - HALO] / cnt - e[HALO:n - HALO]
    return _dot(pooled.astype(BF16), pw_ref[gi])


def _pool_ffn_body(x_ref, xp_ref, xn_ref, mod_ref, mpre_ref, mpost_ref, pw_ref, pb_ref, ps_ref,
                   fmod_ref, pre_ref, post_ref, wi_ref, wo_ref, o_ref, x3_s, act_ref, *, nc, t_len, n_chunks):
    i = pl.program_id(0)
    c = jnp.minimum(i, n_chunks - 1) % nc

    @pl.when(i == 0)
    def _():
        x3_s[...] = jnp.zeros(x3_s.shape, F32)

    st = {}
    ys = [None] * len(POOL_WINDOWS)

    def pre_norm():
        gs = mpre_ref[...] * (1.0 + mod_ref[0, 1:2, :])
        sh = mod_ref[0, 0:1, :]
        st['ext'] = jnp.concatenate([jnp.where(c > 0, _rms(xp_ref[0]) * gs + sh, 0.0),
                                     _rms(x_ref[0]) * gs + sh,
                                     jnp.where(c < nc - 1, _rms(xn_ref[0]) * gs + sh, 0.0)], axis=0)

    def group(gi):
        def f():
            ys[gi] = _pool_group(st['ext'], gi, c, pw_ref, t_len)
        return f

    def residual():
        y = (jnp.concatenate(ys, axis=1) + pb_ref[...]) * ps_ref[...]
        x3_s[i % 2] = x_ref[0] + _rms(y) * (mpost_ref[...] * mod_ref[0, 2:3, :])

    mixer = [pre_norm] + [group(gi) for gi in range(len(POOL_WINDOWS))] + [residual]

    xf = x3_s[(i + 1) % 2]
    yf = []
    _trace_interleaved(_ffn_steps(_ffn_head(xf, fmod_ref, pre_ref), wi_ref, wo_ref, act_ref, yf), mixer)
    o_ref[0] = _ffn_tail(xf, yf[0], fmod_ref, post_ref)


def _const_spec(shape, single=False):
    nd = len(shape)
    if single:
        return pl.BlockSpec(shape, lambda *_: (0,) * nd, pipeline_mode=pl.Buffered(1))
    return pl.BlockSpec(shape, lambda *_: (0,) * nd)


def _ffn_specs(d):
    return [_const_spec((1, d)), _const_spec((1, d)),
            _const_spec((d, 2 * D_FF), single=True), _const_spec((D_FF, d), single=True)]


def _layer0(x, pos, mod, mod_row, h0, p, tc):
    bsz, t_len, d = x.shape
    nc = t_len // tc
    n_chunks = bsz * nc
    n_hblk = t_len // HALO
    has_pos = pos is not None
    act = jax.ShapeDtypeStruct((bsz, t_len, d), F32)
    st = jax.ShapeDtypeStruct((bsz, 1, d), F32)

    tcb = tc * min(FWD_CHUNKS, nc)
    hb8 = tcb // HALO
    prev_blk = lambda c: jnp.maximum(c * hb8 - 1, 0)
    next_blk = lambda c: jnp.minimum((c + 1) * hb8, n_hblk - 1)
    row_spec = pl.BlockSpec((1, d), lambda b, c: (0, 0))
    chunk = pl.BlockSpec((1, tcb, d), lambda b, c: (b, c, 0))
    state_spec = pl.BlockSpec((1, 1, d), lambda b, c: (b, 0, 0))
    pos_specs = [_const_spec(pos.shape)] if has_pos else []
    in_specs = ([chunk,
                 pl.BlockSpec((1, HALO, d), lambda b, c: (b, prev_blk(c), 0)),
                 pl.BlockSpec((1, HALO, d), lambda b, c: (b, next_blk(c), 0))]
                + pos_specs
                + [pl.BlockSpec((1, N_MOD, d), lambda b, c: (mod_row(b), 0, 0)), row_spec,
                   _const_spec((d, 2 * D_RNN)), _const_spec((CONV_W, d)), row_spec,
                   _const_spec((N_LRU_BLOCKS, LRU_BLOCK, 2 * LRU_BLOCK)), _const_spec((2, d)), row_spec, state_spec])
    args = ([x, x, x] + ([pos] if has_pos else [])
            + [mod, p['mix_pre_g'], p['w_in'], p['conv_w_half'], p['conv_b_half'], p['wg'][0], p['bg_half'][0],
               p['lam'][0], h0[:, 0:1]])
    zg, hu, hf, sf = pl.pallas_call(
        functools.partial(_fwd_body, nc=t_len // tcb, tc=tc, has_pos=has_pos),
        grid=(bsz, t_len // tcb), in_specs=in_specs, out_specs=[chunk, chunk, chunk, state_spec],
        out_shape=[act, act, act, st],
        scratch_shapes=[pltpu.VMEM((SUBLANES, d), F32),
                        pltpu.VMEM((N_SLABS, HALO + SUBLANES * _pitch(tc), LANES), F32)],
        compiler_params=pltpu.CompilerParams(
            dimension_semantics=("parallel", "arbitrary"), vmem_limit_bytes=VMEM_LIMIT),
        name="l0_fwd",
    )(*args)

    n_sub = min(BWD_CHUNKS, nc)
    tcb = tc * n_sub
    nb = t_len // tcb
    n_blocks = bsz * nb

    def mix_at(i):
        im = jnp.minimum(i, n_blocks - 1)
        return im // nb, nb - 1 - im % nb

    def ffn_at(i):
        return mix_at(jnp.maximum(i - 1, 0))

    mchunk = pl.BlockSpec((1, tcb, d), lambda i: (*mix_at(i), 0))
    in_specs = ([mchunk, mchunk, mchunk, mchunk]
                + pos_specs
                + [pl.BlockSpec((1, N_MOD, d), lambda i: (mod_row(mix_at(i)[0]), 0, 0)),
                   _const_spec((N_LRU_BLOCKS, LRU_BLOCK, 2 * LRU_BLOCK)), _const_spec((2, d)), _const_spec((1, d)),
                   pl.BlockSpec((1, 1, d), lambda i: (mix_at(i)[0], 0, 0)),
                   _const_spec((D_RNN, d)), _const_spec((1, d)),
                   pl.BlockSpec((1, N_MOD, d), lambda i: (mod_row(ffn_at(i)[0]), 0, 0))]
                + _ffn_specs(d))
    args = ([hu, zg, hf, x] + ([pos] if has_pos else [])
            + [mod, p['wg'][1], p['bg_half'][1], p['lam'][1], h0[:, 1:2], p['w_out'], p['mix_post_g'],
               mod, p['ffn_pre_g'], p['ffn_post_g'], p['ffn_w_in'], p['ffn_w_out']])
    x2, sb = pl.pallas_call(
        functools.partial(_bwd_ffn_body, nc=nb, tc=tc, n_chunks=n_blocks, has_pos=has_pos),
        grid=(n_blocks + 1,), in_specs=in_specs,
        out_specs=[pl.BlockSpec((1, tcb, d), lambda i: (*ffn_at(i), 0)),
                   pl.BlockSpec((1, 1, d), lambda i: (mix_at(i)[0], 0, 0))],
        out_shape=[act, st],
        scratch_shapes=[pltpu.VMEM((SUBLANES, d), F32),
                        pltpu.VMEM((n_sub, N_SLABS, SUBLANES * _pitch(tc), LANES), F32),
                        pltpu.VMEM((2, tcb, d), F32), pltpu.VMEM((tcb, D_FF), BF16), pltpu.VMEM((tcb, D_RNN), BF16)],
        compiler_params=pltpu.CompilerParams(dimension_semantics=("arbitrary",), vmem_limit_bytes=VMEM_LIMIT_BIG),
        name="l0_bwd_ffn",
    )(*args)
    return x2, jnp.concatenate([sf, sb], axis=1)


def _layer1(x, mod, mod_row, p, tc):
    bsz, t_len, d = x.shape
    nc = t_len // tc
    n_chunks = bsz * nc
    hb8 = tc // HALO
    n_hblk = t_len // HALO

    def mix_at(i):
        im = jnp.minimum(i, n_chunks - 1)
        return im // nc, im % nc

    def ffn_at(i):
        return mix_at(jnp.maximum(i - 1, 0))

    def halo_spec(blk_of):
        return pl.BlockSpec((1, HALO, d), lambda i: (mix_at(i)[0], blk_of(mix_at(i)[1]), 0))

    return pl.pallas_call(
        functools.partial(_pool_ffn_body, nc=nc, t_len=t_len, n_chunks=n_chunks),
        grid=(n_chunks + 1,),
        in_specs=[pl.BlockSpec((1, tc, d), lambda i: (*mix_at(i), 0)),
                  halo_spec(lambda c: jnp.maximum(c * hb8 - 1, 0)),
                  halo_spec(lambda c: jnp.minimum((c + 1) * hb8, n_hblk - 1)),
                  pl.BlockSpec((1, N_MOD, d), lambda i: (mod_row(mix_at(i)[0]), 0, 0)),
                  _const_spec((1, d)), _const_spec((1, d)),
                  _const_spec((len(POOL_WINDOWS), POOL_GROUP, POOL_GROUP)), _const_spec((1, d)), _const_spec((1, d)),
                  pl.BlockSpec((1, N_MOD, d), lambda i: (mod_row(ffn_at(i)[0]), 0, 0))] + _ffn_specs(d),
        out_specs=pl.BlockSpec((1, tc, d), lambda i: (*ffn_at(i), 0)),
        out_shape=jax.ShapeDtypeStruct((bsz, t_len, d), F32),
        scratch_shapes=[pltpu.VMEM((2, tc, d), F32), pltpu.VMEM((tc, D_FF), BF16)],
        compiler_params=pltpu.CompilerParams(dimension_semantics=("arbitrary",), vmem_limit_bytes=VMEM_LIMIT),
        name="l1_ffn",
    )(x, x, x, mod, p['mix_pre_g'], p['mix_post_g'], p['pool_w'], p['pool_b'], p['pool_scale'],
      mod, p['ffn_pre_g'], p['ffn_post_g'], p['ffn_w_in'], p['ffn_w_out'])


def _grid_pos_table(t_len):
    rows = t_len // GRID_W
    quarter = D_MODEL // 4
    omega = 1.0 / (POS_THETA ** (jnp.arange(quarter, dtype=F32) / quarter))
    ang = jnp.arange(max(rows, GRID_W), dtype=F32)[:, None] * omega[None, :]
    return jnp.concatenate([jnp.sin(ang), jnp.cos(ang)], axis=-1)


def _row(v):
    return v.reshape(1, -1)


def kernel(x_prompt, x_sample, state_l0_rglru, c, c_ctx, l0_mod_w, l0_mod_b, l0_mix_pre_g, l0_mix_post_g, l0_w_in, l0_conv_w, l0_conv_b, l0_gate_a_w, l0_gate_a_b, l0_gate_x_w, l0_gate_x_b, l0_lambda, l0_w_out, l0_ffn_pre_g, l0_ffn_post_g, l0_ffn_w_in, l0_ffn_w_out, l1_mod_w, l1_mod_b, l1_mix_pre_g, l1_mix_post_g, l1_pool_w, l1_pool_b, l1_pool_scale, l1_ffn_pre_g, l1_ffn_post_g, l1_ffn_w_in, l1_ffn_w_out):
    n_ctx, t_ctx, d = x_prompt.shape
    n_lat, t_lat, _ = x_sample.shape

    cond = jnp.concatenate(
        [c, c_ctx[None, :], jnp.zeros((MOD_ROWS - n_lat - 1, d), F32)], axis=0)
    mod0 = _modulation(cond, l0_mod_w, l0_mod_b)
    mod1 = _modulation(cond, l1_mod_w, l1_mod_b)

    p0 = dict(
        mix_pre_g=_row(l0_mix_pre_g), mix_post_g=_row(l0_mix_post_g), w_in=l0_w_in.astype(BF16),
        conv_w_half=0.5 * l0_conv_w, conv_b_half=_row(0.5 * l0_conv_b),
        wg=[jnp.concatenate([l0_gate_a_w[k], l0_gate_x_w[k]], axis=-1).astype(BF16) for k in range(2)],
        bg_half=[0.5 * jnp.stack([l0_gate_a_b[k], l0_gate_x_b[k]], axis=0) for k in range(2)],
        lam=[_row(l0_lambda[k]) for k in range(2)],
        w_out=l0_w_out.astype(BF16),
        ffn_pre_g=_row(l0_ffn_pre_g), ffn_post_g=_row(l0_ffn_post_g),
        ffn_w_in=l0_ffn_w_in.astype(BF16), ffn_w_out=l0_ffn_w_out.astype(BF16))
    p1 = dict(
        mix_pre_g=_row(l1_mix_pre_g), mix_post_g=_row(l1_mix_post_g), pool_w=l1_pool_w.astype(BF16),
        pool_b=_row(l1_pool_b), pool_scale=_row(l1_pool_scale),
        ffn_pre_g=_row(l1_ffn_pre_g), ffn_post_g=_row(l1_ffn_post_g),
        ffn_w_in=l1_ffn_w_in.astype(BF16), ffn_w_out=l1_ffn_w_out.astype(BF16))

    def run(x, pos, mod_row, h0):
        t_len = x.shape[1]
        x2, state = _layer0(x, pos, mod0, mod_row, h0, p0, min(CHUNK_L0, t_len))
        return _layer1(x2, mod1, mod_row, p1, min(CHUNK_L1, t_len)), state

    y_prompt, new_state = run(x_prompt, None, lambda b: CTX_ROW, jnp.zeros((n_ctx, 2, D_RNN), F32))
    y_sample, _ = run(x_sample, _grid_pos_table(t_lat), lambda b: b, state_l0_rglru)
    return y_prompt, y_sample, new_state
```

```python
import functools

import jax
import jax.numpy as jnp
from jax import lax
from jax.experimental import pallas as pl
from jax.experimental.pallas import tpu as pltpu

D_MODEL = 1024
D_RNN = D_MODEL
N_LRU_BLOCKS = 4
LRU_BLOCK = D_RNN // N_LRU_BLOCKS
CONV_W = 4
LRU_C = 8.0
POOL_WINDOWS = (2, 4, 8, 16)
POOL_GROUP = D_MODEL // len(POOL_WINDOWS)
D_FF = 2816
N_MOD = 6
EPS = 1e-6
POS_THETA = 10000.0
GRID_W = 64

LANES = 128
SUBLANES = 8
HALO = SUBLANES
N_SLABS = D_RNN // LANES
FF_CHUNK = 256
N_FF_CHUNKS = D_FF // FF_CHUNK
MOD_ROWS = 16
CTX_ROW = 8
MOD_TN = 1536
SQRT_FLOOR = 1e-36
CHUNK_L0 = 256
CHUNK_L1 = 512
FWD_CHUNKS = 2
BWD_CHUNKS = 2

VMEM_LIMIT = 56 * 1024 * 1024
VMEM_LIMIT_BIG = 62 * 1024 * 1024

F32 = jnp.float32
BF16 = jnp.bfloat16


def _dot(a, b):
    return jnp.dot(a, b, preferred_element_type=F32)


def _rms(x):
    return x * lax.rsqrt(jnp.mean(x * x, axis=-1, keepdims=True) + EPS)


def _sigmoid(x):
    return 1.0 / (1.0 + jnp.exp(-x))


def _mod_body(cond_ref, w_ref, b_ref, o_ref):
    c = cond_ref[...]
    s = (c * _sigmoid(c)).astype(BF16)
    o_ref[...] = _dot(s, w_ref[...].astype(BF16)) + b_ref[...]


def _modulation(cond, w, b):
    n = N_MOD * D_MODEL
    out = pl.pallas_call(
        _mod_body,
        grid=(n // MOD_TN,),
        in_specs=[
            pl.BlockSpec((MOD_ROWS, D_MODEL), lambda j: (0, 0)),
            pl.BlockSpec((D_MODEL, MOD_TN), lambda j: (0, j)),
            pl.BlockSpec((1, MOD_TN), lambda j: (0, j)),
        ],
        out_specs=pl.BlockSpec((MOD_ROWS, MOD_TN), lambda j: (0, j)),
        out_shape=jax.ShapeDtypeStruct((MOD_ROWS, n), F32),
        compiler_params=pltpu.CompilerParams(
            dimension_semantics=("parallel",), vmem_limit_bytes=VMEM_LIMIT),
        name="modulation",
    )(cond, w, b.reshape(1, n))
    return out.reshape(MOD_ROWS, N_MOD, D_MODEL)


def _gate_ab(hu, n, wg_ref, bg_ref, lam_ref):
    cols = slice(n * LRU_BLOCK, (n + 1) * LRU_BLOCK)
    o = _dot(hu.astype(BF16), wg_ref[n])
    t_r = jnp.tanh(o[:, :LRU_BLOCK] + bg_ref[0:1, cols])
    t_i = jnp.tanh(o[:, LRU_BLOCK:] + bg_ref[1:2, cols])
    nl = -lam_ref[:, cols]
    softplus = jnp.maximum(nl, 0.0) + jnp.log1p(jnp.exp(-jnp.abs(nl)))
    log_a = (t_r + 1.0) * ((-0.5 * LRU_C) * softplus)
    a = jnp.exp(log_a)
    s = jnp.tanh(log_a) * (-1.0 - a * a)
    root = s * lax.rsqrt(jnp.maximum(s, SQRT_FLOOR))
    b = root * (hu + hu * t_i)
    return a, b


def _scan(a, b, h_in, reverse):
    tc = a.shape[0]
    sub = tc // SUBLANES
    sub_id = lax.broadcasted_iota(jnp.int32, (SUBLANES, LANES), 0)
    steps = range(sub - 1, -1, -1) if reverse else range(sub)
    order = range(SUBLANES - 1, -1, -1) if reverse else range(SUBLANES)
    n_slabs = a.shape[1] // LANES
    hs = [[None] * sub for _ in range(n_slabs)]
    ps = [[None] * sub for _ in range(n_slabs)]
    h, p = [None] * n_slabs, [None] * n_slabs
    for j in steps:
        for k in range(n_slabs):
            av = a[j * SUBLANES:(j + 1) * SUBLANES, k * LANES:(k + 1) * LANES]
            bv = b[j * SUBLANES:(j + 1) * SUBLANES, k * LANES:(k + 1) * LANES]
            h[k] = bv if h[k] is None else av * h[k] + bv
            p[k] = av if p[k] is None else av * p[k]
            hs[k][j], ps[k][j] = h[k], p[k]
    slabs, last = [], []
    for k in range(n_slabs):
        carry = h_in[:, k * LANES:(k + 1) * LANES]
        carry_in = jnp.zeros((SUBLANES, LANES), F32)
        for s in order:
            carry_in = jnp.where(sub_id == s, carry, carry_in)
            carry = p[k][s:s + 1, :] * carry + h[k][s:s + 1, :]
        slabs.append(jnp.concatenate([ps[k][j] * carry_in + hs[k][j] for j in range(sub)], axis=0))
        last.append(carry)
    return jnp.concatenate(slabs, axis=1), jnp.concatenate(last, axis=1)


def _pitch(tc):
    return tc // SUBLANES + SUBLANES


def _permute_in(e, ext_s, tc, n_tiles):
    sub = tc // SUBLANES
    pitch = _pitch(tc)
    row = lax.broadcasted_iota(jnp.int32, (SUBLANES, LANES), 0)
    for k in range(N_SLABS):
        lanes = slice(k * LANES, (k + 1) * LANES)
        ext_s[k, 0:HALO, :] = e[0:HALO, lanes]
        for s in range(SUBLANES):
            base = HALO + s * pitch
            end = HALO + (s + 1) * sub
            ext_s[k, base:base + sub, :] = e[end - sub:end, lanes]
            ext_s[k, base + sub:base + pitch, :] = jnp.where(
                row < SUBLANES // 2, e[end:end + SUBLANES, lanes], e[end - SUBLANES:end, lanes])
    tiles = [jnp.concatenate([ext_s[k, pl.ds(HALO - 2 + q, SUBLANES, stride=pitch), :] for k in range(N_SLABS)],
                             axis=1) for q in range(n_tiles)]
    return jnp.concatenate(tiles, axis=0)


def _permute_out(r, o_s, tc):
    sub = tc // SUBLANES
    pitch = _pitch(tc)
    for k in range(N_SLABS):
        for j in range(sub):
            o_s[k, pl.ds(j, SUBLANES, stride=pitch), :] = r[j * SUBLANES:(j + 1) * SUBLANES, k * LANES:(k + 1) * LANES]
    return jnp.concatenate(
        [jnp.concatenate([o_s[k, s * pitch:s * pitch + sub, :] for s in range(SUBLANES)], axis=0)
         for k in range(N_SLABS)], axis=1)


def _pos_rows(ptab_ref, grid_row, col0, n):
    half = ptab_ref.shape[1]
    by_row = jnp.broadcast_to(ptab_ref[pl.ds(grid_row, 1), :], (n, half))
    return jnp.concatenate([by_row, ptab_ref[col0:col0 + n, :]], axis=1)


def _pos_block(ptab_ref, first_grid_row, n_rows):
    return jnp.concatenate([_pos_rows(ptab_ref, first_grid_row + k, 0, GRID_W) for k in range(n_rows // GRID_W)],
                           axis=0)


def _fwd_body(*refs, nc, tc, has_pos):
    if has_pos:
        (x_ref, xp_ref, xn_ref, ptab_ref, mod_ref, g_ref, w_ref, cw_ref, cb_ref, wg_ref, bg_ref,
         lam_ref, h0_ref, zg_ref, hu_ref, hf_ref, sf_ref, carry_ref, ext_s) = refs
    else:
        (x_ref, xp_ref, xn_ref, mod_ref, g_ref, w_ref, cw_ref, cb_ref, wg_ref, bg_ref,
         lam_ref, h0_ref, zg_ref, hu_ref, hf_ref, sf_ref, carry_ref, ext_s) = refs
    carry_ref = carry_ref.at[0:1]
    c = pl.program_id(1)
    n_sub = x_ref.shape[1] // tc
    sub = tc // SUBLANES
    n_tiles = sub + CONV_W

    @pl.when(c == 0)
    def _():
        carry_ref[...] = h0_ref[0]

    x, xp, xn = x_ref[0], xp_ref[0], xn_ref[0]
    if has_pos:
        rows_per_block = x.shape[0] // GRID_W
        n_grid_rows = nc * rows_per_block
        r0 = c * rows_per_block
        x = x + _pos_block(ptab_ref, r0, x.shape[0])
        xp = xp + _pos_rows(ptab_ref, jnp.maximum(r0 - 1, 0), GRID_W - HALO, HALO)
        xn = xn + _pos_rows(ptab_ref, jnp.minimum(r0 + rows_per_block, n_grid_rows - 1), 0, HALO)
    gs = g_ref[...] * (1.0 + mod_ref[0, 1:2, :])
    sh = mod_ref[0, 0:1, :]
    h_all = jnp.concatenate([_rms(xp) * gs + sh, _rms(x) * gs + sh, _rms(xn) * gs + sh], axis=0)
    row = lax.broadcasted_iota(jnp.int32, (2 * SUBLANES, D_RNN), 0) & (SUBLANES - 1)
    carry = carry_ref[...]
    for q in range(n_sub):
        rows = slice(q * tc, (q + 1) * tc)
        e = h_all[q * tc:(q + 1) * tc + 2 * HALO]
        hp = _permute_in(e, ext_s, tc, n_tiles).astype(BF16)
        zg_ref[0, rows, :] = _dot(hp[2 * SUBLANES:2 * SUBLANES + tc], w_ref[:, :D_RNN])
        rec = _dot(hp, w_ref[:, D_RNN:])
        if q == 0:
            head = jnp.where(row < jnp.where(c == 0, 1, 0), 0.0, rec[0:2 * SUBLANES])
            rec = jnp.concatenate([head, rec[2 * SUBLANES:]], axis=0)
        if q == n_sub - 1:
            tail = jnp.where(row > jnp.where(c == nc - 1, SUBLANES - 2, SUBLANES - 1), 0.0,
                             rec[(sub + 2) * SUBLANES:])
            rec = jnp.concatenate([rec[:(sub + 2) * SUBLANES], tail], axis=0)
        hu = rec[0:tc] * cw_ref[0:1, :]
        for k in range(1, CONV_W):
            hu = hu + rec[k * SUBLANES:k * SUBLANES + tc] * cw_ref[k:k + 1, :]
        hu = hu + cb_ref[...]
        hu_ref[0, rows, :] = hu
        ab = [_gate_ab(hu[:, n * LRU_BLOCK:(n + 1) * LRU_BLOCK], n, wg_ref, bg_ref, lam_ref)
              for n in range(N_LRU_BLOCKS)]
        hf, carry = _scan(jnp.concatenate([a for a, _ in ab], axis=1),
                          jnp.concatenate([b for _, b in ab], axis=1), carry, False)
        hf_ref[0, rows, :] = hf
    carry_ref[...] = carry
    sf_ref[0] = carry


def _ffn_head(x, mod_ref, pre_ref):
    sh = mod_ref[0, 3:4, :]
    sc = mod_ref[0, 4:5, :]
    return (_rms(x) * (pre_ref[...] * (1.0 + sc)) + sh).astype(BF16)


def _ffn_steps(h, wi_ref, wo_ref, act_ref, out):
    def chunk(n):
        def f():
            g = _dot(h, wi_ref[:, n * FF_CHUNK:(n + 1) * FF_CHUNK])
            v = _dot(h, wi_ref[:, D_FF + n * FF_CHUNK:D_FF + (n + 1) * FF_CHUNK])
            hg = 0.5 * g
            act_ref[:, n * FF_CHUNK:(n + 1) * FF_CHUNK] = ((hg + hg * jnp.tanh(hg)) * v).astype(BF16)
        return f

    def down():
        out.append(_dot(act_ref[...], wo_ref[...]))

    return [chunk(n) for n in range(N_FF_CHUNKS)] + [down]


def _ffn_tail(x, y, mod_ref, post_ref):
    return x + _rms(y) * (post_ref[...] * mod_ref[0, 5:6, :])


def _trace_interleaved(major, minor):
    done = 0
    for k, step in enumerate(major):
        step()
        while done < (k + 1) * len(minor) // len(major):
            minor[done]()
            done += 1


def _bwd_ffn_body(*refs, nc, tc, n_chunks, has_pos):
    if has_pos:
        (hu_ref, zg_ref, hf_ref, x_ref, ptab_ref, mod_ref, wg_ref, bg_ref, lam_ref, h0_ref, wo_ref, pg_ref,
         fmod_ref, pre_ref, post_ref, wi_ref, wo2_ref, x2_ref, sb_ref,
         carry_ref, o_s, x1_s, act_ref, mixed_ref) = refs
    else:
        (hu_ref, zg_ref, hf_ref, x_ref, mod_ref, wg_ref, bg_ref, lam_ref, h0_ref, wo_ref, pg_ref,
         fmod_ref, pre_ref, post_ref, wi_ref, wo2_ref, x2_ref, sb_ref,
         carry_ref, o_s, x1_s, act_ref, mixed_ref) = refs
    carry_ref = carry_ref.at[0:1]
    i = pl.program_id(0)
    c = jnp.minimum(i, n_chunks - 1) % nc
    n_sub = x_ref.shape[1] // tc

    @pl.when(c == 0)
    def _():
        carry_ref[...] = h0_ref[0]

    def mixer_steps():
        st = {}
        lasts = [None] * N_LRU_BLOCKS

        def block(n, q):
            def f():
                rows = slice(q * tc, (q + 1) * tc)
                cols = slice(n * LRU_BLOCK, (n + 1) * LRU_BLOCK)
                a, b = _gate_ab(hu_ref[0, rows, cols], n, wg_ref, bg_ref, lam_ref)
                h_in = carry_ref[:, cols] if lasts[n] is None else lasts[n]
                hb, lasts[n] = _scan(a, b, h_in, True)
                mixed_ref[rows, cols] = (
                    (hf_ref[0, rows, cols] + hb) * jax.nn.gelu(zg_ref[0, rows, cols])).astype(BF16)
            return f

        def out_proj():
            st['y'] = _dot(mixed_ref[...], wo_ref[...])

        def residual():
            x = x_ref[0]
            if has_pos:
                x = x + _pos_block(ptab_ref, (nc - 1 - c) * (x.shape[0] // GRID_W), x.shape[0])
            r = _rms(st['y']) * (pg_ref[...] * mod_ref[0, 2:3, :])
            x1_s[i % 2] = x + jnp.concatenate(
                [_permute_out(r[q * tc:(q + 1) * tc], o_s.at[q], tc) for q in range(n_sub)], axis=0)
            h_last = jnp.concatenate(lasts, axis=1)
            carry_ref[...] = h_last
            sb_ref[0] = h_last

        return ([block(n, q) for q in range(n_sub - 1, -1, -1) for n in range(N_LRU_BLOCKS)]
                + [out_proj, residual])

    def ffn_with(mixer):
        xf = x1_s[(i + 1) % 2]
        yf = []
        _trace_interleaved(_ffn_steps(_ffn_head(xf, fmod_ref, pre_ref), wi_ref, wo2_ref, act_ref, yf), mixer)
        x2_ref[0] = _ffn_tail(xf, yf[0], fmod_ref, post_ref)

    @pl.when(i == 0)
    def _():
        for step in mixer_steps():
            step()

    @pl.when(jnp.logical_and(i > 0, i < n_chunks))
    def _():
        ffn_with(mixer_steps())

    @pl.when(i == n_chunks)
    def _():
        ffn_with([])


def _pool_group(ext, gi, c, pw_ref, t_len):
    n = ext.shape[0]
    tc = n - 2 * HALO
    win = POOL_WINDOWS[gi]
    e = ext[:, gi * POOL_GROUP:(gi + 1) * POOL_GROUP]
    w = e + pltpu.roll(e, 1, 0)
    half = 1
    while 2 * half < win:
        w = pltpu.roll(w, half, 0) + pltpu.roll(w, n - half, 0)
        half *= 2
    t = c * tc + lax.broadcasted_iota(jnp.int32, (tc, POOL_GROUP), 0)
    cnt = (jnp.minimum(t + win // 2, t_len) - jnp.maximum(t - win // 2, 0)).astype(F32)
    pooled = w[HALO:n - HALO] / cnt - e[HALO:n - HALO]
    return _dot(pooled.astype(BF16), pw_ref[gi])


def _pool_ffn_body(x_ref, xp_ref, xn_ref, mod_ref, mpre_ref, mpost_ref, pw_ref, pb_ref, ps_ref,
                   fmod_ref, pre_ref, post_ref, wi_ref, wo_ref, o_ref, x3_s, act_ref, *, nc, t_len, n_chunks):
    i = pl.program_id(0)
    c = jnp.minimum(i, n_chunks - 1) % nc

    def mixer_steps():
        st = {}
        ys = [None] * len(POOL_WINDOWS)

        def pre_norm():
            gs = mpre_ref[...] * (1.0 + mod_ref[0, 1:2, :])
            sh = mod_ref[0, 0:1, :]
            st['ext'] = jnp.concatenate([jnp.where(c > 0, _rms(xp_ref[0]) * gs + sh, 0.0),
                                         _rms(x_ref[0]) * gs + sh,
                                         jnp.where(c < nc - 1, _rms(xn_ref[0]) * gs + sh, 0.0)], axis=0)

        def group(gi):
            def f():
                ys[gi] = _pool_group(st['ext'], gi, c, pw_ref, t_len)
            return f

        def residual():
            y = (jnp.concatenate(ys, axis=1) + pb_ref[...]) * ps_ref[...]
            x3_s[i % 2] = x_ref[0] + _rms(y) * (mpost_ref[...] * mod_ref[0, 2:3, :])

        return [pre_norm] + [group(gi) for gi in range(len(POOL_WINDOWS))] + [residual]

    def ffn_with(mixer):
        xf = x3_s[(i + 1) % 2]
        yf = []
        _trace_interleaved(_ffn_steps(_ffn_head(xf, fmod_ref, pre_ref), wi_ref, wo_ref, act_ref, yf), mixer)
        o_ref[0] = _ffn_tail(xf, yf[0], fmod_ref, post_ref)

    @pl.when(i == 0)
    def _():
        for step in mixer_steps():
            step()

    @pl.when(jnp.logical_and(i > 0, i < n_chunks))
    def _():
        ffn_with(mixer_steps())

    @pl.when(i == n_chunks)
    def _():
        ffn_with([])


def _const_spec(shape, single=False):
    nd = len(shape)
    if single:
        return pl.BlockSpec(shape, lambda *_: (0,) * nd, pipeline_mode=pl.Buffered(1))
    return pl.BlockSpec(shape, lambda *_: (0,) * nd)


def _ffn_specs(d):
    return [_const_spec((1, d)), _const_spec((1, d)),
            _const_spec((d, 2 * D_FF), single=True), _const_spec((D_FF, d), single=True)]


def _layer0(x, pos, mod, mod_row, h0, p, tc):
    bsz, t_len, d = x.shape
    nc = t_len // tc
    n_chunks = bsz * nc
    n_hblk = t_len // HALO
    has_pos = pos is not None
    act = jax.ShapeDtypeStruct((bsz, t_len, d), F32)
    st = jax.ShapeDtypeStruct((bsz, 1, d), F32)

    tcb = tc * min(FWD_CHUNKS, nc)
    hb8 = tcb // HALO
    prev_blk = lambda c: jnp.maximum(c * hb8 - 1, 0)
    next_blk = lambda c: jnp.minimum((c + 1) * hb8, n_hblk - 1)
    row_spec = pl.BlockSpec((1, d), lambda b, c: (0, 0))
    chunk = pl.BlockSpec((1, tcb, d), lambda b, c: (b, c, 0))
    state_spec = pl.BlockSpec((1, 1, d), lambda b, c: (b, 0, 0))
    pos_specs = [_const_spec(pos.shape)] if has_pos else []
    in_specs = ([chunk,
                 pl.BlockSpec((1, HALO, d), lambda b, c: (b, prev_blk(c), 0)),
                 pl.BlockSpec((1, HALO, d), lambda b, c: (b, next_blk(c), 0))]
                + pos_specs
                + [pl.BlockSpec((1, N_MOD, d), lambda b, c: (mod_row(b), 0, 0)), row_spec,
                   _const_spec((d, 2 * D_RNN)), _const_spec((CONV_W, d)), row_spec,
                   _const_spec((N_LRU_BLOCKS, LRU_BLOCK, 2 * LRU_BLOCK)), _const_spec((2, d)), row_spec, state_spec])
    args = ([x, x, x] + ([pos] if has_pos else [])
            + [mod, p['mix_pre_g'], p['w_in'], p['conv_w_half'], p['conv_b_half'], p['wg'][0], p['bg_half'][0],
               p['lam'][0], h0[:, 0:1]])
    zg, hu, hf, sf = pl.pallas_call(
        functools.partial(_fwd_body, nc=t_len // tcb, tc=tc, has_pos=has_pos),
        grid=(bsz, t_len // tcb), in_specs=in_specs, out_specs=[chunk, chunk, chunk, state_spec],
        out_shape=[act, act, act, st],
        scratch_shapes=[pltpu.VMEM((SUBLANES, d), F32),
                        pltpu.VMEM((N_SLABS, HALO + SUBLANES * _pitch(tc), LANES), F32)],
        compiler_params=pltpu.CompilerParams(
            dimension_semantics=("parallel", "arbitrary"), vmem_limit_bytes=VMEM_LIMIT),
        name="l0_fwd",
    )(*args)

    n_sub = min(BWD_CHUNKS, nc)
    tcb = tc * n_sub
    nb = t_len // tcb
    n_blocks = bsz * nb

    def mix_at(i):
        im = jnp.minimum(i, n_blocks - 1)
        return im // nb, nb - 1 - im % nb

    def ffn_at(i):
        return mix_at(jnp.maximum(i - 1, 0))

    mchunk = pl.BlockSpec((1, tcb, d), lambda i: (*mix_at(i), 0))
    in_specs = ([mchunk, mchunk, mchunk, mchunk]
                + pos_specs
                + [pl.BlockSpec((1, N_MOD, d), lambda i: (mod_row(mix_at(i)[0]), 0, 0)),
                   _const_spec((N_LRU_BLOCKS, LRU_BLOCK, 2 * LRU_BLOCK)), _const_spec((2, d)), _const_spec((1, d)),
                   pl.BlockSpec((1, 1, d), lambda i: (mix_at(i)[0], 0, 0)),
                   _const_spec((D_RNN, d)), _const_spec((1, d)),
                   pl.BlockSpec((1, N_MOD, d), lambda i: (mod_row(ffn_at(i)[0]), 0, 0))]
                + _ffn_specs(d))
    args = ([hu, zg, hf, x] + ([pos] if has_pos else [])
            + [mod, p['wg'][1], p['bg_half'][1], p['lam'][1], h0[:, 1:2], p['w_out'], p['mix_post_g'],
               mod, p['ffn_pre_g'], p['ffn_post_g'], p['ffn_w_in'], p['ffn_w_out']])
    x2, sb = pl.pallas_call(
        functools.partial(_bwd_ffn_body, nc=nb, tc=tc, n_chunks=n_blocks, has_pos=has_pos),
        grid=(n_blocks + 1,), in_specs=in_specs,
        out_specs=[pl.BlockSpec((1, tcb, d), lambda i: (*ffn_at(i), 0)),
                   pl.BlockSpec((1, 1, d), lambda i: (mix_at(i)[0], 0, 0))],
        out_shape=[act, st],
        scratch_shapes=[pltpu.VMEM((SUBLANES, d), F32),
                        pltpu.VMEM((n_sub, N_SLABS, SUBLANES * _pitch(tc), LANES), F32),
                        pltpu.VMEM((2, tcb, d), F32), pltpu.VMEM((tcb, D_FF), BF16), pltpu.VMEM((tcb, D_RNN), BF16)],
        compiler_params=pltpu.CompilerParams(dimension_semantics=("arbitrary",), vmem_limit_bytes=VMEM_LIMIT_BIG),
        name="l0_bwd_ffn",
    )(*args)
    return x2, jnp.concatenate([sf, sb], axis=1)


def _layer1(x, mod, mod_row, p, tc):
    bsz, t_len, d = x.shape
    nc = t_len // tc
    n_chunks = bsz * nc
    hb8 = tc // HALO
    n_hblk = t_len // HALO

    def mix_at(i):
        im = jnp.minimum(i, n_chunks - 1)
        return im // nc, im % nc

    def ffn_at(i):
        return mix_at(jnp.maximum(i - 1, 0))

    def halo_spec(blk_of):
        return pl.BlockSpec((1, HALO, d), lambda i: (mix_at(i)[0], blk_of(mix_at(i)[1]), 0))

    return pl.pallas_call(
        functools.partial(_pool_ffn_body, nc=nc, t_len=t_len, n_chunks=n_chunks),
        grid=(n_chunks + 1,),
        in_specs=[pl.BlockSpec((1, tc, d), lambda i: (*mix_at(i), 0)),
                  halo_spec(lambda c: jnp.maximum(c * hb8 - 1, 0)),
                  halo_spec(lambda c: jnp.minimum((c + 1) * hb8, n_hblk - 1)),
                  pl.BlockSpec((1, N_MOD, d), lambda i: (mod_row(mix_at(i)[0]), 0, 0)),
                  _const_spec((1, d)), _const_spec((1, d)),
                  _const_spec((len(POOL_WINDOWS), POOL_GROUP, POOL_GROUP)), _const_spec((1, d)), _const_spec((1, d)),
                  pl.BlockSpec((1, N_MOD, d), lambda i: (mod_row(ffn_at(i)[0]), 0, 0))] + _ffn_specs(d),
        out_specs=pl.BlockSpec((1, tc, d), lambda i: (*ffn_at(i), 0)),
        out_shape=jax.ShapeDtypeStruct((bsz, t_len, d), F32),
        scratch_shapes=[pltpu.VMEM((2, tc, d), F32), pltpu.VMEM((tc, D_FF), BF16)],
        compiler_params=pltpu.CompilerParams(dimension_semantics=("arbitrary",), vmem_limit_bytes=VMEM_LIMIT),
        name="l1_ffn",
    )(x, x, x, mod, p['mix_pre_g'], p['mix_post_g'], p['pool_w'], p['pool_b'], p['pool_scale'],
      mod, p['ffn_pre_g'], p['ffn_post_g'], p['ffn_w_in'], p['ffn_w_out'])


def _grid_pos_table(t_len):
    rows = t_len // GRID_W
    quarter = D_MODEL // 4
    omega = 1.0 / (POS_THETA ** (jnp.arange(quarter, dtype=F32) / quarter))
    ang = jnp.arange(max(rows, GRID_W), dtype=F32)[:, None] * omega[None, :]
    return jnp.concatenate([jnp.sin(ang), jnp.cos(ang)], axis=-1)


def _row(v):
    return v.reshape(1, -1)


def kernel(x_prompt, x_sample, state_l0_rglru, c, c_ctx, l0_mod_w, l0_mod_b, l0_mix_pre_g, l0_mix_post_g, l0_w_in, l0_conv_w, l0_conv_b, l0_gate_a_w, l0_gate_a_b, l0_gate_x_w, l0_gate_x_b, l0_lambda, l0_w_out, l0_ffn_pre_g, l0_ffn_post_g, l0_ffn_w_in, l0_ffn_w_out, l1_mod_w, l1_mod_b, l1_mix_pre_g, l1_mix_post_g, l1_pool_w, l1_pool_b, l1_pool_scale, l1_ffn_pre_g, l1_ffn_post_g, l1_ffn_w_in, l1_ffn_w_out):
    n_ctx, t_ctx, d = x_prompt.shape
    n_lat, t_lat, _ = x_sample.shape

    cond = jnp.concatenate(
        [c, c_ctx[None, :], jnp.zeros((MOD_ROWS - n_lat - 1, d), F32)], axis=0)
    mod0 = _modulation(cond, l0_mod_w, l0_mod_b)
    mod1 = _modulation(cond, l1_mod_w, l1_mod_b)

    p0 = dict(
        mix_pre_g=_row(l0_mix_pre_g), mix_post_g=_row(l0_mix_post_g), w_in=l0_w_in.astype(BF16),
        conv_w_half=0.5 * l0_conv_w, conv_b_half=_row(0.5 * l0_conv_b),
        wg=[jnp.concatenate([l0_gate_a_w[k], l0_gate_x_w[k]], axis=-1).astype(BF16) for k in range(2)],
        bg_half=[0.5 * jnp.stack([l0_gate_a_b[k], l0_gate_x_b[k]], axis=0) for k in range(2)],
        lam=[_row(l0_lambda[k]) for k in range(2)],
        w_out=l0_w_out.astype(BF16),
        ffn_pre_g=_row(l0_ffn_pre_g), ffn_post_g=_row(l0_ffn_post_g),
        ffn_w_in=l0_ffn_w_in.astype(BF16), ffn_w_out=l0_ffn_w_out.astype(BF16))
    p1 = dict(
        mix_pre_g=_row(l1_mix_pre_g), mix_post_g=_row(l1_mix_post_g), pool_w=l1_pool_w.astype(BF16),
        pool_b=_row(l1_pool_b), pool_scale=_row(l1_pool_scale),
        ffn_pre_g=_row(l1_ffn_pre_g), ffn_post_g=_row(l1_ffn_post_g),
        ffn_w_in=l1_ffn_w_in.astype(BF16), ffn_w_out=l1_ffn_w_out.astype(BF16))

    def run(x, pos, mod_row, h0):
        t_len = x.shape[1]
        x2, state = _layer0(x, pos, mod0, mod_row, h0, p0, min(CHUNK_L0, t_len))
        return _layer1(x2, mod1, mod_row, p1, min(CHUNK_L1, t_len)), state

    y_prompt, new_state = run(x_prompt, None, lambda b: CTX_ROW, jnp.zeros((n_ctx, 2, D_RNN), F32))
    y_sample, _ = run(x_sample, _grid_pos_table(t_lat), lambda b: b, state_l0_rglru)
    return y_prompt, y_sample, new_state
```

```python
import functools

import jax
import jax.numpy as jnp
from jax import lax
from jax.experimental import pallas as pl
from jax.experimental.pallas import tpu as pltpu

D_MODEL = 1024
D_RNN = D_MODEL
N_LRU_BLOCKS = 4
LRU_BLOCK = D_RNN // N_LRU_BLOCKS
CONV_W = 4
LRU_C = 8.0
POOL_WINDOWS = (2, 4, 8, 16)
POOL_GROUP = D_MODEL // len(POOL_WINDOWS)
D_FF = 2816
N_MOD = 6
EPS = 1e-6
POS_THETA = 10000.0
GRID_W = 64

LANES = 128
SUBLANES = 8
HALO = SUBLANES
N_SLABS = D_RNN // LANES
FF_CHUNK = 256
N_FF_CHUNKS = D_FF // FF_CHUNK
MOD_ROWS = 16
CTX_ROW = 8
MOD_TN = 1536
SQRT_FLOOR = 1e-36
CHUNK_L0 = 256
CHUNK_L1 = 512
FWD_CHUNKS = 2
BWD_CHUNKS = 2

VMEM_LIMIT = 56 * 1024 * 1024
VMEM_LIMIT_BIG = 62 * 1024 * 1024

F32 = jnp.float32
BF16 = jnp.bfloat16


def _dot(a, b):
    return jnp.dot(a, b, preferred_element_type=F32)


def _rms(x):
    return x * lax.rsqrt(jnp.mean(x * x, axis=-1, keepdims=True) + EPS)


def _sigmoid(x):
    return 1.0 / (1.0 + jnp.exp(-x))


def _mod_body(cond_ref, w_ref, b_ref, o_ref):
    c = cond_ref[...]
    s = (c * _sigmoid(c)).astype(BF16)
    o_ref[...] = _dot(s, w_ref[...].astype(BF16)) + b_ref[...]


def _modulation(cond, w, b):
    n = N_MOD * D_MODEL
    out = pl.pallas_call(
        _mod_body,
        grid=(n // MOD_TN,),
        in_specs=[
            pl.BlockSpec((MOD_ROWS, D_MODEL), lambda j: (0, 0)),
            pl.BlockSpec((D_MODEL, MOD_TN), lambda j: (0, j)),
            pl.BlockSpec((1, MOD_TN), lambda j: (0, j)),
        ],
        out_specs=pl.BlockSpec((MOD_ROWS, MOD_TN), lambda j: (0, j)),
        out_shape=jax.ShapeDtypeStruct((MOD_ROWS, n), F32),
        compiler_params=pltpu.CompilerParams(
            dimension_semantics=("parallel",), vmem_limit_bytes=VMEM_LIMIT),
        name="modulation",
    )(cond, w, b.reshape(1, n))
    return out.reshape(MOD_ROWS, N_MOD, D_MODEL)


def _gate_ab(hu, n, wg_ref, bg_ref, lam_ref):
    cols = slice(n * LRU_BLOCK, (n + 1) * LRU_BLOCK)
    o = _dot(hu.astype(BF16), wg_ref[n])
    t_r = jnp.tanh(o[:, :LRU_BLOCK] + bg_ref[0:1, cols])
    t_i = jnp.tanh(o[:, LRU_BLOCK:] + bg_ref[1:2, cols])
    nl = -lam_ref[:, cols]
    softplus = jnp.maximum(nl, 0.0) + jnp.log1p(jnp.exp(-jnp.abs(nl)))
    log_a = (t_r + 1.0) * ((-0.5 * LRU_C) * softplus)
    a = jnp.exp(log_a)
    s = jnp.tanh(log_a) * (-1.0 - a * a)
    root = s * lax.rsqrt(jnp.maximum(s, SQRT_FLOOR))
    b = root * (hu + hu * t_i)
    return a, b


def _scan(a, b, h_in, reverse):
    tc = a.shape[0]
    sub = tc // SUBLANES
    sub_id = lax.broadcasted_iota(jnp.int32, (SUBLANES, LANES), 0)
    steps = range(sub - 1, -1, -1) if reverse else range(sub)
    order = range(SUBLANES - 1, -1, -1) if reverse else range(SUBLANES)
    n_slabs = a.shape[1] // LANES
    hs = [[None] * sub for _ in range(n_slabs)]
    ps = [[None] * sub for _ in range(n_slabs)]
    h, p = [None] * n_slabs, [None] * n_slabs
    for j in steps:
        for k in range(n_slabs):
            av = a[j * SUBLANES:(j + 1) * SUBLANES, k * LANES:(k + 1) * LANES]
            bv = b[j * SUBLANES:(j + 1) * SUBLANES, k * LANES:(k + 1) * LANES]
            h[k] = bv if h[k] is None else av * h[k] + bv
            p[k] = av if p[k] is None else av * p[k]
            hs[k][j], ps[k][j] = h[k], p[k]
    slabs, last = [], []
    for k in range(n_slabs):
        carry = h_in[:, k * LANES:(k + 1) * LANES]
        carry_in = jnp.zeros((SUBLANES, LANES), F32)
        for s in order:
            carry_in = jnp.where(sub_id == s, carry, carry_in)
            carry = p[k][s:s + 1, :] * carry + h[k][s:s + 1, :]
        slabs.append(jnp.concatenate([ps[k][j] * carry_in + hs[k][j] for j in range(sub)], axis=0))
        last.append(carry)
    return jnp.concatenate(slabs, axis=1), jnp.concatenate(last, axis=1)


def _pitch(tc):
    return tc // SUBLANES + SUBLANES


def _permute_in(e, ext_s, tc, n_tiles):
    sub = tc // SUBLANES
    pitch = _pitch(tc)
    row = lax.broadcasted_iota(jnp.int32, (SUBLANES, LANES), 0)
    for k in range(N_SLABS):
        lanes = slice(k * LANES, (k + 1) * LANES)
        ext_s[k, 0:HALO, :] = e[0:HALO, lanes]
        for s in range(SUBLANES):
            base = HALO + s * pitch
            end = HALO + (s + 1) * sub
            ext_s[k, base:base + sub, :] = e[end - sub:end, lanes]
            ext_s[k, base + sub:base + pitch, :] = jnp.where(
                row < SUBLANES // 2, e[end:end + SUBLANES, lanes], e[end - SUBLANES:end, lanes])
    tiles = [jnp.concatenate([ext_s[k, pl.ds(HALO - 2 + q, SUBLANES, stride=pitch), :] for k in range(N_SLABS)],
                             axis=1) for q in range(n_tiles)]
    return jnp.concatenate(tiles, axis=0)


def _permute_out(r, o_s, tc):
    sub = tc // SUBLANES
    pitch = _pitch(tc)
    for k in range(N_SLABS):
        for j in range(sub):
            o_s[k, pl.ds(j, SUBLANES, stride=pitch), :] = r[j * SUBLANES:(j + 1) * SUBLANES, k * LANES:(k + 1) * LANES]
    return jnp.concatenate(
        [jnp.concatenate([o_s[k, s * pitch:s * pitch + sub, :] for s in range(SUBLANES)], axis=0)
         for k in range(N_SLABS)], axis=1)


def _pos_rows(ptab_ref, grid_row, col0, n):
    half = ptab_ref.shape[1]
    by_row = jnp.broadcast_to(ptab_ref[pl.ds(grid_row, 1), :], (n, half))
    return jnp.concatenate([by_row, ptab_ref[col0:col0 + n, :]], axis=1)


def _pos_block(ptab_ref, first_grid_row, n_rows):
    return jnp.concatenate([_pos_rows(ptab_ref, first_grid_row + k, 0, GRID_W) for k in range(n_rows // GRID_W)],
                           axis=0)


def _fwd_body(*refs, nc, tc, has_pos):
    if has_pos:
        (x_ref, xp_ref, xn_ref, ptab_ref, mod_ref, g_ref, w_ref, cw_ref, cb_ref, wg_ref, bg_ref,
         lam_ref, h0_ref, zg_ref, hu_ref, hf_ref, sf_ref, carry_ref, ext_s) = refs
    else:
        (x_ref, xp_ref, xn_ref, mod_ref, g_ref, w_ref, cw_ref, cb_ref, wg_ref, bg_ref,
         lam_ref, h0_ref, zg_ref, hu_ref, hf_ref, sf_ref, carry_ref, ext_s) = refs
    carry_ref = carry_ref.at[0:1]
    c = pl.program_id(1)
    n_sub = x_ref.shape[1] // tc
    sub = tc // SUBLANES
    n_tiles = sub + CONV_W

    @pl.when(c == 0)
    def _():
        carry_ref[...] = h0_ref[0]

    x, xp, xn = x_ref[0], xp_ref[0], xn_ref[0]
    if has_pos:
        rows_per_block = x.shape[0] // GRID_W
        n_grid_rows = nc * rows_per_block
        r0 = c * rows_per_block
        x = x + _pos_block(ptab_ref, r0, x.shape[0])
        xp = xp + _pos_rows(ptab_ref, jnp.maximum(r0 - 1, 0), GRID_W - HALO, HALO)
        xn = xn + _pos_rows(ptab_ref, jnp.minimum(r0 + rows_per_block, n_grid_rows - 1), 0, HALO)
    gs = g_ref[...] * (1.0 + mod_ref[0, 1:2, :])
    sh = mod_ref[0, 0:1, :]
    h_all = jnp.concatenate([_rms(xp) * gs + sh, _rms(x) * gs + sh, _rms(xn) * gs + sh], axis=0)
    row = lax.broadcasted_iota(jnp.int32, (2 * SUBLANES, D_RNN), 0) & (SUBLANES - 1)
    carry = carry_ref[...]
    for q in range(n_sub):
        rows = slice(q * tc, (q + 1) * tc)
        e = h_all[q * tc:(q + 1) * tc + 2 * HALO]
        hp = _permute_in(e, ext_s, tc, n_tiles).astype(BF16)
        zg_ref[0, rows, :] = _dot(hp[2 * SUBLANES:2 * SUBLANES + tc], w_ref[:, :D_RNN])
        rec = _dot(hp, w_ref[:, D_RNN:])
        if q == 0:
            head = jnp.where(row < jnp.where(c == 0, 1, 0), 0.0, rec[0:2 * SUBLANES])
            rec = jnp.concatenate([head, rec[2 * SUBLANES:]], axis=0)
        if q == n_sub - 1:
            tail = jnp.where(row > jnp.where(c == nc - 1, SUBLANES - 2, SUBLANES - 1), 0.0,
                             rec[(sub + 2) * SUBLANES:])
            rec = jnp.concatenate([rec[:(sub + 2) * SUBLANES], tail], axis=0)
        hu = rec[0:tc] * cw_ref[0:1, :]
        for k in range(1, CONV_W):
            hu = hu + rec[k * SUBLANES:k * SUBLANES + tc] * cw_ref[k:k + 1, :]
        hu = hu + cb_ref[...]
        hu_ref[0, rows, :] = hu
        ab = [_gate_ab(hu[:, n * LRU_BLOCK:(n + 1) * LRU_BLOCK], n, wg_ref, bg_ref, lam_ref)
              for n in range(N_LRU_BLOCKS)]
        hf, carry = _scan(jnp.concatenate([a for a, _ in ab], axis=1),
                          jnp.concatenate([b for _, b in ab], axis=1), carry, False)
        hf_ref[0, rows, :] = hf
    carry_ref[...] = carry
    sf_ref[0] = carry


def _ffn_head(x, mod_ref, pre_ref):
    sh = mod_ref[0, 3:4, :]
    sc = mod_ref[0, 4:5, :]
    return (_rms(x) * (pre_ref[...] * (1.0 + sc)) + sh).astype(BF16)


def _ffn_steps(h, wi_ref, wo_ref, act_ref, out):
    def chunk(n):
        def f():
            g = _dot(h, wi_ref[:, n * FF_CHUNK:(n + 1) * FF_CHUNK])
            v = _dot(h, wi_ref[:, D_FF + n * FF_CHUNK:D_FF + (n + 1) * FF_CHUNK])
            hg = 0.5 * g
            act_ref[:, n * FF_CHUNK:(n + 1) * FF_CHUNK] = ((hg + hg * jnp.tanh(hg)) * v).astype(BF16)
        return f

    def down():
        out.append(_dot(act_ref[...], wo_ref[...]))

    return [chunk(n) for n in range(N_FF_CHUNKS)] + [down]


def _ffn_tail(x, y, mod_ref, post_ref):
    return x + _rms(y) * (post_ref[...] * mod_ref[0, 5:6, :])


def _trace_interleaved(major, minor):
    done = 0
    for k, step in enumerate(major):
        step()
        while done < (k + 1) * len(minor) // len(major):
            minor[done]()
            done += 1


def _bwd_ffn_body(*refs, nc, tc, n_chunks, has_pos):
    if has_pos:
        (hu_ref, zg_ref, hf_ref, x_ref, ptab_ref, mod_ref, wg_ref, bg_ref, lam_ref, h0_ref, wo_ref, pg_ref,
         fmod_ref, pre_ref, post_ref, wi_ref, wo2_ref, x2_ref, sb_ref,
         carry_ref, o_s, x1_s, act_ref, mixed_ref) = refs
    else:
        (hu_ref, zg_ref, hf_ref, x_ref, mod_ref, wg_ref, bg_ref, lam_ref, h0_ref, wo_ref, pg_ref,
         fmod_ref, pre_ref, post_ref, wi_ref, wo2_ref, x2_ref, sb_ref,
         carry_ref, o_s, x1_s, act_ref, mixed_ref) = refs
    carry_ref = carry_ref.at[0:1]
    i = pl.program_id(0)
    c = jnp.minimum(i, n_chunks - 1) % nc
    n_sub = x_ref.shape[1] // tc

    @pl.when(i == 0)
    def _():
        x1_s[...] = jnp.zeros(x1_s.shape, F32)

    @pl.when(c == 0)
    def _():
        carry_ref[...] = h0_ref[0]

    st = {}
    lasts = [None] * N_LRU_BLOCKS

    def block(n, q):
        def f():
            rows = slice(q * tc, (q + 1) * tc)
            cols = slice(n * LRU_BLOCK, (n + 1) * LRU_BLOCK)
            a, b = _gate_ab(hu_ref[0, rows, cols], n, wg_ref, bg_ref, lam_ref)
            h_in = carry_ref[:, cols] if lasts[n] is None else lasts[n]
            hb, lasts[n] = _scan(a, b, h_in, True)
            mixed_ref[rows, cols] = ((hf_ref[0, rows, cols] + hb) * jax.nn.gelu(zg_ref[0, rows, cols])).astype(BF16)
        return f

    def out_proj():
        st['y'] = _dot(mixed_ref[...], wo_ref[...])

    def residual():
        x = x_ref[0]
        if has_pos:
            x = x + _pos_block(ptab_ref, (nc - 1 - c) * (x.shape[0] // GRID_W), x.shape[0])
        r = _rms(st['y']) * (pg_ref[...] * mod_ref[0, 2:3, :])
        x1_s[i % 2] = x + jnp.concatenate(
            [_permute_out(r[q * tc:(q + 1) * tc], o_s.at[q], tc) for q in range(n_sub)], axis=0)

    mixer = ([block(n, q) for q in range(n_sub - 1, -1, -1) for n in range(N_LRU_BLOCKS)]
             + [out_proj, residual])

    xf = x1_s[(i + 1) % 2]
    yf = []
    _trace_interleaved(_ffn_steps(_ffn_head(xf, fmod_ref, pre_ref), wi_ref, wo2_ref, act_ref, yf), mixer)
    x2_ref[0] = _ffn_tail(xf, yf[0], fmod_ref, post_ref)
    h_last = jnp.concatenate(lasts, axis=1)

    @pl.when(i < n_chunks)
    def _():
        carry_ref[...] = h_last
        sb_ref[0] = h_last


def _pool_group(ext, gi, c, pw_ref, t_len):
    n = ext.shape[0]
    tc = n - 2 * HALO
    win = POOL_WINDOWS[gi]
    e = ext[:, gi * POOL_GROUP:(gi + 1) * POOL_GROUP]
    w = e + pltpu.roll(e, 1, 0)
    half = 1
    while 2 * half < win:
        w = pltpu.roll(w, half, 0) + pltpu.roll(w, n - half, 0)
        half *= 2
    t = c * tc + lax.broadcasted_iota(jnp.int32, (tc, POOL_GROUP), 0)
    cnt = (jnp.minimum(t + win // 2, t_len) - jnp.maximum(t - win // 2, 0)).astype(F32)
    pooled = w[HALO:n - HALO] / cnt - e[HALO:n - HALO]
    return _dot(pooled.astype(BF16), pw_ref[gi])


def _pool_ffn_body(x_ref, xp_ref, xn_ref, mod_ref, mpre_ref, mpost_ref, pw_ref, pb_ref, ps_ref,
                   fmod_ref, pre_ref, post_ref, wi_ref, wo_ref, o_ref, x3_s, act_ref, *, nc, t_len, n_chunks):
    i = pl.program_id(0)
    c = jnp.minimum(i, n_chunks - 1) % nc

    def mixer_steps():
        st = {}
        ys = [None] * len(POOL_WINDOWS)

        def pre_norm():
            gs = mpre_ref[...] * (1.0 + mod_ref[0, 1:2, :])
            sh = mod_ref[0, 0:1, :]
            st['ext'] = jnp.concatenate([jnp.where(c > 0, _rms(xp_ref[0]) * gs + sh, 0.0),
                                         _rms(x_ref[0]) * gs + sh,
                                         jnp.where(c < nc - 1, _rms(xn_ref[0]) * gs + sh, 0.0)], axis=0)

        def group(gi):
            def f():
                ys[gi] = _pool_group(st['ext'], gi, c, pw_ref, t_len)
            return f

        def residual():
            y = (jnp.concatenate(ys, axis=1) + pb_ref[...]) * ps_ref[...]
            x3_s[i % 2] = x_ref[0] + _rms(y) * (mpost_ref[...] * mod_ref[0, 2:3, :])

        return [pre_norm] + [group(gi) for gi in range(len(POOL_WINDOWS))] + [residual]

    def ffn_with(mixer):
        xf = x3_s[(i + 1) % 2]
        yf = []
        _trace_interleaved(_ffn_steps(_ffn_head(xf, fmod_ref, pre_ref), wi_ref, wo_ref, act_ref, yf), mixer)
        o_ref[0] = _ffn_tail(xf, yf[0], fmod_ref, post_ref)

    @pl.when(i == 0)
    def _():
        for step in mixer_steps():
            step()

    @pl.when(jnp.logical_and(i > 0, i < n_chunks))
    def _():
        ffn_with(mixer_steps())

    @pl.when(i == n_chunks)
    def _():
        ffn_with([])


def _const_spec(shape, single=False):
    nd = len(shape)
    if single:
        return pl.BlockSpec(shape, lambda *_: (0,) * nd, pipeline_mode=pl.Buffered(1))
    return pl.BlockSpec(shape, lambda *_: (0,) * nd)


def _ffn_specs(d):
    return [_const_spec((1, d)), _const_spec((1, d)),
            _const_spec((d, 2 * D_FF), single=True), _const_spec((D_FF, d), single=True)]


def _layer0(x, pos, mod, mod_row, h0, p, tc):
    bsz, t_len, d = x.shape
    nc = t_len // tc
    n_chunks = bsz * nc
    n_hblk = t_len // HALO
    has_pos = pos is not None
    act = jax.ShapeDtypeStruct((bsz, t_len, d), F32)
    st = jax.ShapeDtypeStruct((bsz, 1, d), F32)

    tcb = tc * min(FWD_CHUNKS, nc)
    hb8 = tcb // HALO
    prev_blk = lambda c: jnp.maximum(c * hb8 - 1, 0)
    next_blk = lambda c: jnp.minimum((c + 1) * hb8, n_hblk - 1)
    row_spec = pl.BlockSpec((1, d), lambda b, c: (0, 0))
    chunk = pl.BlockSpec((1, tcb, d), lambda b, c: (b, c, 0))
    state_spec = pl.BlockSpec((1, 1, d), lambda b, c: (b, 0, 0))
    pos_specs = [_const_spec(pos.shape)] if has_pos else []
    in_specs = ([chunk,
                 pl.BlockSpec((1, HALO, d), lambda b, c: (b, prev_blk(c), 0)),
                 pl.BlockSpec((1, HALO, d), lambda b, c: (b, next_blk(c), 0))]
                + pos_specs
                + [pl.BlockSpec((1, N_MOD, d), lambda b, c: (mod_row(b), 0, 0)), row_spec,
                   _const_spec((d, 2 * D_RNN)), _const_spec((CONV_W, d)), row_spec,
                   _const_spec((N_LRU_BLOCKS, LRU_BLOCK, 2 * LRU_BLOCK)), _const_spec((2, d)), row_spec, state_spec])
    args = ([x, x, x] + ([pos] if has_pos else [])
            + [mod, p['mix_pre_g'], p['w_in'], p['conv_w_half'], p['conv_b_half'], p['wg'][0], p['bg_half'][0],
               p['lam'][0], h0[:, 0:1]])
    zg, hu, hf, sf = pl.pallas_call(
        functools.partial(_fwd_body, nc=t_len // tcb, tc=tc, has_pos=has_pos),
        grid=(bsz, t_len // tcb), in_specs=in_specs, out_specs=[chunk, chunk, chunk, state_spec],
        out_shape=[act, act, act, st],
        scratch_shapes=[pltpu.VMEM((SUBLANES, d), F32),
                        pltpu.VMEM((N_SLABS, HALO + SUBLANES * _pitch(tc), LANES), F32)],
        compiler_params=pltpu.CompilerParams(
            dimension_semantics=("parallel", "arbitrary"), vmem_limit_bytes=VMEM_LIMIT),
        name="l0_fwd",
    )(*args)

    n_sub = min(BWD_CHUNKS, nc)
    tcb = tc * n_sub
    nb = t_len // tcb
    n_blocks = bsz * nb

    def mix_at(i):
        im = jnp.minimum(i, n_blocks - 1)
        return im // nb, nb - 1 - im % nb

    def ffn_at(i):
        return mix_at(jnp.maximum(i - 1, 0))

    mchunk = pl.BlockSpec((1, tcb, d), lambda i: (*mix_at(i), 0))
    in_specs = ([mchunk, mchunk, mchunk, mchunk]
                + pos_specs
                + [pl.BlockSpec((1, N_MOD, d), lambda i: (mod_row(mix_at(i)[0]), 0, 0)),
                   _const_spec((N_LRU_BLOCKS, LRU_BLOCK, 2 * LRU_BLOCK)), _const_spec((2, d)), _const_spec((1, d)),
                   pl.BlockSpec((1, 1, d), lambda i: (mix_at(i)[0], 0, 0)),
                   _const_spec((D_RNN, d)), _const_spec((1, d)),
                   pl.BlockSpec((1, N_MOD, d), lambda i: (mod_row(ffn_at(i)[0]), 0, 0))]
                + _ffn_specs(d))
    args = ([hu, zg, hf, x] + ([pos] if has_pos else [])
            + [mod, p['wg'][1], p['bg_half'][1], p['lam'][1], h0[:, 1:2], p['w_out'], p['mix_post_g'],
               mod, p['ffn_pre_g'], p['ffn_post_g'], p['ffn_w_in'], p['ffn_w_out']])
    x2, sb = pl.pallas_call(
        functools.partial(_bwd_ffn_body, nc=nb, tc=tc, n_chunks=n_blocks, has_pos=has_pos),
        grid=(n_blocks + 1,), in_specs=in_specs,
        out_specs=[pl.BlockSpec((1, tcb, d), lambda i: (*ffn_at(i), 0)),
                   pl.BlockSpec((1, 1, d), lambda i: (mix_at(i)[0], 0, 0))],
        out_shape=[act, st],
        scratch_shapes=[pltpu.VMEM((SUBLANES, d), F32),
                        pltpu.VMEM((n_sub, N_SLABS, SUBLANES * _pitch(tc), LANES), F32),
                        pltpu.VMEM((2, tcb, d), F32), pltpu.VMEM((tcb, D_FF), BF16), pltpu.VMEM((tcb, D_RNN), BF16)],
        compiler_params=pltpu.CompilerParams(dimension_semantics=("arbitrary",), vmem_limit_bytes=VMEM_LIMIT_BIG),
        name="l0_bwd_ffn",
    )(*args)
    return x2, jnp.concatenate([sf, sb], axis=1)


def _layer1(x, mod, mod_row, p, tc):
    bsz, t_len, d = x.shape
    nc = t_len // tc
    n_chunks = bsz * nc
    hb8 = tc // HALO
    n_hblk = t_len // HALO

    def mix_at(i):
        im = jnp.minimum(i, n_chunks - 1)
        return im // nc, im % nc

    def ffn_at(i):
        return mix_at(jnp.maximum(i - 1, 0))

    def halo_spec(blk_of):
        return pl.BlockSpec((1, HALO, d), lambda i: (mix_at(i)[0], blk_of(mix_at(i)[1]), 0))

    return pl.pallas_call(
        functools.partial(_pool_ffn_body, nc=nc, t_len=t_len, n_chunks=n_chunks),
        grid=(n_chunks + 1,),
        in_specs=[pl.BlockSpec((1, tc, d), lambda i: (*mix_at(i), 0)),
                  halo_spec(lambda c: jnp.maximum(c * hb8 - 1, 0)),
                  halo_spec(lambda c: jnp.minimum((c + 1) * hb8, n_hblk - 1)),
                  pl.BlockSpec((1, N_MOD, d), lambda i: (mod_row(mix_at(i)[0]), 0, 0)),
                  _const_spec((1, d)), _const_spec((1, d)),
                  _const_spec((len(POOL_WINDOWS), POOL_GROUP, POOL_GROUP)), _const_spec((1, d)), _const_spec((1, d)),
                  pl.BlockSpec((1, N_MOD, d), lambda i: (mod_row(ffn_at(i)[0]), 0, 0))] + _ffn_specs(d),
        out_specs=pl.BlockSpec((1, tc, d), lambda i: (*ffn_at(i), 0)),
        out_shape=jax.ShapeDtypeStruct((bsz, t_len, d), F32),
        scratch_shapes=[pltpu.VMEM((2, tc, d), F32), pltpu.VMEM((tc, D_FF), BF16)],
        compiler_params=pltpu.CompilerParams(dimension_semantics=("arbitrary",), vmem_limit_bytes=VMEM_LIMIT),
        name="l1_ffn",
    )(x, x, x, mod, p['mix_pre_g'], p['mix_post_g'], p['pool_w'], p['pool_b'], p['pool_scale'],
      mod, p['ffn_pre_g'], p['ffn_post_g'], p['ffn_w_in'], p['ffn_w_out'])


def _grid_pos_table(t_len):
    rows = t_len // GRID_W
    quarter = D_MODEL // 4
    omega = 1.0 / (POS_THETA ** (jnp.arange(quarter, dtype=F32) / quarter))
    ang = jnp.arange(max(rows, GRID_W), dtype=F32)[:, None] * omega[None, :]
    return jnp.concatenate([jnp.sin(ang), jnp.cos(ang)], axis=-1)


def _row(v):
    return v.reshape(1, -1)


def kernel(x_prompt, x_sample, state_l0_rglru, c, c_ctx, l0_mod_w, l0_mod_b, l0_mix_pre_g, l0_mix_post_g, l0_w_in, l0_conv_w, l0_conv_b, l0_gate_a_w, l0_gate_a_b, l0_gate_x_w, l0_gate_x_b, l0_lambda, l0_w_out, l0_ffn_pre_g, l0_ffn_post_g, l0_ffn_w_in, l0_ffn_w_out, l1_mod_w, l1_mod_b, l1_mix_pre_g, l1_mix_post_g, l1_pool_w, l1_pool_b, l1_pool_scale, l1_ffn_pre_g, l1_ffn_post_g, l1_ffn_w_in, l1_ffn_w_out):
    n_ctx, t_ctx, d = x_prompt.shape
    n_lat, t_lat, _ = x_sample.shape

    cond = jnp.concatenate(
        [c, c_ctx[None, :], jnp.zeros((MOD_ROWS - n_lat - 1, d), F32)], axis=0)
    mod0 = _modulation(cond, l0_mod_w, l0_mod_b)
    mod1 = _modulation(cond, l1_mod_w, l1_mod_b)

    p0 = dict(
        mix_pre_g=_row(l0_mix_pre_g), mix_post_g=_row(l0_mix_post_g), w_in=l0_w_in.astype(BF16),
        conv_w_half=0.5 * l0_conv_w, conv_b_half=_row(0.5 * l0_conv_b),
        wg=[jnp.concatenate([l0_gate_a_w[k], l0_gate_x_w[k]], axis=-1).astype(BF16) for k in range(2)],
        bg_half=[0.5 * jnp.stack([l0_gate_a_b[k], l0_gate_x_b[k]], axis=0) for k in range(2)],
        lam=[_row(l0_lambda[k]) for k in range(2)],
        w_out=l0_w_out.astype(BF16),
        ffn_pre_g=_row(l0_ffn_pre_g), ffn_post_g=_row(l0_ffn_post_g),
        ffn_w_in=l0_ffn_w_in.astype(BF16), ffn_w_out=l0_ffn_w_out.astype(BF16))
    p1 = dict(
        mix_pre_g=_row(l1_mix_pre_g), mix_post_g=_row(l1_mix_post_g), pool_w=l1_pool_w.astype(BF16),
        pool_b=_row(l1_pool_b), pool_scale=_row(l1_pool_scale),
        ffn_pre_g=_row(l1_ffn_pre_g), ffn_post_g=_row(l1_ffn_post_g),
        ffn_w_in=l1_ffn_w_in.astype(BF16), ffn_w_out=l1_ffn_w_out.astype(BF16))

    def run(x, pos, mod_row, h0):
        t_len = x.shape[1]
        x2, state = _layer0(x, pos, mod0, mod_row, h0, p0, min(CHUNK_L0, t_len))
        return _layer1(x2, mod1, mod_row, p1, min(CHUNK_L1, t_len)), state

    y_prompt, new_state = run(x_prompt, None, lambda b: CTX_ROW, jnp.zeros((n_ctx, 2, D_RNN), F32))
    y_sample, _ = run(x_sample, _grid_pos_table(t_lat), lambda b: b, state_l0_rglru)
    return y_prompt, y_sample, new_state
```

```python
import functools

import jax
import jax.numpy as jnp
from jax import lax
from jax.experimental import pallas as pl
from jax.experimental.pallas import tpu as pltpu

D_MODEL = 1024
D_RNN = D_MODEL
N_LRU_BLOCKS = 4
LRU_BLOCK = D_RNN // N_LRU_BLOCKS
CONV_W = 4
LRU_C = 8.0
POOL_WINDOWS = (2, 4, 8, 16)
POOL_GROUP = D_MODEL // len(POOL_WINDOWS)
D_FF = 2816
N_MOD = 6
EPS = 1e-6
POS_THETA = 10000.0
GRID_W = 64

LANES = 128
SUBLANES = 8
HALO = SUBLANES
N_SLABS = D_RNN // LANES
FF_CHUNK = 256
N_FF_CHUNKS = D_FF // FF_CHUNK
MOD_ROWS = 16
CTX_ROW = 8
MOD_TN = 1536
SQRT_FLOOR = 1e-36
CHUNK_L0 = 256
CHUNK_L1 = 512
FWD_CHUNKS = 2
BWD_CHUNKS = 2

VMEM_LIMIT = 56 * 1024 * 1024
VMEM_LIMIT_BIG = 62 * 1024 * 1024

F32 = jnp.float32
BF16 = jnp.bfloat16


def _dot(a, b):
    return jnp.dot(a, b, preferred_element_type=F32)


def _rms(x):
    return x * lax.rsqrt(jnp.mean(x * x, axis=-1, keepdims=True) + EPS)


def _sigmoid(x):
    return 1.0 / (1.0 + jnp.exp(-x))


def _mod_body(cond_ref, w_ref, b_ref, o_ref):
    c = cond_ref[...]
    s = (c * _sigmoid(c)).astype(BF16)
    o_ref[...] = _dot(s, w_ref[...].astype(BF16)) + b_ref[...]


def _modulation(cond, w, b):
    n = N_MOD * D_MODEL
    out = pl.pallas_call(
        _mod_body,
        grid=(n // MOD_TN,),
        in_specs=[
            pl.BlockSpec((MOD_ROWS, D_MODEL), lambda j: (0, 0)),
            pl.BlockSpec((D_MODEL, MOD_TN), lambda j: (0, j)),
            pl.BlockSpec((1, MOD_TN), lambda j: (0, j)),
        ],
        out_specs=pl.BlockSpec((MOD_ROWS, MOD_TN), lambda j: (0, j)),
        out_shape=jax.ShapeDtypeStruct((MOD_ROWS, n), F32),
        compiler_params=pltpu.CompilerParams(
            dimension_semantics=("parallel",), vmem_limit_bytes=VMEM_LIMIT),
        name="modulation",
    )(cond, w, b.reshape(1, n))
    return out.reshape(MOD_ROWS, N_MOD, D_MODEL)


def _gate_ab(hu, n, wg_ref, bg_ref, lam_ref):
    cols = slice(n * LRU_BLOCK, (n + 1) * LRU_BLOCK)
    o = _dot(hu.astype(BF16), wg_ref[n])
    t_r = jnp.tanh(o[:, :LRU_BLOCK] + bg_ref[0:1, cols])
    t_i = jnp.tanh(o[:, LRU_BLOCK:] + bg_ref[1:2, cols])
    nl = -lam_ref[:, cols]
    softplus = jnp.maximum(nl, 0.0) + jnp.log1p(jnp.exp(-jnp.abs(nl)))
    log_a = (t_r + 1.0) * ((-0.5 * LRU_C) * softplus)
    a = jnp.exp(log_a)
    s = jnp.tanh(log_a) * (-1.0 - a * a)
    root = s * lax.rsqrt(jnp.maximum(s, SQRT_FLOOR))
    b = root * (hu + hu * t_i)
    return a, b


def _scan(a, b, h_in, reverse):
    tc = a.shape[0]
    sub = tc // SUBLANES
    sub_id = lax.broadcasted_iota(jnp.int32, (SUBLANES, LANES), 0)
    steps = range(sub - 1, -1, -1) if reverse else range(sub)
    order = range(SUBLANES - 1, -1, -1) if reverse else range(SUBLANES)
    n_slabs = a.shape[1] // LANES
    hs = [[None] * sub for _ in range(n_slabs)]
    ps = [[None] * sub for _ in range(n_slabs)]
    h, p = [None] * n_slabs, [None] * n_slabs
    for j in steps:
        for k in range(n_slabs):
            av = a[j * SUBLANES:(j + 1) * SUBLANES, k * LANES:(k + 1) * LANES]
            bv = b[j * SUBLANES:(j + 1) * SUBLANES, k * LANES:(k + 1) * LANES]
            h[k] = bv if h[k] is None else av * h[k] + bv
            p[k] = av if p[k] is None else av * p[k]
            hs[k][j], ps[k][j] = h[k], p[k]
    slabs, last = [], []
    for k in range(n_slabs):
        carry = h_in[:, k * LANES:(k + 1) * LANES]
        carry_in = jnp.zeros((SUBLANES, LANES), F32)
        for s in order:
            carry_in = jnp.where(sub_id == s, carry, carry_in)
            carry = p[k][s:s + 1, :] * carry + h[k][s:s + 1, :]
        slabs.append(jnp.concatenate([ps[k][j] * carry_in + hs[k][j] for j in range(sub)], axis=0))
        last.append(carry)
    return jnp.concatenate(slabs, axis=1), jnp.concatenate(last, axis=1)


def _pitch(tc):
    return tc // SUBLANES + SUBLANES


def _permute_in(e, ext_s, tc, n_tiles):
    sub = tc // SUBLANES
    pitch = _pitch(tc)
    row = lax.broadcasted_iota(jnp.int32, (SUBLANES, LANES), 0)
    for k in range(N_SLABS):
        lanes = slice(k * LANES, (k + 1) * LANES)
        ext_s[k, 0:HALO, :] = e[0:HALO, lanes]
        for s in range(SUBLANES):
            base = HALO + s * pitch
            end = HALO + (s + 1) * sub
            ext_s[k, base:base + sub, :] = e[end - sub:end, lanes]
            ext_s[k, base + sub:base + pitch, :] = jnp.where(
                row < SUBLANES // 2, e[end:end + SUBLANES, lanes], e[end - SUBLANES:end, lanes])
    tiles = [jnp.concatenate([ext_s[k, pl.ds(HALO - 2 + q, SUBLANES, stride=pitch), :] for k in range(N_SLABS)],
                             axis=1) for q in range(n_tiles)]
    return jnp.concatenate(tiles, axis=0)


def _permute_out(r, o_s, tc):
    sub = tc // SUBLANES
    pitch = _pitch(tc)
    for k in range(N_SLABS):
        for j in range(sub):
            o_s[k, pl.ds(j, SUBLANES, stride=pitch), :] = r[j * SUBLANES:(j + 1) * SUBLANES, k * LANES:(k + 1) * LANES]
    return jnp.concatenate(
        [jnp.concatenate([o_s[k, s * pitch:s * pitch + sub, :] for s in range(SUBLANES)], axis=0)
         for k in range(N_SLABS)], axis=1)


def _pos_rows(ptab_ref, grid_row, col0, n):
    half = ptab_ref.shape[1]
    by_row = jnp.broadcast_to(ptab_ref[pl.ds(grid_row, 1), :], (n, half))
    return jnp.concatenate([by_row, ptab_ref[col0:col0 + n, :]], axis=1)


def _pos_block(ptab_ref, first_grid_row, n_rows):
    return jnp.concatenate([_pos_rows(ptab_ref, first_grid_row + k, 0, GRID_W) for k in range(n_rows // GRID_W)],
                           axis=0)


def _fwd_body(*refs, nc, tc, has_pos):
    if has_pos:
        (x_ref, xp_ref, xn_ref, ptab_ref, mod_ref, g_ref, w_ref, cw_ref, cb_ref, wg_ref, bg_ref,
         lam_ref, h0_ref, zg_ref, hu_ref, hf_ref, sf_ref, carry_ref, ext_s) = refs
    else:
        (x_ref, xp_ref, xn_ref, mod_ref, g_ref, w_ref, cw_ref, cb_ref, wg_ref, bg_ref,
         lam_ref, h0_ref, zg_ref, hu_ref, hf_ref, sf_ref, carry_ref, ext_s) = refs
    carry_ref = carry_ref.at[0:1]
    c = pl.program_id(1)
    n_sub = x_ref.shape[1] // tc
    sub = tc // SUBLANES
    n_tiles = sub + CONV_W

    @pl.when(c == 0)
    def _():
        carry_ref[...] = h0_ref[0]

    x, xp, xn = x_ref[0], xp_ref[0], xn_ref[0]
    if has_pos:
        rows_per_block = x.shape[0] // GRID_W
        n_grid_rows = nc * rows_per_block
        r0 = c * rows_per_block
        x = x + _pos_block(ptab_ref, r0, x.shape[0])
        xp = xp + _pos_rows(ptab_ref, jnp.maximum(r0 - 1, 0), GRID_W - HALO, HALO)
        xn = xn + _pos_rows(ptab_ref, jnp.minimum(r0 + rows_per_block, n_grid_rows - 1), 0, HALO)
    gs = g_ref[...] * (1.0 + mod_ref[0, 1:2, :])
    sh = mod_ref[0, 0:1, :]
    h_all = jnp.concatenate([_rms(xp) * gs + sh, _rms(x) * gs + sh, _rms(xn) * gs + sh], axis=0)
    row = lax.broadcasted_iota(jnp.int32, (2 * SUBLANES, D_RNN), 0) & (SUBLANES - 1)
    carry = carry_ref[...]
    for q in range(n_sub):
        rows = slice(q * tc, (q + 1) * tc)
        e = h_all[q * tc:(q + 1) * tc + 2 * HALO]
        hp = _permute_in(e, ext_s, tc, n_tiles).astype(BF16)
        zg_ref[0, rows, :] = _dot(hp[2 * SUBLANES:2 * SUBLANES + tc], w_ref[:, :D_RNN]).astype(BF16)
        rec = _dot(hp, w_ref[:, D_RNN:])
        if q == 0:
            head = jnp.where(row < jnp.where(c == 0, 1, 0), 0.0, rec[0:2 * SUBLANES])
            rec = jnp.concatenate([head, rec[2 * SUBLANES:]], axis=0)
        if q == n_sub - 1:
            tail = jnp.where(row > jnp.where(c == nc - 1, SUBLANES - 2, SUBLANES - 1), 0.0,
                             rec[(sub + 2) * SUBLANES:])
            rec = jnp.concatenate([rec[:(sub + 2) * SUBLANES], tail], axis=0)
        hu = rec[0:tc] * cw_ref[0:1, :]
        for k in range(1, CONV_W):
            hu = hu + rec[k * SUBLANES:k * SUBLANES + tc] * cw_ref[k:k + 1, :]
        hu = hu + cb_ref[...]
        hu_ref[0, rows, :] = hu
        ab = [_gate_ab(hu[:, n * LRU_BLOCK:(n + 1) * LRU_BLOCK], n, wg_ref, bg_ref, lam_ref)
              for n in range(N_LRU_BLOCKS)]
        hf, carry = _scan(jnp.concatenate([a for a, _ in ab], axis=1),
                          jnp.concatenate([b for _, b in ab], axis=1), carry, False)
        hf_ref[0, rows, :] = hf
    carry_ref[...] = carry
    sf_ref[0] = carry


def _ffn_head(x, mod_ref, pre_ref):
    sh = mod_ref[0, 3:4, :]
    sc = mod_ref[0, 4:5, :]
    return (_rms(x) * (pre_ref[...] * (1.0 + sc)) + sh).astype(BF16)


def _ffn_steps(h, wi_ref, wo_ref, act_ref, out):
    def chunk(n):
        def f():
            g = _dot(h, wi_ref[:, n * FF_CHUNK:(n + 1) * FF_CHUNK])
            v = _dot(h, wi_ref[:, D_FF + n * FF_CHUNK:D_FF + (n + 1) * FF_CHUNK])
            hg = 0.5 * g
            act_ref[:, n * FF_CHUNK:(n + 1) * FF_CHUNK] = ((hg + hg * jnp.tanh(hg)) * v).astype(BF16)
        return f

    def down():
        out.append(_dot(act_ref[...], wo_ref[...]))

    return [chunk(n) for n in range(N_FF_CHUNKS)] + [down]


def _ffn_tail(x, y, mod_ref, post_ref):
    return x + _rms(y) * (post_ref[...] * mod_ref[0, 5:6, :])


def _trace_interleaved(major, minor):
    done = 0
    for k, step in enumerate(major):
        step()
        while done < (k + 1) * len(minor) // len(major):
            minor[done]()
            done += 1


def _bwd_ffn_body(*refs, nc, tc, n_chunks, has_pos):
    if has_pos:
        (hu_ref, zg_ref, hf_ref, x_ref, ptab_ref, mod_ref, wg_ref, bg_ref, lam_ref, h0_ref, wo_ref, pg_ref,
         fmod_ref, pre_ref, post_ref, wi_ref, wo2_ref, x2_ref, sb_ref,
         carry_ref, o_s, x1_s, act_ref, mixed_ref) = refs
    else:
        (hu_ref, zg_ref, hf_ref, x_ref, mod_ref, wg_ref, bg_ref, lam_ref, h0_ref, wo_ref, pg_ref,
         fmod_ref, pre_ref, post_ref, wi_ref, wo2_ref, x2_ref, sb_ref,
         carry_ref, o_s, x1_s, act_ref, mixed_ref) = refs
    carry_ref = carry_ref.at[0:1]
    i = pl.program_id(0)
    c = jnp.minimum(i, n_chunks - 1) % nc
    n_sub = x_ref.shape[1] // tc

    @pl.when(c == 0)
    def _():
        carry_ref[...] = h0_ref[0]

    def mixer_steps():
        st = {}
        lasts = [None] * N_LRU_BLOCKS

        def block(n, q):
            def f():
                rows = slice(q * tc, (q + 1) * tc)
                cols = slice(n * LRU_BLOCK, (n + 1) * LRU_BLOCK)
                a, b = _gate_ab(hu_ref[0, rows, cols], n, wg_ref, bg_ref, lam_ref)
                h_in = carry_ref[:, cols] if lasts[n] is None else lasts[n]
                hb, lasts[n] = _scan(a, b, h_in, True)
                mixed_ref[rows, cols] = (
                    (hf_ref[0, rows, cols] + hb) * jax.nn.gelu(zg_ref[0, rows, cols].astype(F32))).astype(BF16)
            return f

        def out_proj():
            st['y'] = _dot(mixed_ref[...], wo_ref[...])

        def residual():
            x = x_ref[0]
            if has_pos:
                x = x + _pos_block(ptab_ref, (nc - 1 - c) * (x.shape[0] // GRID_W), x.shape[0])
            r = _rms(st['y']) * (pg_ref[...] * mod_ref[0, 2:3, :])
            x1_s[i % 2] = x + jnp.concatenate(
                [_permute_out(r[q * tc:(q + 1) * tc], o_s.at[q], tc) for q in range(n_sub)], axis=0)
            h_last = jnp.concatenate(lasts, axis=1)
            carry_ref[...] = h_last
            sb_ref[0] = h_last

        return ([block(n, q) for q in range(n_sub - 1, -1, -1) for n in range(N_LRU_BLOCKS)]
                + [out_proj, residual])

    def ffn_with(mixer):
        xf = x1_s[(i + 1) % 2]
        yf = []
        _trace_interleaved(_ffn_steps(_ffn_head(xf, fmod_ref, pre_ref), wi_ref, wo2_ref, act_ref, yf), mixer)
        x2_ref[0] = _ffn_tail(xf, yf[0], fmod_ref, post_ref)

    @pl.when(i == 0)
    def _():
        for step in mixer_steps():
            step()

    @pl.when(jnp.logical_and(i > 0, i < n_chunks))
    def _():
        ffn_with(mixer_steps())

    @pl.when(i == n_chunks)
    def _():
        ffn_with([])


def _pool_group(ext, gi, c, pw_ref, t_len):
    n = ext.shape[0]
    tc = n - 2 * HALO
    win = POOL_WINDOWS[gi]
    e = ext[:, gi * POOL_GROUP:(gi + 1) * POOL_GROUP]
    w = e + pltpu.roll(e, 1, 0)
    half = 1
    while 2 * half < win:
        w = pltpu.roll(w, half, 0) + pltpu.roll(w, n - half, 0)
        half *= 2
    t = c * tc + lax.broadcasted_iota(jnp.int32, (tc, POOL_GROUP), 0)
    cnt = (jnp.minimum(t + win // 2, t_len) - jnp.maximum(t - win // 2, 0)).astype(F32)
    pooled = w[HALO:n - HALO] / cnt - e[HALO:n - HALO]
    return _dot(pooled.astype(BF16), pw_ref[gi])


def _pool_ffn_body(x_ref, xp_ref, xn_ref, mod_ref, mpre_ref, mpost_ref, pw_ref, pb_ref, ps_ref,
                   fmod_ref, pre_ref, post_ref, wi_ref, wo_ref, o_ref, x3_s, act_ref, *, nc, t_len, n_chunks):
    i = pl.program_id(0)
    c = jnp.minimum(i, n_chunks - 1) % nc

    def mixer_steps():
        st = {}
        ys = [None] * len(POOL_WINDOWS)

        def pre_norm():
            gs = mpre_ref[...] * (1.0 + mod_ref[0, 1:2, :])
            sh = mod_ref[0, 0:1, :]
            st['ext'] = jnp.concatenate([jnp.where(c > 0, _rms(xp_ref[0]) * gs + sh, 0.0),
                                         _rms(x_ref[0]) * gs + sh,
                                         jnp.where(c < nc - 1, _rms(xn_ref[0]) * gs + sh, 0.0)], axis=0)

        def group(gi):
            def f():
                ys[gi] = _pool_group(st['ext'], gi, c, pw_ref, t_len)
            return f

        def residual():
            y = (jnp.concatenate(ys, axis=1) + pb_ref[...]) * ps_ref[...]
            x3_s[i % 2] = x_ref[0] + _rms(y) * (mpost_ref[...] * mod_ref[0, 2:3, :])

        return [pre_norm] + [group(gi) for gi in range(len(POOL_WINDOWS))] + [residual]

    def ffn_with(mixer):
        xf = x3_s[(i + 1) % 2]
        yf = []
        _trace_interleaved(_ffn_steps(_ffn_head(xf, fmod_ref, pre_ref), wi_ref, wo_ref, act_ref, yf), mixer)
        o_ref[0] = _ffn_tail(xf, yf[0], fmod_ref, post_ref)

    @pl.when(i == 0)
    def _():
        for step in mixer_steps():
            step()

    @pl.when(jnp.logical_and(i > 0, i < n_chunks))
    def _():
        ffn_with(mixer_steps())

    @pl.when(i == n_chunks)
    def _():
        ffn_with([])


def _const_spec(shape, single=False):
    nd = len(shape)
    if single:
        return pl.BlockSpec(shape, lambda *_: (0,) * nd, pipeline_mode=pl.Buffered(1))
    return pl.BlockSpec(shape, lambda *_: (0,) * nd)


def _ffn_specs(d):
    return [_const_spec((1, d)), _const_spec((1, d)),
            _const_spec((d, 2 * D_FF), single=True), _const_spec((D_FF, d), single=True)]


def _layer0(x, pos, mod, mod_row, h0, p, tc):
    bsz, t_len, d = x.shape
    nc = t_len // tc
    n_chunks = bsz * nc
    n_hblk = t_len // HALO
    has_pos = pos is not None
    act = jax.ShapeDtypeStruct((bsz, t_len, d), F32)
    st = jax.ShapeDtypeStruct((bsz, 1, d), F32)

    tcb = tc * min(FWD_CHUNKS, nc)
    hb8 = tcb // HALO
    prev_blk = lambda c: jnp.maximum(c * hb8 - 1, 0)
    next_blk = lambda c: jnp.minimum((c + 1) * hb8, n_hblk - 1)
    row_spec = pl.BlockSpec((1, d), lambda b, c: (0, 0))
    chunk = pl.BlockSpec((1, tcb, d), lambda b, c: (b, c, 0))
    state_spec = pl.BlockSpec((1, 1, d), lambda b, c: (b, 0, 0))
    pos_specs = [_const_spec(pos.shape)] if has_pos else []
    in_specs = ([chunk,
                 pl.BlockSpec((1, HALO, d), lambda b, c: (b, prev_blk(c), 0)),
                 pl.BlockSpec((1, HALO, d), lambda b, c: (b, next_blk(c), 0))]
                + pos_specs
                + [pl.BlockSpec((1, N_MOD, d), lambda b, c: (mod_row(b), 0, 0)), row_spec,
                   _const_spec((d, 2 * D_RNN)), _const_spec((CONV_W, d)), row_spec,
                   _const_spec((N_LRU_BLOCKS, LRU_BLOCK, 2 * LRU_BLOCK)), _const_spec((2, d)), row_spec, state_spec])
    args = ([x, x, x] + ([pos] if has_pos else [])
            + [mod, p['mix_pre_g'], p['w_in'], p['conv_w_half'], p['conv_b_half'], p['wg'][0], p['bg_half'][0],
               p['lam'][0], h0[:, 0:1]])
    zg, hu, hf, sf = pl.pallas_call(
        functools.partial(_fwd_body, nc=t_len // tcb, tc=tc, has_pos=has_pos),
        grid=(bsz, t_len // tcb), in_specs=in_specs, out_specs=[chunk, chunk, chunk, state_spec],
        out_shape=[jax.ShapeDtypeStruct((bsz, t_len, d), BF16), act, act, st],
        scratch_shapes=[pltpu.VMEM((SUBLANES, d), F32),
                        pltpu.VMEM((N_SLABS, HALO + SUBLANES * _pitch(tc), LANES), F32)],
        compiler_params=pltpu.CompilerParams(
            dimension_semantics=("parallel", "arbitrary"), vmem_limit_bytes=VMEM_LIMIT),
        name="l0_fwd",
    )(*args)

    n_sub = min(BWD_CHUNKS, nc)
    tcb = tc * n_sub
    nb = t_len // tcb
    n_blocks = bsz * nb

    def mix_at(i):
        im = jnp.minimum(i, n_blocks - 1)
        return im // nb, nb - 1 - im % nb

    def ffn_at(i):
        return mix_at(jnp.maximum(i - 1, 0))

    mchunk = pl.BlockSpec((1, tcb, d), lambda i: (*mix_at(i), 0))
    in_specs = ([mchunk, mchunk, mchunk, mchunk]
                + pos_specs
                + [pl.BlockSpec((1, N_MOD, d), lambda i: (mod_row(mix_at(i)[0]), 0, 0)),
                   _const_spec((N_LRU_BLOCKS, LRU_BLOCK, 2 * LRU_BLOCK)), _const_spec((2, d)), _const_spec((1, d)),
                   pl.BlockSpec((1, 1, d), lambda i: (mix_at(i)[0], 0, 0)),
                   _const_spec((D_RNN, d)), _const_spec((1, d)),
                   pl.BlockSpec((1, N_MOD, d), lambda i: (mod_row(ffn_at(i)[0]), 0, 0))]
                + _ffn_specs(d))
    args = ([hu, zg, hf, x] + ([pos] if has_pos else [])
            + [mod, p['wg'][1], p['bg_half'][1], p['lam'][1], h0[:, 1:2], p['w_out'], p['mix_post_g'],
               mod, p['ffn_pre_g'], p['ffn_post_g'], p['ffn_w_in'], p['ffn_w_out']])
    x2, sb = pl.pallas_call(
        functools.partial(_bwd_ffn_body, nc=nb, tc=tc, n_chunks=n_blocks, has_pos=has_pos),
        grid=(n_blocks + 1,), in_specs=in_specs,
        out_specs=[pl.BlockSpec((1, tcb, d), lambda i: (*ffn_at(i), 0)),
                   pl.BlockSpec((1, 1, d), lambda i: (mix_at(i)[0], 0, 0))],
        out_shape=[act, st],
        scratch_shapes=[pltpu.VMEM((SUBLANES, d), F32),
                        pltpu.VMEM((n_sub, N_SLABS, SUBLANES * _pitch(tc), LANES), F32),
                        pltpu.VMEM((2, tcb, d), F32), pltpu.VMEM((tcb, D_FF), BF16), pltpu.VMEM((tcb, D_RNN), BF16)],
        compiler_params=pltpu.CompilerParams(dimension_semantics=("arbitrary",), vmem_limit_bytes=VMEM_LIMIT_BIG),
        name="l0_bwd_ffn",
    )(*args)
    return x2, jnp.concatenate([sf, sb], axis=1)


def _layer1(x, mod, mod_row, p, tc):
    bsz, t_len, d = x.shape
    nc = t_len // tc
    n_chunks = bsz * nc
    hb8 = tc // HALO
    n_hblk = t_len // HALO

    def mix_at(i):
        im = jnp.minimum(i, n_chunks - 1)
        return im // nc, im % nc

    def ffn_at(i):
        return mix_at(jnp.maximum(i - 1, 0))

    def halo_spec(blk_of):
        return pl.BlockSpec((1, HALO, d), lambda i: (mix_at(i)[0], blk_of(mix_at(i)[1]), 0))

    return pl.pallas_call(
        functools.partial(_pool_ffn_body, nc=nc, t_len=t_len, n_chunks=n_chunks),
        grid=(n_chunks + 1,),
        in_specs=[pl.BlockSpec((1, tc, d), lambda i: (*mix_at(i), 0)),
                  halo_spec(lambda c: jnp.maximum(c * hb8 - 1, 0)),
                  halo_spec(lambda c: jnp.minimum((c + 1) * hb8, n_hblk - 1)),
                  pl.BlockSpec((1, N_MOD, d), lambda i: (mod_row(mix_at(i)[0]), 0, 0)),
                  _const_spec((1, d)), _const_spec((1, d)),
                  _const_spec((len(POOL_WINDOWS), POOL_GROUP, POOL_GROUP)), _const_spec((1, d)), _const_spec((1, d)),
                  pl.BlockSpec((1, N_MOD, d), lambda i: (mod_row(ffn_at(i)[0]), 0, 0))] + _ffn_specs(d),
        out_specs=pl.BlockSpec((1, tc, d), lambda i: (*ffn_at(i), 0)),
        out_shape=jax.ShapeDtypeStruct((bsz, t_len, d), F32),
        scratch_shapes=[pltpu.VMEM((2, tc, d), F32), pltpu.VMEM((tc, D_FF), BF16)],
        compiler_params=pltpu.CompilerParams(dimension_semantics=("arbitrary",), vmem_limit_bytes=VMEM_LIMIT),
        name="l1_ffn",
    )(x, x, x, mod, p['mix_pre_g'], p['mix_post_g'], p['pool_w'], p['pool_b'], p['pool_scale'],
      mod, p['ffn_pre_g'], p['ffn_post_g'], p['ffn_w_in'], p['ffn_w_out'])


def _grid_pos_table(t_len):
    rows = t_len // GRID_W
    quarter = D_MODEL // 4
    omega = 1.0 / (POS_THETA ** (jnp.arange(quarter, dtype=F32) / quarter))
    ang = jnp.arange(max(rows, GRID_W), dtype=F32)[:, None] * omega[None, :]
    return jnp.concatenate([jnp.sin(ang), jnp.cos(ang)], axis=-1)


def _row(v):
    return v.reshape(1, -1)


def kernel(x_prompt, x_sample, state_l0_rglru, c, c_ctx, l0_mod_w, l0_mod_b, l0_mix_pre_g, l0_mix_post_g, l0_w_in, l0_conv_w, l0_conv_b, l0_gate_a_w, l0_gate_a_b, l0_gate_x_w, l0_gate_x_b, l0_lambda, l0_w_out, l0_ffn_pre_g, l0_ffn_post_g, l0_ffn_w_in, l0_ffn_w_out, l1_mod_w, l1_mod_b, l1_mix_pre_g, l1_mix_post_g, l1_pool_w, l1_pool_b, l1_pool_scale, l1_ffn_pre_g, l1_ffn_post_g, l1_ffn_w_in, l1_ffn_w_out):
    n_ctx, t_ctx, d = x_prompt.shape
    n_lat, t_lat, _ = x_sample.shape

    cond = jnp.concatenate(
        [c, c_ctx[None, :], jnp.zeros((MOD_ROWS - n_lat - 1, d), F32)], axis=0)
    mod0 = _modulation(cond, l0_mod_w, l0_mod_b)
    mod1 = _modulation(cond, l1_mod_w, l1_mod_b)

    p0 = dict(
        mix_pre_g=_row(l0_mix_pre_g), mix_post_g=_row(l0_mix_post_g), w_in=l0_w_in.astype(BF16),
        conv_w_half=0.5 * l0_conv_w, conv_b_half=_row(0.5 * l0_conv_b),
        wg=[jnp.concatenate([l0_gate_a_w[k], l0_gate_x_w[k]], axis=-1).astype(BF16) for k in range(2)],
        bg_half=[0.5 * jnp.stack([l0_gate_a_b[k], l0_gate_x_b[k]], axis=0) for k in range(2)],
        lam=[_row(l0_lambda[k]) for k in range(2)],
        w_out=l0_w_out.astype(BF16),
        ffn_pre_g=_row(l0_ffn_pre_g), ffn_post_g=_row(l0_ffn_post_g),
        ffn_w_in=l0_ffn_w_in.astype(BF16), ffn_w_out=l0_ffn_w_out.astype(BF16))
    p1 = dict(
        mix_pre_g=_row(l1_mix_pre_g), mix_post_g=_row(l1_mix_post_g), pool_w=l1_pool_w.astype(BF16),
        pool_b=_row(l1_pool_b), pool_scale=_row(l1_pool_scale),
        ffn_pre_g=_row(l1_ffn_pre_g), ffn_post_g=_row(l1_ffn_post_g),
        ffn_w_in=l1_ffn_w_in.astype(BF16), ffn_w_out=l1_ffn_w_out.astype(BF16))

    def run(x, pos, mod_row, h0):
        t_len = x.shape[1]
        x2, state = _layer0(x, pos, mod0, mod_row, h0, p0, min(CHUNK_L0, t_len))
        return _layer1(x2, mod1, mod_row, p1, min(CHUNK_L1, t_len)), state

    y_prompt, new_state = run(x_prompt, None, lambda b: CTX_ROW, jnp.zeros((n_ctx, 2, D_RNN), F32))
    y_sample, _ = run(x_sample, _grid_pos_table(t_lat), lambda b: b, state_l0_rglru)
    return y_prompt, y_sample, new_state
```

```python
import functools

import jax
import jax.numpy as jnp
from jax import lax
from jax.experimental import pallas as pl
from jax.experimental.pallas import tpu as pltpu

D_MODEL = 1024
D_RNN = D_MODEL
N_LRU_BLOCKS = 4
LRU_BLOCK = D_RNN // N_LRU_BLOCKS
CONV_W = 4
LRU_C = 8.0
POOL_WINDOWS = (2, 4, 8, 16)
POOL_GROUP = D_MODEL // len(POOL_WINDOWS)
D_FF = 2816
N_MOD = 6
EPS = 1e-6
POS_THETA = 10000.0
GRID_W = 64

LANES = 128
SUBLANES = 8
HALO = SUBLANES
N_SLABS = D_RNN // LANES
FF_CHUNK = 256
N_FF_CHUNKS = D_FF // FF_CHUNK
MOD_ROWS = 16
CTX_ROW = 8
MOD_TN = 1536
SQRT_FLOOR = 1e-36
CHUNK_L0 = 256
CHUNK_L1 = 512
FWD_CHUNKS = 2
BWD_CHUNKS = 2

VMEM_LIMIT = 56 * 1024 * 1024
VMEM_LIMIT_BIG = 62 * 1024 * 1024

F32 = jnp.float32
BF16 = jnp.bfloat16


def _dot(a, b):
    return jnp.dot(a, b, preferred_element_type=F32)


def _rms(x):
    return x * lax.rsqrt(jnp.mean(x * x, axis=-1, keepdims=True) + EPS)


def _sigmoid(x):
    return 1.0 / (1.0 + jnp.exp(-x))


def _mod_body(cond_ref, w_ref, b_ref, o_ref):
    c = cond_ref[...]
    s = (c * _sigmoid(c)).astype(BF16)
    o_ref[...] = _dot(s, w_ref[...].astype(BF16)) + b_ref[...]


def _modulation(cond, w, b):
    n = N_MOD * D_MODEL
    out = pl.pallas_call(
        _mod_body,
        grid=(n // MOD_TN,),
        in_specs=[
            pl.BlockSpec((MOD_ROWS, D_MODEL), lambda j: (0, 0)),
            pl.BlockSpec((D_MODEL, MOD_TN), lambda j: (0, j)),
            pl.BlockSpec((1, MOD_TN), lambda j: (0, j)),
        ],
        out_specs=pl.BlockSpec((MOD_ROWS, MOD_TN), lambda j: (0, j)),
        out_shape=jax.ShapeDtypeStruct((MOD_ROWS, n), F32),
        compiler_params=pltpu.CompilerParams(
            dimension_semantics=("parallel",), vmem_limit_bytes=VMEM_LIMIT),
        name="modulation",
    )(cond, w, b.reshape(1, n))
    return out.reshape(MOD_ROWS, N_MOD, D_MODEL)


def _gate_ab(hu, n, wg_ref, bg_ref, lam_ref):
    cols = slice(n * LRU_BLOCK, (n + 1) * LRU_BLOCK)
    o = _dot(hu.astype(BF16), wg_ref[n])
    t_r = jnp.tanh(o[:, :LRU_BLOCK] + bg_ref[0:1, cols])
    t_i = jnp.tanh(o[:, LRU_BLOCK:] + bg_ref[1:2, cols])
    nl = -lam_ref[:, cols]
    softplus = jnp.maximum(nl, 0.0) + jnp.log1p(jnp.exp(-jnp.abs(nl)))
    log_a = (t_r + 1.0) * ((-0.5 * LRU_C) * softplus)
    a = jnp.exp(log_a)
    s = jnp.tanh(log_a) * (-1.0 - a * a)
    root = s * lax.rsqrt(jnp.maximum(s, SQRT_FLOOR))
    b = root * (hu + hu * t_i)
    return a, b


def _scan(a, b, h_in, reverse):
    tc = a.shape[0]
    sub = tc // SUBLANES
    sub_id = lax.broadcasted_iota(jnp.int32, (SUBLANES, LANES), 0)
    steps = range(sub - 1, -1, -1) if reverse else range(sub)
    order = range(SUBLANES - 1, -1, -1) if reverse else range(SUBLANES)
    n_slabs = a.shape[1] // LANES
    hs = [[None] * sub for _ in range(n_slabs)]
    ps = [[None] * sub for _ in range(n_slabs)]
    h, p = [None] * n_slabs, [None] * n_slabs
    for j in steps:
        for k in range(n_slabs):
            av = a[j * SUBLANES:(j + 1) * SUBLANES, k * LANES:(k + 1) * LANES]
            bv = b[j * SUBLANES:(j + 1) * SUBLANES, k * LANES:(k + 1) * LANES]
            h[k] = bv if h[k] is None else av * h[k] + bv
            p[k] = av if p[k] is None else av * p[k]
            hs[k][j], ps[k][j] = h[k], p[k]
    slabs, last = [], []
    for k in range(n_slabs):
        carry = h_in[:, k * LANES:(k + 1) * LANES]
        carry_in = jnp.zeros((SUBLANES, LANES), F32)
        for s in order:
            carry_in = jnp.where(sub_id == s, carry, carry_in)
            carry = p[k][s:s + 1, :] * carry + h[k][s:s + 1, :]
        slabs.append(jnp.concatenate([ps[k][j] * carry_in + hs[k][j] for j in range(sub)], axis=0))
        last.append(carry)
    return jnp.concatenate(slabs, axis=1), jnp.concatenate(last, axis=1)


def _pitch(tc):
    return tc // SUBLANES + SUBLANES


def _permute_in(e, ext_s, tc, n_tiles):
    sub = tc // SUBLANES
    pitch = _pitch(tc)
    row = lax.broadcasted_iota(jnp.int32, (SUBLANES, LANES), 0)
    for k in range(N_SLABS):
        lanes = slice(k * LANES, (k + 1) * LANES)
        ext_s[k, 0:HALO, :] = e[0:HALO, lanes]
        for s in range(SUBLANES):
            base = HALO + s * pitch
            end = HALO + (s + 1) * sub
            ext_s[k, base:base + sub, :] = e[end - sub:end, lanes]
            ext_s[k, base + sub:base + pitch, :] = jnp.where(
                row < SUBLANES // 2, e[end:end + SUBLANES, lanes], e[end - SUBLANES:end, lanes])
    tiles = [jnp.concatenate([ext_s[k, pl.ds(HALO - 2 + q, SUBLANES, stride=pitch), :] for k in range(N_SLABS)],
                             axis=1) for q in range(n_tiles)]
    return jnp.concatenate(tiles, axis=0)


def _permute_out(r, o_s, tc):
    sub = tc // SUBLANES
    pitch = _pitch(tc)
    for k in range(N_SLABS):
        for j in range(sub):
            o_s[k, pl.ds(j, SUBLANES, stride=pitch), :] = r[j * SUBLANES:(j + 1) * SUBLANES, k * LANES:(k + 1) * LANES]
    return jnp.concatenate(
        [jnp.concatenate([o_s[k, s * pitch:s * pitch + sub, :] for s in range(SUBLANES)], axis=0)
         for k in range(N_SLABS)], axis=1)


def _pos_rows(ptab_ref, grid_row, col0, n):
    half = ptab_ref.shape[1]
    by_row = jnp.broadcast_to(ptab_ref[pl.ds(grid_row, 1), :], (n, half))
    return jnp.concatenate([by_row, ptab_ref[col0:col0 + n, :]], axis=1)


def _pos_block(ptab_ref, first_grid_row, n_rows):
    return jnp.concatenate([_pos_rows(ptab_ref, first_grid_row + k, 0, GRID_W) for k in range(n_rows // GRID_W)],
                           axis=0)


def _fwd_body(*refs, nc, tc, has_pos):
    if has_pos:
        (x_ref, xp_ref, xn_ref, ptab_ref, mod_ref, g_ref, w_ref, cw_ref, cb_ref, wg_ref, bg_ref,
         lam_ref, h0_ref, zg_ref, hu_ref, hf_ref, sf_ref, carry_ref, ext_s) = refs
    else:
        (x_ref, xp_ref, xn_ref, mod_ref, g_ref, w_ref, cw_ref, cb_ref, wg_ref, bg_ref,
         lam_ref, h0_ref, zg_ref, hu_ref, hf_ref, sf_ref, carry_ref, ext_s) = refs
    carry_ref = carry_ref.at[0:1]
    c = pl.program_id(1)
    n_sub = x_ref.shape[1] // tc
    sub = tc // SUBLANES
    n_tiles = sub + CONV_W

    @pl.when(c == 0)
    def _():
        carry_ref[...] = h0_ref[0]

    x, xp, xn = x_ref[0], xp_ref[0], xn_ref[0]
    if has_pos:
        rows_per_block = x.shape[0] // GRID_W
        n_grid_rows = nc * rows_per_block
        r0 = c * rows_per_block
        x = x + _pos_block(ptab_ref, r0, x.shape[0])
        xp = xp + _pos_rows(ptab_ref, jnp.maximum(r0 - 1, 0), GRID_W - HALO, HALO)
        xn = xn + _pos_rows(ptab_ref, jnp.minimum(r0 + rows_per_block, n_grid_rows - 1), 0, HALO)
    gs = g_ref[...] * (1.0 + mod_ref[0, 1:2, :])
    sh = mod_ref[0, 0:1, :]
    h_all = jnp.concatenate([_rms(xp) * gs + sh, _rms(x) * gs + sh, _rms(xn) * gs + sh], axis=0)
    row = lax.broadcasted_iota(jnp.int32, (2 * SUBLANES, D_RNN), 0) & (SUBLANES - 1)
    carry = carry_ref[...]
    for q in range(n_sub):
        rows = slice(q * tc, (q + 1) * tc)
        e = h_all[q * tc:(q + 1) * tc + 2 * HALO]
        hp = _permute_in(e, ext_s, tc, n_tiles).astype(BF16)
        zg_ref[0, rows, :] = hp[2 * SUBLANES:2 * SUBLANES + tc]
        rec = _dot(hp, w_ref[...])
        if q == 0:
            head = jnp.where(row < jnp.where(c == 0, 1, 0), 0.0, rec[0:2 * SUBLANES])
            rec = jnp.concatenate([head, rec[2 * SUBLANES:]], axis=0)
        if q == n_sub - 1:
            tail = jnp.where(row > jnp.where(c == nc - 1, SUBLANES - 2, SUBLANES - 1), 0.0,
                             rec[(sub + 2) * SUBLANES:])
            rec = jnp.concatenate([rec[:(sub + 2) * SUBLANES], tail], axis=0)
        hu = rec[0:tc] * cw_ref[0:1, :]
        for k in range(1, CONV_W):
            hu = hu + rec[k * SUBLANES:k * SUBLANES + tc] * cw_ref[k:k + 1, :]
        hu = hu + cb_ref[...]
        hu_ref[0, rows, :] = hu
        ab = [_gate_ab(hu[:, n * LRU_BLOCK:(n + 1) * LRU_BLOCK], n, wg_ref, bg_ref, lam_ref)
              for n in range(N_LRU_BLOCKS)]
        hf, carry = _scan(jnp.concatenate([a for a, _ in ab], axis=1),
                          jnp.concatenate([b for _, b in ab], axis=1), carry, False)
        hf_ref[0, rows, :] = hf
    carry_ref[...] = carry
    sf_ref[0] = carry


def _ffn_head(x, mod_ref, pre_ref):
    sh = mod_ref[0, 3:4, :]
    sc = mod_ref[0, 4:5, :]
    return (_rms(x) * (pre_ref[...] * (1.0 + sc)) + sh).astype(BF16)


def _ffn_steps(h, wi_ref, wo_ref, act_ref, out):
    def chunk(n):
        def f():
            g = _dot(h, wi_ref[:, n * FF_CHUNK:(n + 1) * FF_CHUNK])
            v = _dot(h, wi_ref[:, D_FF + n * FF_CHUNK:D_FF + (n + 1) * FF_CHUNK])
            hg = 0.5 * g
            act_ref[:, n * FF_CHUNK:(n + 1) * FF_CHUNK] = ((hg + hg * jnp.tanh(hg)) * v).astype(BF16)
        return f

    def down():
        out.append(_dot(act_ref[...], wo_ref[...]))

    return [chunk(n) for n in range(N_FF_CHUNKS)] + [down]


def _ffn_tail(x, y, mod_ref, post_ref):
    return x + _rms(y) * (post_ref[...] * mod_ref[0, 5:6, :])


def _trace_interleaved(major, minor):
    done = 0
    for k, step in enumerate(major):
        step()
        while done < (k + 1) * len(minor) // len(major):
            minor[done]()
            done += 1


def _bwd_ffn_body(*refs, nc, tc, n_chunks, has_pos):
    if has_pos:
        (hu_ref, zg_ref, hf_ref, x_ref, ptab_ref, mod_ref, wg_ref, bg_ref, lam_ref, h0_ref, wo_ref, pg_ref,
         wgt_ref, fmod_ref, pre_ref, post_ref, wi_ref, wo2_ref, x2_ref, sb_ref,
         carry_ref, o_s, x1_s, act_ref, mixed_ref) = refs
    else:
        (hu_ref, zg_ref, hf_ref, x_ref, mod_ref, wg_ref, bg_ref, lam_ref, h0_ref, wo_ref, pg_ref,
         wgt_ref, fmod_ref, pre_ref, post_ref, wi_ref, wo2_ref, x2_ref, sb_ref,
         carry_ref, o_s, x1_s, act_ref, mixed_ref) = refs
    carry_ref = carry_ref.at[0:1]
    i = pl.program_id(0)
    c = jnp.minimum(i, n_chunks - 1) % nc
    n_sub = x_ref.shape[1] // tc

    @pl.when(c == 0)
    def _():
        carry_ref[...] = h0_ref[0]

    def mixer_steps():
        st = {}
        lasts = [None] * N_LRU_BLOCKS

        def gate_proj(n):
            def f():
                st['zg'] = _dot(zg_ref[0], wgt_ref[:, n * LRU_BLOCK:(n + 1) * LRU_BLOCK])
            return f

        def block(n, q):
            def f():
                rows = slice(q * tc, (q + 1) * tc)
                cols = slice(n * LRU_BLOCK, (n + 1) * LRU_BLOCK)
                a, b = _gate_ab(hu_ref[0, rows, cols], n, wg_ref, bg_ref, lam_ref)
                h_in = carry_ref[:, cols] if lasts[n] is None else lasts[n]
                hb, lasts[n] = _scan(a, b, h_in, True)
                mixed_ref[rows, cols] = ((hf_ref[0, rows, cols] + hb) * jax.nn.gelu(st['zg'][rows])).astype(BF16)
            return f

        def out_proj():
            st['y'] = _dot(mixed_ref[...], wo_ref[...])

        def residual():
            x = x_ref[0]
            if has_pos:
                x = x + _pos_block(ptab_ref, (nc - 1 - c) * (x.shape[0] // GRID_W), x.shape[0])
            r = _rms(st['y']) * (pg_ref[...] * mod_ref[0, 2:3, :])
            x1_s[i % 2] = x + jnp.concatenate(
                [_permute_out(r[q * tc:(q + 1) * tc], o_s.at[q], tc) for q in range(n_sub)], axis=0)
            h_last = jnp.concatenate(lasts, axis=1)
            carry_ref[...] = h_last
            sb_ref[0] = h_last

        steps = []
        for n in range(N_LRU_BLOCKS):
            steps += [gate_proj(n)] + [block(n, q) for q in range(n_sub - 1, -1, -1)]
        return steps + [out_proj, residual]

    def ffn_with(mixer):
        xf = x1_s[(i + 1) % 2]
        yf = []
        _trace_interleaved(_ffn_steps(_ffn_head(xf, fmod_ref, pre_ref), wi_ref, wo2_ref, act_ref, yf), mixer)
        x2_ref[0] = _ffn_tail(xf, yf[0], fmod_ref, post_ref)

    @pl.when(i == 0)
    def _():
        for step in mixer_steps():
            step()

    @pl.when(jnp.logical_and(i > 0, i < n_chunks))
    def _():
        ffn_with(mixer_steps())

    @pl.when(i == n_chunks)
    def _():
        ffn_with([])


def _pool_group(ext, gi, c, pw_ref, t_len):
    n = ext.shape[0]
    tc = n - 2 * HALO
    win = POOL_WINDOWS[gi]
    e = ext[:, gi * POOL_GROUP:(gi + 1) * POOL_GROUP]
    w = e + pltpu.roll(e, 1, 0)
    half = 1
    while 2 * half < win:
        w = pltpu.roll(w, half, 0) + pltpu.roll(w, n - half, 0)
        half *= 2
    t = c * tc + lax.broadcasted_iota(jnp.int32, (tc, POOL_GROUP), 0)
    cnt = (jnp.minimum(t + win // 2, t_len) - jnp.maximum(t - win // 2, 0)).astype(F32)
    pooled = w[HALO:n - HALO] / cnt - e[HALO:n - HALO]
    return _dot(pooled.astype(BF16), pw_ref[gi])


def _pool_ffn_body(x_ref, xp_ref, xn_ref, mod_ref, mpre_ref, mpost_ref, pw_ref, pb_ref, ps_ref,
                   fmod_ref, pre_ref, post_ref, wi_ref, wo_ref, o_ref, x3_s, act_ref, *, nc, t_len, n_chunks):
    i = pl.program_id(0)
    c = jnp.minimum(i, n_chunks - 1) % nc

    def mixer_steps():
        st = {}
        ys = [None] * len(POOL_WINDOWS)

        def pre_norm():
            gs = mpre_ref[...] * (1.0 + mod_ref[0, 1:2, :])
            sh = mod_ref[0, 0:1, :]
            st['ext'] = jnp.concatenate([jnp.where(c > 0, _rms(xp_ref[0]) * gs + sh, 0.0),
                                         _rms(x_ref[0]) * gs + sh,
                                         jnp.where(c < nc - 1, _rms(xn_ref[0]) * gs + sh, 0.0)], axis=0)

        def group(gi):
            def f():
                ys[gi] = _pool_group(st['ext'], gi, c, pw_ref, t_len)
            return f

        def residual():
            y = (jnp.concatenate(ys, axis=1) + pb_ref[...]) * ps_ref[...]
            x3_s[i % 2] = x_ref[0] + _rms(y) * (mpost_ref[...] * mod_ref[0, 2:3, :])

        return [pre_norm] + [group(gi) for gi in range(len(POOL_WINDOWS))] + [residual]

    def ffn_with(mixer):
        xf = x3_s[(i + 1) % 2]
        yf = []
        _trace_interleaved(_ffn_steps(_ffn_head(xf, fmod_ref, pre_ref), wi_ref, wo_ref, act_ref, yf), mixer)
        o_ref[0] = _ffn_tail(xf, yf[0], fmod_ref, post_ref)

    @pl.when(i == 0)
    def _():
        for step in mixer_steps():
            step()

    @pl.when(jnp.logical_and(i > 0, i < n_chunks))
    def _():
        ffn_with(mixer_steps())

    @pl.when(i == n_chunks)
    def _():
        ffn_with([])


def _const_spec(shape, single=False):
    nd = len(shape)
    if single:
        return pl.BlockSpec(shape, lambda *_: (0,) * nd, pipeline_mode=pl.Buffered(1))
    return pl.BlockSpec(shape, lambda *_: (0,) * nd)


def _ffn_specs(d):
    return [_const_spec((1, d)), _const_spec((1, d)),
            _const_spec((d, 2 * D_FF), single=True), _const_spec((D_FF, d), single=True)]


def _layer0(x, pos, mod, mod_row, h0, p, tc):
    bsz, t_len, d = x.shape
    nc = t_len // tc
    n_chunks = bsz * nc
    n_hblk = t_len // HALO
    has_pos = pos is not None
    act = jax.ShapeDtypeStruct((bsz, t_len, d), F32)
    st = jax.ShapeDtypeStruct((bsz, 1, d), F32)

    tcb = tc * min(FWD_CHUNKS, nc)
    hb8 = tcb // HALO
    prev_blk = lambda c: jnp.maximum(c * hb8 - 1, 0)
    next_blk = lambda c: jnp.minimum((c + 1) * hb8, n_hblk - 1)
    row_spec = pl.BlockSpec((1, d), lambda b, c: (0, 0))
    chunk = pl.BlockSpec((1, tcb, d), lambda b, c: (b, c, 0))
    state_spec = pl.BlockSpec((1, 1, d), lambda b, c: (b, 0, 0))
    pos_specs = [_const_spec(pos.shape)] if has_pos else []
    in_specs = ([chunk,
                 pl.BlockSpec((1, HALO, d), lambda b, c: (b, prev_blk(c), 0)),
                 pl.BlockSpec((1, HALO, d), lambda b, c: (b, next_blk(c), 0))]
                + pos_specs
                + [pl.BlockSpec((1, N_MOD, d), lambda b, c: (mod_row(b), 0, 0)), row_spec,
                   pl.BlockSpec((d, D_RNN), lambda b, c: (0, 1)), _const_spec((CONV_W, d)), row_spec,
                   _const_spec((N_LRU_BLOCKS, LRU_BLOCK, 2 * LRU_BLOCK)), _const_spec((2, d)), row_spec, state_spec])
    args = ([x, x, x] + ([pos] if has_pos else [])
            + [mod, p['mix_pre_g'], p['w_in'], p['conv_w_half'], p['conv_b_half'], p['wg'][0], p['bg_half'][0],
               p['lam'][0], h0[:, 0:1]])
    zg, hu, hf, sf = pl.pallas_call(
        functools.partial(_fwd_body, nc=t_len // tcb, tc=tc, has_pos=has_pos),
        grid=(bsz, t_len // tcb), in_specs=in_specs, out_specs=[chunk, chunk, chunk, state_spec],
        out_shape=[jax.ShapeDtypeStruct((bsz, t_len, d), BF16), act, act, st],
        scratch_shapes=[pltpu.VMEM((SUBLANES, d), F32),
                        pltpu.VMEM((N_SLABS, HALO + SUBLANES * _pitch(tc), LANES), F32)],
        compiler_params=pltpu.CompilerParams(
            dimension_semantics=("parallel", "arbitrary"), vmem_limit_bytes=VMEM_LIMIT),
        name="l0_fwd",
    )(*args)

    n_sub = min(BWD_CHUNKS, nc)
    tcb = tc * n_sub
    nb = t_len // tcb
    n_blocks = bsz * nb

    def mix_at(i):
        im = jnp.minimum(i, n_blocks - 1)
        return im // nb, nb - 1 - im % nb

    def ffn_at(i):
        return mix_at(jnp.maximum(i - 1, 0))

    mchunk = pl.BlockSpec((1, tcb, d), lambda i: (*mix_at(i), 0))
    in_specs = ([mchunk, mchunk, mchunk, mchunk]
                + pos_specs
                + [pl.BlockSpec((1, N_MOD, d), lambda i: (mod_row(mix_at(i)[0]), 0, 0)),
                   _const_spec((N_LRU_BLOCKS, LRU_BLOCK, 2 * LRU_BLOCK)), _const_spec((2, d)), _const_spec((1, d)),
                   pl.BlockSpec((1, 1, d), lambda i: (mix_at(i)[0], 0, 0)),
                   _const_spec((D_RNN, d)), _const_spec((1, d)),
                   pl.BlockSpec((d, D_RNN), lambda i: (0, 0), pipeline_mode=pl.Buffered(1)),
                   pl.BlockSpec((1, N_MOD, d), lambda i: (mod_row(ffn_at(i)[0]), 0, 0))]
                + _ffn_specs(d))
    args = ([hu, zg, hf, x] + ([pos] if has_pos else [])
            + [mod, p['wg'][1], p['bg_half'][1], p['lam'][1], h0[:, 1:2], p['w_out'], p['mix_post_g'], p['w_in'],
               mod, p['ffn_pre_g'], p['ffn_post_g'], p['ffn_w_in'], p['ffn_w_out']])
    x2, sb = pl.pallas_call(
        functools.partial(_bwd_ffn_body, nc=nb, tc=tc, n_chunks=n_blocks, has_pos=has_pos),
        grid=(n_blocks + 1,), in_specs=in_specs,
        out_specs=[pl.BlockSpec((1, tcb, d), lambda i: (*ffn_at(i), 0)),
                   pl.BlockSpec((1, 1, d), lambda i: (mix_at(i)[0], 0, 0))],
        out_shape=[act, st],
        scratch_shapes=[pltpu.VMEM((SUBLANES, d), F32),
                        pltpu.VMEM((n_sub, N_SLABS, SUBLANES * _pitch(tc), LANES), F32),
                        pltpu.VMEM((2, tcb, d), F32), pltpu.VMEM((tcb, D_FF), BF16), pltpu.VMEM((tcb, D_RNN), BF16)],
        compiler_params=pltpu.CompilerParams(dimension_semantics=("arbitrary",), vmem_limit_bytes=VMEM_LIMIT_BIG),
        name="l0_bwd_ffn",
    )(*args)
    return x2, jnp.concatenate([sf, sb], axis=1)


def _layer1(x, mod, mod_row, p, tc):
    bsz, t_len, d = x.shape
    nc = t_len // tc
    n_chunks = bsz * nc
    hb8 = tc // HALO
    n_hblk = t_len // HALO

    def mix_at(i):
        im = jnp.minimum(i, n_chunks - 1)
        return im // nc, im % nc

    def ffn_at(i):
        return mix_at(jnp.maximum(i - 1, 0))

    def halo_spec(blk_of):
        return pl.BlockSpec((1, HALO, d), lambda i: (mix_at(i)[0], blk_of(mix_at(i)[1]), 0))

    return pl.pallas_call(
        functools.partial(_pool_ffn_body, nc=nc, t_len=t_len, n_chunks=n_chunks),
        grid=(n_chunks + 1,),
        in_specs=[pl.BlockSpec((1, tc, d), lambda i: (*mix_at(i), 0)),
                  halo_spec(lambda c: jnp.maximum(c * hb8 - 1, 0)),
                  halo_spec(lambda c: jnp.minimum((c + 1) * hb8, n_hblk - 1)),
                  pl.BlockSpec((1, N_MOD, d), lambda i: (mod_row(mix_at(i)[0]), 0, 0)),
                  _const_spec((1, d)), _const_spec((1, d)),
                  _const_spec((len(POOL_WINDOWS), POOL_GROUP, POOL_GROUP)), _const_spec((1, d)), _const_spec((1, d)),
                  pl.BlockSpec((1, N_MOD, d), lambda i: (mod_row(ffn_at(i)[0]), 0, 0))] + _ffn_specs(d),
        out_specs=pl.BlockSpec((1, tc, d), lambda i: (*ffn_at(i), 0)),
        out_shape=jax.ShapeDtypeStruct((bsz, t_len, d), F32),
        scratch_shapes=[pltpu.VMEM((2, tc, d), F32), pltpu.VMEM((tc, D_FF), BF16)],
        compiler_params=pltpu.CompilerParams(dimension_semantics=("arbitrary",), vmem_limit_bytes=VMEM_LIMIT),
        name="l1_ffn",
    )(x, x, x, mod, p['mix_pre_g'], p['mix_post_g'], p['pool_w'], p['pool_b'], p['pool_scale'],
      mod, p['ffn_pre_g'], p['ffn_post_g'], p['ffn_w_in'], p['ffn_w_out'])


def _grid_pos_table(t_len):
    rows = t_len // GRID_W
    quarter = D_MODEL // 4
    omega = 1.0 / (POS_THETA ** (jnp.arange(quarter, dtype=F32) / quarter))
    ang = jnp.arange(max(rows, GRID_W), dtype=F32)[:, None] * omega[None, :]
    return jnp.concatenate([jnp.sin(ang), jnp.cos(ang)], axis=-1)


def _row(v):
    return v.reshape(1, -1)


def kernel(x_prompt, x_sample, state_l0_rglru, c, c_ctx, l0_mod_w, l0_mod_b, l0_mix_pre_g, l0_mix_post_g, l0_w_in, l0_conv_w, l0_conv_b, l0_gate_a_w, l0_gate_a_b, l0_gate_x_w, l0_gate_x_b, l0_lambda, l0_w_out, l0_ffn_pre_g, l0_ffn_post_g, l0_ffn_w_in, l0_ffn_w_out, l1_mod_w, l1_mod_b, l1_mix_pre_g, l1_mix_post_g, l1_pool_w, l1_pool_b, l1_pool_scale, l1_ffn_pre_g, l1_ffn_post_g, l1_ffn_w_in, l1_ffn_w_out):
    n_ctx, t_ctx, d = x_prompt.shape
    n_lat, t_lat, _ = x_sample.shape

    cond = jnp.concatenate(
        [c, c_ctx[None, :], jnp.zeros((MOD_ROWS - n_lat - 1, d), F32)], axis=0)
    mod0 = _modulation(cond, l0_mod_w, l0_mod_b)
    mod1 = _modulation(cond, l1_mod_w, l1_mod_b)

    p0 = dict(
        mix_pre_g=_row(l0_mix_pre_g), mix_post_g=_row(l0_mix_post_g), w_in=l0_w_in.astype(BF16),
        conv_w_half=0.5 * l0_conv_w, conv_b_half=_row(0.5 * l0_conv_b),
        wg=[jnp.concatenate([l0_gate_a_w[k], l0_gate_x_w[k]], axis=-1).astype(BF16) for k in range(2)],
        bg_half=[0.5 * jnp.stack([l0_gate_a_b[k], l0_gate_x_b[k]], axis=0) for k in range(2)],
        lam=[_row(l0_lambda[k]) for k in range(2)],
        w_out=l0_w_out.astype(BF16),
        ffn_pre_g=_row(l0_ffn_pre_g), ffn_post_g=_row(l0_ffn_post_g),
        ffn_w_in=l0_ffn_w_in.astype(BF16), ffn_w_out=l0_ffn_w_out.astype(BF16))
    p1 = dict(
        mix_pre_g=_row(l1_mix_pre_g), mix_post_g=_row(l1_mix_post_g), pool_w=l1_pool_w.astype(BF16),
        pool_b=_row(l1_pool_b), pool_scale=_row(l1_pool_scale),
        ffn_pre_g=_row(l1_ffn_pre_g), ffn_post_g=_row(l1_ffn_post_g),
        ffn_w_in=l1_ffn_w_in.astype(BF16), ffn_w_out=l1_ffn_w_out.astype(BF16))

    def run(x, pos, mod_row, h0):
        t_len = x.shape[1]
        x2, state = _layer0(x, pos, mod0, mod_row, h0, p0, min(CHUNK_L0, t_len))
        return _layer1(x2, mod1, mod_row, p1, min(CHUNK_L1, t_len)), state

    y_prompt, new_state = run(x_prompt, None, lambda b: CTX_ROW, jnp.zeros((n_ctx, 2, D_RNN), F32))
    y_sample, _ = run(x_sample, _grid_pos_table(t_lat), lambda b: b, state_l0_rglru)
    return y_prompt, y_sample, new_state
```

```python
import functools

import jax
import jax.numpy as jnp
from jax import lax
from jax.experimental import pallas as pl
from jax.experimental.pallas import tpu as pltpu

D_MODEL = 1024
D_RNN = D_MODEL
N_LRU_BLOCKS = 4
LRU_BLOCK = D_RNN // N_LRU_BLOCKS
CONV_W = 4
LRU_C = 8.0
POOL_WINDOWS = (2, 4, 8, 16)
POOL_GROUP = D_MODEL // len(POOL_WINDOWS)
D_FF = 2816
N_MOD = 6
EPS = 1e-6
POS_THETA = 10000.0
GRID_W = 64

LANES = 128
SUBLANES = 8
HALO = SUBLANES
N_SLABS = D_RNN // LANES
FF_CHUNK = 256
N_FF_CHUNKS = D_FF // FF_CHUNK
MOD_ROWS = 16
CTX_ROW = 8
MOD_TN = 1536
SQRT_FLOOR = 1e-36
CHUNK_L0 = 256
CHUNK_L1 = 512
FWD_CHUNKS = 2
BWD_CHUNKS = 2

VMEM_LIMIT = 56 * 1024 * 1024
VMEM_LIMIT_BIG = 62 * 1024 * 1024

F32 = jnp.float32
BF16 = jnp.bfloat16


def _dot(a, b):
    return jnp.dot(a, b, preferred_element_type=F32)


def _rms(x):
    return x * lax.rsqrt(jnp.mean(x * x, axis=-1, keepdims=True) + EPS)


def _sigmoid(x):
    return 1.0 / (1.0 + jnp.exp(-x))


def _mod_body(cond_ref, w0_ref, b0_ref, w1_ref, b1_ref, o0_ref, o1_ref, *, tiles):
    j = pl.program_id(0)
    c = cond_ref[...]
    s = (c * _sigmoid(c)).astype(BF16)

    @pl.when(j < tiles)
    def _():
        o0_ref[...] = _dot(s, w0_ref[...].astype(BF16)) + b0_ref[...]

    @pl.when(j >= tiles)
    def _():
        o1_ref[...] = _dot(s, w1_ref[...].astype(BF16)) + b1_ref[...]


def _modulation(cond, w0, b0, w1, b1):
    n = N_MOD * D_MODEL
    tiles = n // MOD_TN
    first = lambda j: (0, jnp.minimum(j, tiles - 1))
    second = lambda j: (0, jnp.maximum(j - tiles, 0))
    out = jax.ShapeDtypeStruct((MOD_ROWS, n), F32)
    o0, o1 = pl.pallas_call(
        functools.partial(_mod_body, tiles=tiles),
        grid=(2 * tiles,),
        in_specs=[
            pl.BlockSpec((MOD_ROWS, D_MODEL), lambda j: (0, 0)),
            pl.BlockSpec((D_MODEL, MOD_TN), first), pl.BlockSpec((1, MOD_TN), first),
            pl.BlockSpec((D_MODEL, MOD_TN), second), pl.BlockSpec((1, MOD_TN), second),
        ],
        out_specs=[pl.BlockSpec((MOD_ROWS, MOD_TN), first), pl.BlockSpec((MOD_ROWS, MOD_TN), second)],
        out_shape=[out, out],
        compiler_params=pltpu.CompilerParams(
            dimension_semantics=("arbitrary",), vmem_limit_bytes=VMEM_LIMIT),
        name="modulation",
    )(cond, w0, b0.reshape(1, n), w1, b1.reshape(1, n))
    return o0.reshape(MOD_ROWS, N_MOD, D_MODEL), o1.reshape(MOD_ROWS, N_MOD, D_MODEL)


def _gate_ab(hu, n, wg_ref, bg_ref, lam_ref):
    cols = slice(n * LRU_BLOCK, (n + 1) * LRU_BLOCK)
    o = _dot(hu.astype(BF16), wg_ref[n])
    t_r = jnp.tanh(o[:, :LRU_BLOCK] + bg_ref[0:1, cols])
    t_i = jnp.tanh(o[:, LRU_BLOCK:] + bg_ref[1:2, cols])
    nl = -lam_ref[:, cols]
    softplus = jnp.maximum(nl, 0.0) + jnp.log1p(jnp.exp(-jnp.abs(nl)))
    log_a = (t_r + 1.0) * ((-0.5 * LRU_C) * softplus)
    a = jnp.exp(log_a)
    s = jnp.tanh(log_a) * (-1.0 - a * a)
    root = s * lax.rsqrt(jnp.maximum(s, SQRT_FLOOR))
    b = root * (hu + hu * t_i)
    return a, b


def _scan(a, b, h_in, reverse):
    tc = a.shape[0]
    sub = tc // SUBLANES
    sub_id = lax.broadcasted_iota(jnp.int32, (SUBLANES, LANES), 0)
    steps = range(sub - 1, -1, -1) if reverse else range(sub)
    order = range(SUBLANES - 1, -1, -1) if reverse else range(SUBLANES)
    n_slabs = a.shape[1] // LANES
    hs = [[None] * sub for _ in range(n_slabs)]
    ps = [[None] * sub for _ in range(n_slabs)]
    h, p = [None] * n_slabs, [None] * n_slabs
    for j in steps:
        for k in range(n_slabs):
            av = a[j * SUBLANES:(j + 1) * SUBLANES, k * LANES:(k + 1) * LANES]
            bv = b[j * SUBLANES:(j + 1) * SUBLANES, k * LANES:(k + 1) * LANES]
            h[k] = bv if h[k] is None else av * h[k] + bv
            p[k] = av if p[k] is None else av * p[k]
            hs[k][j], ps[k][j] = h[k], p[k]
    slabs, last = [], []
    for k in range(n_slabs):
        carry = h_in[:, k * LANES:(k + 1) * LANES]
        carry_in = jnp.zeros((SUBLANES, LANES), F32)
        for s in order:
            carry_in = jnp.where(sub_id == s, carry, carry_in)
            carry = p[k][s:s + 1, :] * carry + h[k][s:s + 1, :]
        slabs.append(jnp.concatenate([ps[k][j] * carry_in + hs[k][j] for j in range(sub)], axis=0))
        last.append(carry)
    return jnp.concatenate(slabs, axis=1), jnp.concatenate(last, axis=1)


def _pitch(tc):
    return tc // SUBLANES + SUBLANES


def _permute_in(e, ext_s, tc, n_tiles):
    sub = tc // SUBLANES
    pitch = _pitch(tc)
    row = lax.broadcasted_iota(jnp.int32, (SUBLANES, LANES), 0)
    for k in range(N_SLABS):
        lanes = slice(k * LANES, (k + 1) * LANES)
        ext_s[k, 0:HALO, :] = e[0:HALO, lanes]
        for s in range(SUBLANES):
            base = HALO + s * pitch
            end = HALO + (s + 1) * sub
            ext_s[k, base:base + sub, :] = e[end - sub:end, lanes]
            ext_s[k, base + sub:base + pitch, :] = jnp.where(
                row < SUBLANES // 2, e[end:end + SUBLANES, lanes], e[end - SUBLANES:end, lanes])
    tiles = [jnp.concatenate([ext_s[k, pl.ds(HALO - 2 + q, SUBLANES, stride=pitch), :] for k in range(N_SLABS)],
                             axis=1) for q in range(n_tiles)]
    return jnp.concatenate(tiles, axis=0)


def _permute_out(r, o_s, tc):
    sub = tc // SUBLANES
    pitch = _pitch(tc)
    for k in range(N_SLABS):
        for j in range(sub):
            o_s[k, pl.ds(j, SUBLANES, stride=pitch), :] = r[j * SUBLANES:(j + 1) * SUBLANES, k * LANES:(k + 1) * LANES]
    return jnp.concatenate(
        [jnp.concatenate([o_s[k, s * pitch:s * pitch + sub, :] for s in range(SUBLANES)], axis=0)
         for k in range(N_SLABS)], axis=1)


def _pos_rows(ptab_ref, grid_row, col0, n):
    half = ptab_ref.shape[1]
    by_row = jnp.broadcast_to(ptab_ref[pl.ds(grid_row, 1), :], (n, half))
    return jnp.concatenate([by_row, ptab_ref[col0:col0 + n, :]], axis=1)


def _pos_block(ptab_ref, first_grid_row, n_rows):
    return jnp.concatenate([_pos_rows(ptab_ref, first_grid_row + k, 0, GRID_W) for k in range(n_rows // GRID_W)],
                           axis=0)


def _fwd_body(*refs, nc, tc, has_pos):
    if has_pos:
        (x_ref, xp_ref, xn_ref, ptab_ref, mod_ref, g_ref, w_ref, cw_ref, cb_ref, wg_ref, bg_ref,
         lam_ref, h0_ref, zg_ref, hu_ref, hf_ref, sf_ref, carry_ref, ext_s) = refs
    else:
        (x_ref, xp_ref, xn_ref, mod_ref, g_ref, w_ref, cw_ref, cb_ref, wg_ref, bg_ref,
         lam_ref, h0_ref, zg_ref, hu_ref, hf_ref, sf_ref, carry_ref, ext_s) = refs
    carry_ref = carry_ref.at[0:1]
    c = pl.program_id(1)
    n_sub = x_ref.shape[1] // tc
    sub = tc // SUBLANES
    n_tiles = sub + CONV_W

    @pl.when(c == 0)
    def _():
        carry_ref[...] = h0_ref[0]

    x, xp, xn = x_ref[0], xp_ref[0], xn_ref[0]
    if has_pos:
        rows_per_block = x.shape[0] // GRID_W
        n_grid_rows = nc * rows_per_block
        r0 = c * rows_per_block
        x = x + _pos_block(ptab_ref, r0, x.shape[0])
        xp = xp + _pos_rows(ptab_ref, jnp.maximum(r0 - 1, 0), GRID_W - HALO, HALO)
        xn = xn + _pos_rows(ptab_ref, jnp.minimum(r0 + rows_per_block, n_grid_rows - 1), 0, HALO)
    gs = g_ref[...] * (1.0 + mod_ref[0, 1:2, :])
    sh = mod_ref[0, 0:1, :]
    h_all = jnp.concatenate([_rms(xp) * gs + sh, _rms(x) * gs + sh, _rms(xn) * gs + sh], axis=0)
    row = lax.broadcasted_iota(jnp.int32, (2 * SUBLANES, D_RNN), 0) & (SUBLANES - 1)
    carry = carry_ref[...]
    for q in range(n_sub):
        rows = slice(q * tc, (q + 1) * tc)
        e = h_all[q * tc:(q + 1) * tc + 2 * HALO]
        hp = _permute_in(e, ext_s, tc, n_tiles).astype(BF16)
        zg_ref[0, rows, :] = _dot(hp[2 * SUBLANES:2 * SUBLANES + tc], w_ref[:, :D_RNN])
        rec = _dot(hp, w_ref[:, D_RNN:])
        if q == 0:
            head = jnp.where(row < jnp.where(c == 0, 1, 0), 0.0, rec[0:2 * SUBLANES])
            rec = jnp.concatenate([head, rec[2 * SUBLANES:]], axis=0)
        if q == n_sub - 1:
            tail = jnp.where(row > jnp.where(c == nc - 1, SUBLANES - 2, SUBLANES - 1), 0.0,
                             rec[(sub + 2) * SUBLANES:])
            rec = jnp.concatenate([rec[:(sub + 2) * SUBLANES], tail], axis=0)
        hu = rec[0:tc] * cw_ref[0:1, :]
        for k in range(1, CONV_W):
            hu = hu + rec[k * SUBLANES:k * SUBLANES + tc] * cw_ref[k:k + 1, :]
        hu = hu + cb_ref[...]
        hu_ref[0, rows, :] = hu
        ab = [_gate_ab(hu[:, n * LRU_BLOCK:(n + 1) * LRU_BLOCK], n, wg_ref, bg_ref, lam_ref)
              for n in range(N_LRU_BLOCKS)]
        hf, carry = _scan(jnp.concatenate([a for a, _ in ab], axis=1),
                          jnp.concatenate([b for _, b in ab], axis=1), carry, False)
        hf_ref[0, rows, :] = hf
    carry_ref[...] = carry
    sf_ref[0] = carry


def _ffn_head(x, mod_ref, pre_ref):
    sh = mod_ref[0, 3:4, :]
    sc = mod_ref[0, 4:5, :]
    return (_rms(x) * (pre_ref[...] * (1.0 + sc)) + sh).astype(BF16)


def _ffn_steps(h, wi_ref, wo_ref, act_ref, out):
    def chunk(n):
        def f():
            g = _dot(h, wi_ref[:, n * FF_CHUNK:(n + 1) * FF_CHUNK])
            v = _dot(h, wi_ref[:, D_FF + n * FF_CHUNK:D_FF + (n + 1) * FF_CHUNK])
            hg = 0.5 * g
            act_ref[:, n * FF_CHUNK:(n + 1) * FF_CHUNK] = ((hg + hg * jnp.tanh(hg)) * v).astype(BF16)
        return f

    def down():
        out.append(_dot(act_ref[...], wo_ref[...]))

    return [chunk(n) for n in range(N_FF_CHUNKS)] + [down]


def _ffn_tail(x, y, mod_ref, post_ref):
    return x + _rms(y) * (post_ref[...] * mod_ref[0, 5:6, :])


def _trace_interleaved(major, minor):
    done = 0
    for k, step in enumerate(major):
        step()
        while done < (k + 1) * len(minor) // len(major):
            minor[done]()
            done += 1


def _bwd_ffn_body(*refs, nc, tc, n_chunks, has_pos):
    if has_pos:
        (hu_ref, zg_ref, hf_ref, x_ref, ptab_ref, mod_ref, wg_ref, bg_ref, lam_ref, h0_ref, wo_ref, pg_ref,
         fmod_ref, pre_ref, post_ref, wi_ref, wo2_ref, x2_ref, sb_ref,
         carry_ref, o_s, x1_s, act_ref, mixed_ref) = refs
    else:
        (hu_ref, zg_ref, hf_ref, x_ref, mod_ref, wg_ref, bg_ref, lam_ref, h0_ref, wo_ref, pg_ref,
         fmod_ref, pre_ref, post_ref, wi_ref, wo2_ref, x2_ref, sb_ref,
         carry_ref, o_s, x1_s, act_ref, mixed_ref) = refs
    carry_ref = carry_ref.at[0:1]
    i = pl.program_id(0)
    c = jnp.minimum(i, n_chunks - 1) % nc
    n_sub = x_ref.shape[1] // tc

    @pl.when(c == 0)
    def _():
        carry_ref[...] = h0_ref[0]

    def mixer_steps():
        st = {}
        lasts = [None] * N_LRU_BLOCKS

        def block(n, q):
            def f():
                rows = slice(q * tc, (q + 1) * tc)
                cols = slice(n * LRU_BLOCK, (n + 1) * LRU_BLOCK)
                a, b = _gate_ab(hu_ref[0, rows, cols], n, wg_ref, bg_ref, lam_ref)
                h_in = carry_ref[:, cols] if lasts[n] is None else lasts[n]
                hb, lasts[n] = _scan(a, b, h_in, True)
                mixed_ref[rows, cols] = (
                    (hf_ref[0, rows, cols] + hb) * jax.nn.gelu(zg_ref[0, rows, cols])).astype(BF16)
            return f

        def out_proj():
            st['y'] = _dot(mixed_ref[...], wo_ref[...])

        def residual():
            x = x_ref[0]
            if has_pos:
                x = x + _pos_block(ptab_ref, (nc - 1 - c) * (x.shape[0] // GRID_W), x.shape[0])
            r = _rms(st['y']) * (pg_ref[...] * mod_ref[0, 2:3, :])
            x1_s[i % 2] = x + jnp.concatenate(
                [_permute_out(r[q * tc:(q + 1) * tc], o_s.at[q], tc) for q in range(n_sub)], axis=0)
            h_last = jnp.concatenate(lasts, axis=1)
            carry_ref[...] = h_last
            sb_ref[0] = h_last

        return ([block(n, q) for q in range(n_sub - 1, -1, -1) for n in range(N_LRU_BLOCKS)]
                + [out_proj, residual])

    def ffn_with(mixer):
        xf = x1_s[(i + 1) % 2]
        yf = []
        _trace_interleaved(_ffn_steps(_ffn_head(xf, fmod_ref, pre_ref), wi_ref, wo2_ref, act_ref, yf), mixer)
        x2_ref[0] = _ffn_tail(xf, yf[0], fmod_ref, post_ref)

    @pl.when(i == 0)
    def _():
        for step in mixer_steps():
            step()

    @pl.when(jnp.logical_and(i > 0, i < n_chunks))
    def _():
        ffn_with(mixer_steps())

    @pl.when(i == n_chunks)
    def _():
        ffn_with([])


def _pool_group(ext, gi, c, pw_ref, t_len):
    n = ext.shape[0]
    tc = n - 2 * HALO
    win = POOL_WINDOWS[gi]
    e = ext[:, gi * POOL_GROUP:(gi + 1) * POOL_GROUP]
    w = e + pltpu.roll(e, 1, 0)
    half = 1
    while 2 * half < win:
        w = pltpu.roll(w, half, 0) + pltpu.roll(w, n - half, 0)
        half *= 2
    t = c * tc + lax.broadcasted_iota(jnp.int32, (tc, POOL_GROUP), 0)
    cnt = (jnp.minimum(t + win // 2, t_len) - jnp.maximum(t - win // 2, 0)).astype(F32)
    pooled = w[HALO:n - HALO] / cnt - e[HALO:n - HALO]
    return _dot(pooled.astype(BF16), pw_ref[gi])


def _pool_ffn_body(x_ref, xp_ref, xn_ref, mod_ref, mpre_ref, mpost_ref, pw_ref, pb_ref, ps_ref,
                   fmod_ref, pre_ref, post_ref, wi_ref, wo_ref, o_ref, x3_s, act_ref, *, nc, t_len, n_chunks):
    i = pl.program_id(0)
    c = jnp.minimum(i, n_chunks - 1) % nc

    def mixer_steps():
        st = {}
        ys = [None] * len(POOL_WINDOWS)

        def pre_norm():
            gs = mpre_ref[...] * (1.0 + mod_ref[0, 1:2, :])
            sh = mod_ref[0, 0:1, :]
            st['ext'] = jnp.concatenate([jnp.where(c > 0, _rms(xp_ref[0]) * gs + sh, 0.0),
                                         _rms(x_ref[0]) * gs + sh,
                                         jnp.where(c < nc - 1, _rms(xn_ref[0]) * gs + sh, 0.0)], axis=0)

        def group(gi):
            def f():
                ys[gi] = _pool_group(st['ext'], gi, c, pw_ref, t_len)
            return f

        def residual():
            y = (jnp.concatenate(ys, axis=1) + pb_ref[...]) * ps_ref[...]
            x3_s[i % 2] = x_ref[0] + _rms(y) * (mpost_ref[...] * mod_ref[0, 2:3, :])

        return [pre_norm] + [group(gi) for gi in range(len(POOL_WINDOWS))] + [residual]

    def ffn_with(mixer):
        xf = x3_s[(i + 1) % 2]
        yf = []
        _trace_interleaved(_ffn_steps(_ffn_head(xf, fmod_ref, pre_ref), wi_ref, wo_ref, act_ref, yf), mixer)
        o_ref[0] = _ffn_tail(xf, yf[0], fmod_ref, post_ref)

    @pl.when(i == 0)
    def _():
        for step in mixer_steps():
            step()

    @pl.when(jnp.logical_and(i > 0, i < n_chunks))
    def _():
        ffn_with(mixer_steps())

    @pl.when(i == n_chunks)
    def _():
        ffn_with([])


def _const_spec(shape, single=False):
    nd = len(shape)
    if single:
        return pl.BlockSpec(shape, lambda *_: (0,) * nd, pipeline_mode=pl.Buffered(1))
    return pl.BlockSpec(shape, lambda *_: (0,) * nd)


def _ffn_specs(d):
    return [_const_spec((1, d)), _const_spec((1, d)),
            _const_spec((d, 2 * D_FF), single=True), _const_spec((D_FF, d), single=True)]


def _layer0(x, pos, mod, mod_row, h0, p, tc):
    bsz, t_len, d = x.shape
    nc = t_len // tc
    n_chunks = bsz * nc
    n_hblk = t_len // HALO
    has_pos = pos is not None
    act = jax.ShapeDtypeStruct((bsz, t_len, d), F32)
    st = jax.ShapeDtypeStruct((bsz, 1, d), F32)

    tcb = tc * min(FWD_CHUNKS, nc)
    hb8 = tcb // HALO
    prev_blk = lambda c: jnp.maximum(c * hb8 - 1, 0)
    next_blk = lambda c: jnp.minimum((c + 1) * hb8, n_hblk - 1)
    row_spec = pl.BlockSpec((1, d), lambda b, c: (0, 0))
    chunk = pl.BlockSpec((1, tcb, d), lambda b, c: (b, c, 0))
    state_spec = pl.BlockSpec((1, 1, d), lambda b, c: (b, 0, 0))
    pos_specs = [_const_spec(pos.shape)] if has_pos else []
    in_specs = ([chunk,
                 pl.BlockSpec((1, HALO, d), lambda b, c: (b, prev_blk(c), 0)),
                 pl.BlockSpec((1, HALO, d), lambda b, c: (b, next_blk(c), 0))]
                + pos_specs
                + [pl.BlockSpec((1, N_MOD, d), lambda b, c: (mod_row(b), 0, 0)), row_spec,
                   _const_spec((d, 2 * D_RNN)), _const_spec((CONV_W, d)), row_spec,
                   _const_spec((N_LRU_BLOCKS, LRU_BLOCK, 2 * LRU_BLOCK)), _const_spec((2, d)), row_spec, state_spec])
    args = ([x, x, x] + ([pos] if has_pos else [])
            + [mod, p['mix_pre_g'], p['w_in'], p['conv_w_half'], p['conv_b_half'], p['wg'][0], p['bg_half'][0],
               p['lam'][0], h0[:, 0:1]])
    zg, hu, hf, sf = pl.pallas_call(
        functools.partial(_fwd_body, nc=t_len // tcb, tc=tc, has_pos=has_pos),
        grid=(bsz, t_len // tcb), in_specs=in_specs, out_specs=[chunk, chunk, chunk, state_spec],
        out_shape=[act, act, act, st],
        scratch_shapes=[pltpu.VMEM((SUBLANES, d), F32),
                        pltpu.VMEM((N_SLABS, HALO + SUBLANES * _pitch(tc), LANES), F32)],
        compiler_params=pltpu.CompilerParams(
            dimension_semantics=("parallel", "arbitrary"), vmem_limit_bytes=VMEM_LIMIT),
        name="l0_fwd",
    )(*args)

    n_sub = min(BWD_CHUNKS, nc)
    tcb = tc * n_sub
    nb = t_len // tcb
    n_blocks = bsz * nb

    def mix_at(i):
        im = jnp.minimum(i, n_blocks - 1)
        return im // nb, nb - 1 - im % nb

    def ffn_at(i):
        return mix_at(jnp.maximum(i - 1, 0))

    mchunk = pl.BlockSpec((1, tcb, d), lambda i: (*mix_at(i), 0))
    in_specs = ([mchunk, mchunk, mchunk, mchunk]
                + pos_specs
                + [pl.BlockSpec((1, N_MOD, d), lambda i: (mod_row(mix_at(i)[0]), 0, 0)),
                   _const_spec((N_LRU_BLOCKS, LRU_BLOCK, 2 * LRU_BLOCK)), _const_spec((2, d)), _const_spec((1, d)),
                   pl.BlockSpec((1, 1, d), lambda i: (mix_at(i)[0], 0, 0)),
                   _const_spec((D_RNN, d)), _const_spec((1, d)),
                   pl.BlockSpec((1, N_MOD, d), lambda i: (mod_row(ffn_at(i)[0]), 0, 0))]
                + _ffn_specs(d))
    args = ([hu, zg, hf, x] + ([pos] if has_pos else [])
            + [mod, p['wg'][1], p['bg_half'][1], p['lam'][1], h0[:, 1:2], p['w_out'], p['mix_post_g'],
               mod, p['ffn_pre_g'], p['ffn_post_g'], p['ffn_w_in'], p['ffn_w_out']])
    x2, sb = pl.pallas_call(
        functools.partial(_bwd_ffn_body, nc=nb, tc=tc, n_chunks=n_blocks, has_pos=has_pos),
        grid=(n_blocks + 1,), in_specs=in_specs,
        out_specs=[pl.BlockSpec((1, tcb, d), lambda i: (*ffn_at(i), 0)),
                   pl.BlockSpec((1, 1, d), lambda i: (mix_at(i)[0], 0, 0))],
        out_shape=[act, st],
        scratch_shapes=[pltpu.VMEM((SUBLANES, d), F32),
                        pltpu.VMEM((n_sub, N_SLABS, SUBLANES * _pitch(tc), LANES), F32),
                        pltpu.VMEM((2, tcb, d), F32), pltpu.VMEM((tcb, D_FF), BF16), pltpu.VMEM((tcb, D_RNN), BF16)],
        compiler_params=pltpu.CompilerParams(dimension_semantics=("arbitrary",), vmem_limit_bytes=VMEM_LIMIT_BIG),
        name="l0_bwd_ffn",
    )(*args)
    return x2, jnp.concatenate([sf, sb], axis=1)


def _layer1(x, mod, mod_row, p, tc):
    bsz, t_len, d = x.shape
    nc = t_len // tc
    n_chunks = bsz * nc
    hb8 = tc // HALO
    n_hblk = t_len // HALO

    def mix_at(i):
        im = jnp.minimum(i, n_chunks - 1)
        return im // nc, im % nc

    def ffn_at(i):
        return mix_at(jnp.maximum(i - 1, 0))

    def halo_spec(blk_of):
        return pl.BlockSpec((1, HALO, d), lambda i: (mix_at(i)[0], blk_of(mix_at(i)[1]), 0))

    return pl.pallas_call(
        functools.partial(_pool_ffn_body, nc=nc, t_len=t_len, n_chunks=n_chunks),
        grid=(n_chunks + 1,),
        in_specs=[pl.BlockSpec((1, tc, d), lambda i: (*mix_at(i), 0)),
                  halo_spec(lambda c: jnp.maximum(c * hb8 - 1, 0)),
                  halo_spec(lambda c: jnp.minimum((c + 1) * hb8, n_hblk - 1)),
                  pl.BlockSpec((1, N_MOD, d), lambda i: (mod_row(mix_at(i)[0]), 0, 0)),
                  _const_spec((1, d)), _const_spec((1, d)),
                  _const_spec((len(POOL_WINDOWS), POOL_GROUP, POOL_GROUP)), _const_spec((1, d)), _const_spec((1, d)),
                  pl.BlockSpec((1, N_MOD, d), lambda i: (mod_row(ffn_at(i)[0]), 0, 0))] + _ffn_specs(d),
        out_specs=pl.BlockSpec((1, tc, d), lambda i: (*ffn_at(i), 0)),
        out_shape=jax.ShapeDtypeStruct((bsz, t_len, d), F32),
        scratch_shapes=[pltpu.VMEM((2, tc, d), F32), pltpu.VMEM((tc, D_FF), BF16)],
        compiler_params=pltpu.CompilerParams(dimension_semantics=("arbitrary",), vmem_limit_bytes=VMEM_LIMIT),
        name="l1_ffn",
    )(x, x, x, mod, p['mix_pre_g'], p['mix_post_g'], p['pool_w'], p['pool_b'], p['pool_scale'],
      mod, p['ffn_pre_g'], p['ffn_post_g'], p['ffn_w_in'], p['ffn_w_out'])


def _grid_pos_table(t_len):
    rows = t_len // GRID_W
    quarter = D_MODEL // 4
    omega = 1.0 / (POS_THETA ** (jnp.arange(quarter, dtype=F32) / quarter))
    ang = jnp.arange(max(rows, GRID_W), dtype=F32)[:, None] * omega[None, :]
    return jnp.concatenate([jnp.sin(ang), jnp.cos(ang)], axis=-1)


def _row(v):
    return v.reshape(1, -1)


def kernel(x_prompt, x_sample, state_l0_rglru, c, c_ctx, l0_mod_w, l0_mod_b, l0_mix_pre_g, l0_mix_post_g, l0_w_in, l0_conv_w, l0_conv_b, l0_gate_a_w, l0_gate_a_b, l0_gate_x_w, l0_gate_x_b, l0_lambda, l0_w_out, l0_ffn_pre_g, l0_ffn_post_g, l0_ffn_w_in, l0_ffn_w_out, l1_mod_w, l1_mod_b, l1_mix_pre_g, l1_mix_post_g, l1_pool_w, l1_pool_b, l1_pool_scale, l1_ffn_pre_g, l1_ffn_post_g, l1_ffn_w_in, l1_ffn_w_out):
    n_ctx, t_ctx, d = x_prompt.shape
    n_lat, t_lat, _ = x_sample.shape

    cond = jnp.concatenate(
        [c, c_ctx[None, :], jnp.zeros((MOD_ROWS - n_lat - 1, d), F32)], axis=0)
    mod0, mod1 = _modulation(cond, l0_mod_w, l0_mod_b, l1_mod_w, l1_mod_b)

    p0 = dict(
        mix_pre_g=_row(l0_mix_pre_g), mix_post_g=_row(l0_mix_post_g), w_in=l0_w_in.astype(BF16),
        conv_w_half=0.5 * l0_conv_w, conv_b_half=_row(0.5 * l0_conv_b),
        wg=[jnp.concatenate([l0_gate_a_w[k], l0_gate_x_w[k]], axis=-1).astype(BF16) for k in range(2)],
        bg_half=[0.5 * jnp.stack([l0_gate_a_b[k], l0_gate_x_b[k]], axis=0) for k in range(2)],
        lam=[_row(l0_lambda[k]) for k in range(2)],
        w_out=l0_w_out.astype(BF16),
        ffn_pre_g=_row(l0_ffn_pre_g), ffn_post_g=_row(l0_ffn_post_g),
        ffn_w_in=l0_ffn_w_in.astype(BF16), ffn_w_out=l0_ffn_w_out.astype(BF16))
    p1 = dict(
        mix_pre_g=_row(l1_mix_pre_g), mix_post_g=_row(l1_mix_post_g), pool_w=l1_pool_w.astype(BF16),
        pool_b=_row(l1_pool_b), pool_scale=_row(l1_pool_scale),
        ffn_pre_g=_row(l1_ffn_pre_g), ffn_post_g=_row(l1_ffn_post_g),
        ffn_w_in=l1_ffn_w_in.astype(BF16), ffn_w_out=l1_ffn_w_out.astype(BF16))

    def run(x, pos, mod_row, h0):
        t_len = x.shape[1]
        x2, state = _layer0(x, pos, mod0, mod_row, h0, p0, min(CHUNK_L0, t_len))
        return _layer1(x2, mod1, mod_row, p1, min(CHUNK_L1, t_len)), state

    y_prompt, new_state = run(x_prompt, None, lambda b: CTX_ROW, jnp.zeros((n_ctx, 2, D_RNN), F32))
    y_sample, _ = run(x_sample, _grid_pos_table(t_lat), lambda b: b, state_l0_rglru)
    return y_prompt, y_sample, new_state
```
